```python
import math, functools
import jax, jax.numpy as jnp
from jax import lax
import numpy as np

D_MODEL = 1024
BATCH = 2
SEQ = 8192
DEPTH = 2
DEC_BATCH = 32
DEC_SEQ = 4
PAST_LEN = 16384
PAGE_SIZE = 128

N_HEADS = 8
HEAD_DIM = 64
ATT_WIDTH = N_HEADS * HEAD_DIM
MOBA_BLOCK = 256
MOBA_TOPK = 3
Q_BLOCK = 128
N_BUCKETS = 32
REL_MAX_DIST = 128
POOL_WINDOWS = (2, 4, 8, 16)
POOL_GROUPS = len(POOL_WINDOWS)
POOL_WIDTH = D_MODEL // 2
POOL_GW = POOL_WIDTH // POOL_GROUPS
POOL_BUF = max(POOL_WINDOWS) - 1
IN_WIDTH = POOL_WIDTH + 3 * ATT_WIDTH + 2 * D_MODEL
N_EXPERTS = 16
N_GROUPS = 4
EXPERTS_PER_GROUP = N_EXPERTS // N_GROUPS
TOP_K = 2
D_EXPERT = 512
EXPERT_ROWS = 128
DEEPNORM_ALPHA = (2 * DEPTH) ** 0.25
DEEPNORM_BETA = (8 * DEPTH) ** -0.25
LN_EPS = 1e-5

kernel_name = "pool_moba_gated_moe_decoder_step"


def layer_norm(x, g, b):
    xf = x.astype(jnp.float32)
    mu = xf.mean(-1, keepdims=True)
    var = jnp.square(xf - mu).mean(-1, keepdims=True)
    return ((xf - mu) * lax.rsqrt(var + LN_EPS) * g + b).astype(x.dtype)


def rel_bucket(dist):
    n = jnp.maximum(dist, 0)
    max_exact = N_BUCKETS // 2
    nf = jnp.maximum(n, 1).astype(jnp.float32)
    large = max_exact + (jnp.log(nf / max_exact) / math.log(REL_MAX_DIST / max_exact)
                         * (N_BUCKETS - max_exact)).astype(jnp.int32)
    large = jnp.minimum(large, N_BUCKETS - 1)
    return jnp.where(n < max_exact, n, large)


def pool_branch(p, prev, pos0, pool_w, pool_scale):
    B, T, _ = p.shape
    prev = prev.astype(p.dtype)
    z = jnp.concatenate([jnp.zeros((B, 1, POOL_WIDTH), p.dtype), prev, p], axis=1).astype(jnp.float32)
    cs = jnp.cumsum(z, axis=1)
    end = cs[:, POOL_BUF + 1:]
    pos = pos0 + jnp.arange(T)
    means = []
    for g, w in enumerate(POOL_WINDOWS):
        c = slice(g * POOL_GW, (g + 1) * POOL_GW)
        start = cs[:, POOL_BUF + 1 - w: POOL_BUF + 1 - w + T, c]
        cnt = jnp.minimum(pos + 1, w).astype(jnp.float32)[None, :, None]
        means.append((end[..., c] - start) / cnt)
    d = (jnp.concatenate(means, -1) - p.astype(jnp.float32)).astype(p.dtype)
    d = d.reshape(B, T, POOL_GROUPS, POOL_GW)
    y = jnp.einsum('btgc,gce->btge', d, pool_w).reshape(B, T, POOL_WIDTH) * pool_scale
    new_prev = jnp.concatenate([prev, p], axis=1)[:, -POOL_BUF:]
    return y, new_prev


def moba_core(q, q_pos, kmean, gather_sel, own_k, own_v, own_pos, rel_bias):
    B, Tq = q.shape[:2]
    n_past = q_pos // MOBA_BLOCK
    head = jnp.arange(N_HEADS)
    causal = own_pos[None, :] <= q_pos[:, None]
    own_bias = rel_bias[rel_bucket(q_pos[:, None] - own_pos[None, :])].transpose(0, 2, 1)
    lo = jnp.einsum('bqhd,bkhd->bqhk', q, own_k).astype(jnp.float32) + own_bias
    lo = jnp.where(causal[:, None, :], lo, -jnp.inf)
    ks = min(MOBA_TOPK, kmean.shape[1])
    if ks == 0:
        p = jax.nn.softmax(lo, axis=-1).astype(own_v.dtype)
        return jnp.einsum('bqhk,bkhd->bqhd', p, own_v)
    gs = jnp.einsum('bqhd,bnhd->bqhn', q.astype(jnp.float32), kmean)
    blk = jnp.arange(kmean.shape[1])
    gs = jnp.where(blk[None, None, None, :] < n_past[None, :, None, None], gs, -jnp.inf)
    idx = lax.top_k(gs, ks)[1]
    valid = idx < n_past[None, :, None, None]
    k_sel, v_sel = gather_sel(idx)
    k_pos = idx[..., None] * MOBA_BLOCK + jnp.arange(MOBA_BLOCK)
    sel_bias = rel_bias[rel_bucket(q_pos[None, :, None, None, None] - k_pos),
                        head[None, None, :, None, None]]
    ls = jnp.einsum('bqhd,bqhkjd->bqhkj', q, k_sel).astype(jnp.float32) + sel_bias
    n_sel = ks * MOBA_BLOCK
    ls = jnp.where(valid[..., None], ls, -jnp.inf).reshape(B, Tq, N_HEADS, n_sel)
    p = jax.nn.softmax(jnp.concatenate([ls, lo], axis=-1), axis=-1).astype(own_v.dtype)
    out = jnp.einsum('bqhj,bqhjd->bqhd', p[..., :n_sel],
                     v_sel.reshape(B, Tq, N_HEADS, n_sel, HEAD_DIM))
    return out + jnp.einsum('bqhk,bkhd->bqhd', p[..., n_sel:], own_v)


def gather_blocks(blocks, idx):
    bi = jnp.arange(idx.shape[0])[:, None, None, None]
    hi = jnp.arange(N_HEADS)[None, None, :, None]
    return blocks[bi, idx, :, hi]


def gather_paged_blocks(pool, layer, page_table, idx):
    ppb = MOBA_BLOCK // PAGE_SIZE
    logical = idx[..., None] * ppb + jnp.arange(ppb)
    phys = page_table[jnp.arange(idx.shape[0])[:, None, None, None, None], logical]
    hi = jnp.arange(N_HEADS)[None, None, :, None, None]
    rows = pool[phys, :, layer, hi]
    return rows.reshape(idx.shape + (MOBA_BLOCK, HEAD_DIM))


def moba_prompt(q, k, v, *, rel_bias):
    B, T = q.shape[:2]
    nb = -(-T // MOBA_BLOCK)
    pad = ((0, 0), (0, nb * MOBA_BLOCK - T), (0, 0), (0, 0))
    kb = jnp.pad(k, pad).reshape(B, nb, MOBA_BLOCK, N_HEADS, HEAD_DIM)
    vb = jnp.pad(v, pad).reshape(B, nb, MOBA_BLOCK, N_HEADS, HEAD_DIM)
    kmean = kb.astype(jnp.float32).mean(axis=2)

    def gather_sel(idx):
        return gather_blocks(kb, idx), gather_blocks(vb, idx)

    def step(qb):
        start = qb * Q_BLOCK
        qs = lax.dynamic_slice_in_dim(q, start, Q_BLOCK, axis=1)
        q_pos = start + jnp.arange(Q_BLOCK)
        own = start // MOBA_BLOCK
        own_k = lax.dynamic_index_in_dim(kb, own, axis=1, keepdims=False)
        own_v = lax.dynamic_index_in_dim(vb, own, axis=1, keepdims=False)
        own_pos = own * MOBA_BLOCK + jnp.arange(MOBA_BLOCK)
        return moba_core(qs, q_pos, kmean, gather_sel, own_k, own_v, own_pos, rel_bias)

    out = lax.map(step, jnp.arange(T // Q_BLOCK))
    return out.transpose(1, 0, 2, 3, 4).reshape(B, T, N_HEADS, HEAD_DIM)


def moba_sample(q, k, v, *, cache_k, cache_v, page_table, layer, rel_bias):
    DB, TS = q.shape[:2]
    past_len = page_table.shape[1] * PAGE_SIZE
    nbp = past_len // MOBA_BLOCK
    k_past = cache_k[page_table, :, layer].reshape(DB, past_len, N_HEADS, HEAD_DIM)
    kmean = k_past[:, :nbp * MOBA_BLOCK].reshape(
        DB, nbp, MOBA_BLOCK, N_HEADS, HEAD_DIM).astype(jnp.float32).mean(axis=2)
    own_start = nbp * MOBA_BLOCK
    v_tail = cache_v[page_table[:, own_start // PAGE_SIZE:], :, layer].reshape(
        DB, past_len - own_start, N_HEADS, HEAD_DIM)
    own_k = jnp.concatenate([k_past[:, own_start:].astype(k.dtype), k], axis=1)
    own_v = jnp.concatenate([v_tail.astype(v.dtype), v], axis=1)
    own_pos = own_start + jnp.arange(past_len - own_start + TS)
    q_pos = past_len + jnp.arange(TS)

    def gather_sel(idx):
        return (gather_paged_blocks(cache_k, layer, page_table, idx),
                gather_paged_blocks(cache_v, layer, page_table, idx))

    return moba_core(q, q_pos, kmean, gather_sel, own_k, own_v, own_pos, rel_bias)


def moe_ffn(h, w_router, b_router, w_gate, w_up, w_down):
    N, D = h.shape
    logits = (h @ w_router).astype(jnp.float32) + b_router.astype(jnp.float32)
    probs = jax.nn.softmax(logits, axis=-1).reshape(N, N_GROUPS, EXPERTS_PER_GROUP)
    group_score = lax.top_k(probs, TOP_K)[0].sum(-1)
    g = jnp.argmax(group_score, axis=-1)
    in_group = jnp.take_along_axis(probs, g[:, None, None], axis=1)[:, 0]
    wv, wi = lax.top_k(in_group, TOP_K)
    expert = (g[:, None] * EXPERTS_PER_GROUP + wi).reshape(-1)
    gate = (wv / wv.sum(-1, keepdims=True)).reshape(-1)
    S = N * TOP_K
    order = jnp.argsort(expert)
    e_sorted = expert[order]
    counts = jnp.bincount(expert, length=N_EXPERTS)
    padded = (counts + EXPERT_ROWS - 1) // EXPERT_ROWS * EXPERT_ROWS
    pad_end = jnp.cumsum(padded)
    pad_start = pad_end - padded
    start = jnp.cumsum(counts) - counts
    dest = pad_start[e_sorted] + jnp.arange(S) - start[e_sorted]
    n_blk = (S + N_EXPERTS * (EXPERT_ROWS - 1) + EXPERT_ROWS - 1) // EXPERT_ROWS
    tok = order // TOP_K
    xs = jnp.zeros((n_blk * EXPERT_ROWS, D), h.dtype).at[dest].set(h[tok])
    blk_expert = jnp.minimum(
        jnp.searchsorted(pad_end, jnp.arange(n_blk) * EXPERT_ROWS, side='right'), N_EXPERTS - 1)

    def expert_rows(args):
        xb, e = args
        return (jax.nn.silu(xb @ w_gate[e]) * (xb @ w_up[e])) @ w_down[e]

    ys = lax.map(expert_rows, (xs.reshape(n_blk, EXPERT_ROWS, D), blk_expert)).reshape(-1, D)
    y_slot = ys[dest] * gate[order][:, None].astype(ys.dtype)
    return jnp.zeros((N, D), ys.dtype).at[tok].add(y_slot).astype(h.dtype)


def token_mixer(x, pool_prev, pos0, attend, w_in, pool_w, pool_scale, w_pool_out, w_attn_out, w_o):
    B, T, _ = x.shape
    proj = x @ w_in
    cuts = [POOL_WIDTH, POOL_WIDTH + ATT_WIDTH, POOL_WIDTH + 2 * ATT_WIDTH,
            POOL_WIDTH + 3 * ATT_WIDTH, POOL_WIDTH + 3 * ATT_WIDTH + D_MODEL]
    p, q, k, v, g_pool, g_attn = jnp.split(proj, cuts, axis=-1)
    y_pool, new_pool = pool_branch(p, pool_prev, pos0, pool_w, pool_scale)
    q = q.reshape(B, T, N_HEADS, HEAD_DIM) * (HEAD_DIM ** -0.5)
    k = k.reshape(B, T, N_HEADS, HEAD_DIM)
    v = v.reshape(B, T, N_HEADS, HEAD_DIM)
    o = attend(q, k, v).reshape(B, T, ATT_WIDTH)
    merged = jax.nn.sigmoid(g_pool) * (y_pool @ w_pool_out) + jax.nn.sigmoid(g_attn) * (o @ w_attn_out)
    return merged @ w_o, new_pool, k, v


def trunk_layer(x, pool_prev, pos0, attend, w_router, b_router, w_in, pool_w, pool_scale,
                w_pool_out, w_attn_out, w_o, ln1_g, ln1_b, w_gate, w_up, w_down, ln2_g, ln2_b):
    mix, new_pool, k, v = token_mixer(x, pool_prev, pos0, attend, w_in, pool_w, pool_scale,
                                      w_pool_out, w_attn_out, w_o)
    x = layer_norm(DEEPNORM_ALPHA * x + mix, ln1_g, ln1_b)
    B, T, D = x.shape
    f = moe_ffn(x.reshape(B * T, D), w_router, b_router, w_gate, w_up, w_down).reshape(B, T, D)
    x = layer_norm(DEEPNORM_ALPHA * x + f, ln2_g, ln2_b)
    return x, new_pool, k, v


def setup_inputs(seed: int = 0) -> dict:
    key = jax.random.key(seed)
    ks = jax.random.split(key, 24)
    n_pages = PAST_LEN // PAGE_SIZE
    n_used = DEC_BATCH * n_pages
    n_phys = n_used + max(1, n_used // 4)

    def nrm(k, shape, s):
        return jax.random.normal(k, shape, jnp.float32) * s

    return {
        "x_prompt": nrm(ks[0], (BATCH, SEQ, D_MODEL), 1.0),
        "x_sample": nrm(ks[1], (DEC_BATCH, DEC_SEQ, D_MODEL), 1.0),
        "cache_k": nrm(ks[2], (n_phys, PAGE_SIZE, DEPTH, N_HEADS, HEAD_DIM), 1.0),
        "cache_v": nrm(ks[3], (n_phys, PAGE_SIZE, DEPTH, N_HEADS, HEAD_DIM), 1.0),
        "state_pool": nrm(ks[4], (DEC_BATCH, DEPTH, POOL_BUF, POOL_WIDTH), 1.0),
        "page_table": jax.random.permutation(ks[5], n_phys)[:n_used].reshape(
            DEC_BATCH, n_pages).astype(jnp.int32),
        "rel_bias": nrm(ks[6], (N_BUCKETS, N_HEADS), 0.5),
        "w_router": nrm(ks[7], (D_MODEL, N_EXPERTS), D_MODEL ** -0.5),
        "b_router": nrm(ks[8], (N_EXPERTS,), 0.01),
        "w_in": nrm(ks[9], (DEPTH, D_MODEL, IN_WIDTH), D_MODEL ** -0.5),
        "pool_w": nrm(ks[10], (DEPTH, POOL_GROUPS, POOL_GW, POOL_GW), POOL_GW ** -0.5),
        "pool_scale": 1.0 + nrm(ks[11], (DEPTH, POOL_WIDTH), 0.1),
        "w_pool_out": nrm(ks[12], (DEPTH, POOL_WIDTH, D_MODEL), POOL_WIDTH ** -0.5),
        "w_attn_out": nrm(ks[13], (DEPTH, ATT_WIDTH, D_MODEL), ATT_WIDTH ** -0.5),
        "w_o": nrm(ks[14], (DEPTH, D_MODEL, D_MODEL), DEEPNORM_BETA * D_MODEL ** -0.5),
        "ln1_g": 1.0 + nrm(ks[15], (DEPTH, D_MODEL), 0.05),
        "ln1_b": nrm(ks[16], (DEPTH, D_MODEL), 0.02),
        "w_gate": nrm(ks[17], (DEPTH, N_EXPERTS, D_MODEL, D_EXPERT), D_MODEL ** -0.5),
        "w_up": nrm(ks[18], (DEPTH, N_EXPERTS, D_MODEL, D_EXPERT), D_MODEL ** -0.5),
        "w_down": nrm(ks[19], (DEPTH, N_EXPERTS, D_EXPERT, D_MODEL), DEEPNORM_BETA * D_EXPERT ** -0.5),
        "ln2_g": 1.0 + nrm(ks[20], (DEPTH, D_MODEL), 0.05),
        "ln2_b": nrm(ks[21], (DEPTH, D_MODEL), 0.02),
    }


def reference(x_prompt, x_sample, cache_k, cache_v, state_pool, page_table, rel_bias,
              w_router, b_router, w_in, pool_w, pool_scale, w_pool_out, w_attn_out, w_o,
              ln1_g, ln1_b, w_gate, w_up, w_down, ln2_g, ln2_b):
    B = x_prompt.shape[0]
    past_len = page_table.shape[1] * PAGE_SIZE
    xp, xs = x_prompt, x_sample
    attend_p = functools.partial(moba_prompt, rel_bias=rel_bias)
    kp, vp, pp, ksm, vsm, psm = [], [], [], [], [], []
    for l in range(DEPTH):
        lw = (w_in[l], pool_w[l], pool_scale[l], w_pool_out[l], w_attn_out[l], w_o[l],
              ln1_g[l], ln1_b[l], w_gate[l], w_up[l], w_down[l], ln2_g[l], ln2_b[l])
        prev0 = jnp.zeros((B, POOL_BUF, POOL_WIDTH), xp.dtype)
        xp, pool_p, k_p, v_p = trunk_layer(xp, prev0, 0, attend_p, w_router, b_router, *lw)
        attend_s = functools.partial(moba_sample, cache_k=cache_k, cache_v=cache_v,
                                     page_table=page_table, layer=l, rel_bias=rel_bias)
        xs, pool_s, k_s, v_s = trunk_layer(xs, state_pool[:, l], past_len, attend_s,
                                           w_router, b_router, *lw)
        kp.append(k_p); vp.append(v_p); pp.append(pool_p)
        ksm.append(k_s); vsm.append(v_s); psm.append(pool_s)
    k_prompt = jnp.stack(kp, axis=2)
    v_prompt = jnp.stack(vp, axis=2)
    pool_prompt = jnp.stack(pp, axis=1)
    k_sample = jnp.stack(ksm, axis=2)
    v_sample = jnp.stack(vsm, axis=2)
    pool_sample = jnp.stack(psm, axis=1)
    return (xp, xs, k_prompt, v_prompt, pool_prompt, k_sample, v_sample, pool_sample)
```

```python
import functools
import math

import numpy as np
import jax
import jax.numpy as jnp
from jax import lax
from jax.experimental import pallas as pl
from jax.experimental.pallas import tpu as pltpu

F32 = jnp.float32
BF16 = jnp.bfloat16
NEG_INF = float("-inf")
HIGHEST = lax.Precision.HIGHEST

D_MODEL = 1024
DEPTH = 2
N_HEADS = 8
HEAD_DIM = 64
ATT_WIDTH = N_HEADS * HEAD_DIM
MOBA_BLOCK = 256
MOBA_TOPK = 3
N_BUCKETS = 32
REL_MAX_DIST = 128
POOL_WINDOWS = (2, 4, 8, 16)
POOL_WIDTH = 512
POOL_GW = 128
POOL_BUF = 15
POOL_HALO = 16
IN_WIDTH = POOL_WIDTH + 3 * ATT_WIDTH + 2 * D_MODEL
N_EXPERTS = 16
N_GROUPS = 4
EXPERTS_PER_GROUP = 4
TOP_K = 2
D_EXPERT = 512
EXPERT_ROWS = 128
DEEPNORM_ALPHA = (2 * DEPTH) ** 0.25
LN_EPS = 1e-5
PAGE_SIZE = 128
ROUTER_PAD = 128

VMEM_LIMIT = 52 * 1024 * 1024


def _cparams(sem):
    return pltpu.CompilerParams(dimension_semantics=sem, vmem_limit_bytes=VMEM_LIMIT)


def _nt_dot(a, b, precision=None):
    return lax.dot_general(a, b, (((1,), (1,)), ((), ())), precision=precision,
                           preferred_element_type=F32)


def _inproj_body(x_ref, w_ref, p_ref, q_ref, k_ref, v_ref, gp_ref, ga_ref, *attn_refs, tm):
    xb = x_ref[...].astype(BF16)

    def seg(a, b):
        return jnp.dot(xb, w_ref[:, a:b], preferred_element_type=F32)

    c0 = POOL_WIDTH
    p_ref[...] = seg(0, c0)
    q_ref[...] = seg(c0, c0 + ATT_WIDTH) * (HEAD_DIM ** -0.5)
    k = seg(c0 + ATT_WIDTH, c0 + 2 * ATT_WIDTH)
    k_ref[...] = k
    v = seg(c0 + 2 * ATT_WIDTH, c0 + 3 * ATT_WIDTH)
    v_ref[...] = v
    g0 = c0 + 3 * ATT_WIDTH
    gp_ref[...] = jax.nn.sigmoid(seg(g0, g0 + D_MODEL))
    ga_ref[...] = jax.nn.sigmoid(seg(g0 + D_MODEL, g0 + 2 * D_MODEL))
    if attn_refs:
        kb_ref, vt_ref, km_ref = attn_refs
        kb_ref[...] = k.astype(BF16)
        vt_ref[0] = v.T.astype(BF16)
        km_ref[0] = jnp.sum(k, axis=0, keepdims=True) * (1.0 / tm)


def _inproj(x, w_bf, *, tm, attn_layouts):
    m = x.shape[0]
    nt = m // tm
    row = lambda n: pl.BlockSpec((tm, n), lambda i: (i, 0))
    out_shape = [jax.ShapeDtypeStruct((m, POOL_WIDTH), F32),
                 jax.ShapeDtypeStruct((m, ATT_WIDTH), F32),
                 jax.ShapeDtypeStruct((m, ATT_WIDTH), F32),
                 jax.ShapeDtypeStruct((m, ATT_WIDTH), F32),
                 jax.ShapeDtypeStruct((m, D_MODEL), F32),
                 jax.ShapeDtypeStruct((m, D_MODEL), F32)]
    out_specs = [row(POOL_WIDTH), row(ATT_WIDTH), row(ATT_WIDTH), row(ATT_WIDTH),
                 row(D_MODEL), row(D_MODEL)]
    if attn_layouts:
        assert tm == MOBA_BLOCK
        out_shape += [jax.ShapeDtypeStruct((m, ATT_WIDTH), BF16),
                      jax.ShapeDtypeStruct((nt, ATT_WIDTH, tm), BF16),
                      jax.ShapeDtypeStruct((nt, 1, ATT_WIDTH), F32)]
        out_specs += [row(ATT_WIDTH),
                      pl.BlockSpec((1, ATT_WIDTH, tm), lambda i: (i, 0, 0)),
                      pl.BlockSpec((1, 1, ATT_WIDTH), lambda i: (i, 0, 0))]
    return pl.pallas_call(
        functools.partial(_inproj_body, tm=tm),
        grid=(nt,),
        in_specs=[row(D_MODEL), pl.BlockSpec((D_MODEL, IN_WIDTH), lambda i: (0, 0))],
        out_specs=out_specs,
        out_shape=out_shape,
        compiler_params=_cparams(("arbitrary",)),
        name="inproj",
    )(x, w_bf)


def _pool_body(prev_ref, halo_ref, p_ref, pw_ref, sc_ref, y_ref, *, tq, pos0):
    i = pl.program_id(1)
    p = p_ref[0]
    halo = jnp.where(i == 0, prev_ref[0], halo_ref[0])
    z = jnp.concatenate([halo, p], axis=0)
    lane = lax.broadcasted_iota(jnp.int32, z.shape, 1)
    x = z
    for s in (8, 4, 2, 1):
        thr = POOL_WIDTH - POOL_GW * {8: 1, 4: 2, 2: 3, 1: 4}[s]
        sh = pltpu.roll(x, s, 0)
        x = x + (jnp.where(lane >= thr, sh, 0.0) if thr > 0 else sh)
    wsum = x[POOL_HALO:, :]
    row = lax.broadcasted_iota(jnp.int32, (tq, POOL_WIDTH), 0)
    lane2 = lax.broadcasted_iota(jnp.int32, (tq, POOL_WIDTH), 1)
    wl = jnp.where(lane2 < POOL_GW, POOL_WINDOWS[0],
                   jnp.where(lane2 < 2 * POOL_GW, POOL_WINDOWS[1],
                             jnp.where(lane2 < 3 * POOL_GW, POOL_WINDOWS[2], POOL_WINDOWS[3])))
    pos = pos0 + i * tq + row
    cnt = jnp.minimum(pos + 1, wl).astype(F32)
    d = (wsum / cnt - p).astype(BF16)
    ys = [jnp.dot(d[:, g * POOL_GW:(g + 1) * POOL_GW], pw_ref[g], preferred_element_type=F32)
          for g in range(len(POOL_WINDOWS))]
    y_ref[0] = jnp.concatenate(ys, axis=1) * sc_ref[...]


def _pool(p, prev16, pool_w_bf, scale, *, tq, pos0):
    b, t, c = p.shape
    nq = t // tq
    hb = tq // POOL_HALO
    return pl.pallas_call(
        functools.partial(_pool_body, tq=tq, pos0=pos0),
        grid=(b, nq),
        in_specs=[pl.BlockSpec((1, POOL_HALO, c), lambda bi, i: (bi, 0, 0)),
                  pl.BlockSpec((1, POOL_HALO, c), lambda bi, i: (bi, jnp.maximum(i * hb - 1, 0), 0)),
                  pl.BlockSpec((1, tq, c), lambda bi, i: (bi, i, 0)),
                  pl.BlockSpec((len(POOL_WINDOWS), POOL_GW, POOL_GW), lambda bi, i: (0, 0, 0)),
                  pl.BlockSpec((1, c), lambda bi, i: (0, 0))],
        out_specs=pl.BlockSpec((1, tq, c), lambda bi, i: (bi, i, 0)),
        out_shape=jax.ShapeDtypeStruct((b, t, c), F32),
        compiler_params=_cparams(("arbitrary", "arbitrary")),
        name="pool",
    )(prev16, p, p, pool_w_bf, scale)


def _bucket_np(dist):
    n = np.maximum(dist, 0)
    max_exact = N_BUCKETS // 2
    nf = np.maximum(n, 1).astype(np.float32)
    large = max_exact + (np.log(nf / np.float32(max_exact)) / np.float32(math.log(REL_MAX_DIST / max_exact))
                         * np.float32(N_BUCKETS - max_exact)).astype(np.int32)
    large = np.minimum(large, N_BUCKETS - 1)
    return np.where(n < max_exact, n, large)


def _far_bucket(min_dist, max_dist):
    b = _bucket_np(np.arange(min_dist, max_dist + 1))
    assert (b == b[0]).all()
    return int(b[0])


def _moba_prompt_body(cfar_ref, q_ref, kb_ref, vt_ref, km_ref, bown_ref, bprev_ref, o_ref, sel_ref,
                      *, nb):
    hp = pl.program_id(1)
    i = pl.program_id(2)
    tq = MOBA_BLOCK
    qf = q_ref[0]
    lane = lax.broadcasted_iota(jnp.int32, qf.shape, 1)
    km = km_ref[0].astype(BF16)
    blk = lax.broadcasted_iota(jnp.int32, (nb, tq), 0)
    valid = blk < i
    qzb = []
    for hh in range(2):
        qz = jnp.where((lane >= HEAD_DIM * hh) & (lane < HEAD_DIM * (hh + 1)), qf, 0.0)
        qzb.append(qz.astype(BF16))
        gs = _nt_dot(km, qzb[hh])
        sel = jnp.zeros(blk.shape, jnp.bool_)
        for _ in range(MOBA_TOPK):
            cand = valid & jnp.logical_not(sel)
            cur = jnp.where(cand, gs, NEG_INF)
            mx = jnp.max(cur, axis=0, keepdims=True)
            ismax = cand & (cur == mx)
            first = jnp.min(jnp.where(ismax, blk, nb), axis=0, keepdims=True)
            sel = sel | (blk == first)
        self32 = sel.astype(F32)
        for jj in range(nb):
            sel_ref[hh, jj] = self32[jj:jj + 1, :]

    def block(j, hh, bias, mask_row, state):
        m, l, acc = state
        st = _nt_dot(kb_ref[0, j], qzb[hh]) + bias
        if mask_row is not None:
            st = jnp.where(mask_row > 0.0, st, NEG_INF)
        m_new = jnp.maximum(m, jnp.max(st, axis=0, keepdims=True))
        alpha = jnp.exp(m - m_new)
        p = jnp.exp(st - m_new)
        l_new = alpha * l + jnp.sum(p, axis=0, keepdims=True)
        vt = vt_ref[0, j, HEAD_DIM * hh:HEAD_DIM * (hh + 1), :]
        acc_new = alpha * acc + jnp.dot(vt, p.astype(BF16), preferred_element_type=F32)
        return m_new, l_new, acc_new

    def far_body(j, carry):
        return tuple(block(j, hh, cfar_ref[2 * hp + hh], sel_ref[hh, j], carry[hh]) for hh in range(2))

    init = tuple((jnp.full((1, tq), -1e30, F32), jnp.zeros((1, tq), F32), jnp.zeros((HEAD_DIM, tq), F32))
                 for _ in range(2))
    carry = lax.fori_loop(0, i - 1, far_body, init)
    jp = jnp.maximum(i - 1, 0)
    outs = []
    for hh in range(2):
        st8 = block(jp, hh, bprev_ref[hh], sel_ref[hh, jp], carry[hh])
        m, l, acc = block(i, hh, bown_ref[hh], None, st8)
        outs.append(acc / l)
    o_ref[0] = jnp.concatenate(outs, axis=0).T


def _moba_prompt(q, kb, vt, km, cfar, b_own, b_prev):
    b, t, _ = q.shape
    nb = t // MOBA_BLOCK
    return pl.pallas_call(
        functools.partial(_moba_prompt_body, nb=nb),
        grid=(b, N_HEADS // 2, nb),
        in_specs=[pl.BlockSpec(memory_space=pltpu.SMEM),
                  pl.BlockSpec((1, MOBA_BLOCK, 128), lambda bi, hp, i: (bi, i, hp)),
                  pl.BlockSpec((1, nb, MOBA_BLOCK, 128), lambda bi, hp, i: (bi, 0, 0, hp)),
                  pl.BlockSpec((1, nb, 128, MOBA_BLOCK), lambda bi, hp, i: (bi, 0, hp, 0)),
                  pl.BlockSpec((1, nb, 128), lambda bi, hp, i: (bi, 0, hp)),
                  pl.BlockSpec((2, MOBA_BLOCK, MOBA_BLOCK), lambda bi, hp, i: (hp, 0, 0)),
                  pl.BlockSpec((2, MOBA_BLOCK, MOBA_BLOCK), lambda bi, hp, i: (hp, 0, 0))],
        out_specs=pl.BlockSpec((1, MOBA_BLOCK, 128), lambda bi, hp, i: (bi, i, hp)),
        out_shape=jax.ShapeDtypeStruct((b, t, ATT_WIDTH), F32),
        scratch_shapes=[pltpu.VMEM((2, nb, 1, MOBA_BLOCK), F32)],
        compiler_params=_cparams(("arbitrary", "arbitrary", "arbitrary")),
        name="moba_prompt",
    )(cfar, q, kb, vt, km, b_own, b_prev)


SAMPLE_PAGES_PER_STEP = 8
PAGES_PER_BLOCK = MOBA_BLOCK // PAGE_SIZE


def _moba_sample_body(pt_ref, qbd_ref, knew_ref, vnew_ref, cfar_ref, blast_ref, bown_ref, *rest,
                      nbs, ts):
    npg = SAMPLE_PAGES_PER_STEP
    k_refs = rest[:npg]
    v_refs = rest[npg:2 * npg]
    o_ref = rest[2 * npg]
    gs_sc, m_sc, l_sc, o_sc = rest[2 * npg + 1:]
    s = pl.program_id(1)
    nsteps = pl.num_programs(1)
    bps = npg // PAGES_PER_BLOCK
    nr = ts * N_HEADS
    qb = qbd_ref[0].astype(BF16)
    qbd = qb.astype(F32)
    for jj in range(bps):
        j = s * bps + jj
        kblk = jnp.concatenate([k_refs[PAGES_PER_BLOCK * jj + u][0] for u in range(PAGES_PER_BLOCK)], axis=0)
        vblk = jnp.concatenate([v_refs[PAGES_PER_BLOCK * jj + u][0] for u in range(PAGES_PER_BLOCK)], axis=0)
        kmean = (jnp.sum(kblk, axis=0, keepdims=True) * (1.0 / MOBA_BLOCK)).astype(BF16).astype(F32)
        gs_sc[j] = jnp.sum(qbd * kmean, axis=1, keepdims=True)
        bias = jnp.where(j == nbs - 1, blast_ref[...], cfar_ref[...])
        st = _nt_dot(qb, kblk.astype(BF16)) + bias
        m = jnp.max(st, axis=1, keepdims=True)
        p = jnp.exp(st - m)
        m_sc[j] = m
        l_sc[j] = jnp.sum(p, axis=1, keepdims=True)
        o_sc[j] = jnp.dot(p.astype(BF16), vblk.astype(BF16), preferred_element_type=F32)

    @pl.when(s == nsteps - 1)
    def _():
        gs = gs_sc[...]
        blk = lax.broadcasted_iota(jnp.int32, gs.shape, 0)
        sel = jnp.zeros(gs.shape, jnp.bool_)
        for _ in range(MOBA_TOPK):
            cand = jnp.logical_not(sel)
            cur = jnp.where(cand, gs, NEG_INF)
            mx = jnp.max(cur, axis=0, keepdims=True)
            ismax = cand & (cur == mx)
            first = jnp.min(jnp.where(ismax, blk, nbs), axis=0, keepdims=True)
            sel = sel | (blk == first)
        knew = knew_ref[0].astype(BF16).astype(F32)
        vnew = vnew_ref[0].astype(BF16).astype(F32)
        s_own = [jnp.sum(qbd * knew[t:t + 1, :], axis=1, keepdims=True) + bown_ref[:, t:t + 1]
                 for t in range(ts)]
        m_all = m_sc[...]
        mtot = jnp.max(jnp.where(sel, m_all, NEG_INF), axis=0)
        for t in range(ts):
            mtot = jnp.maximum(mtot, s_own[t])
        w = jnp.where(sel, jnp.exp(m_all - mtot[None]), 0.0)
        ltot = jnp.sum(w * l_sc[...], axis=0)
        gs_sc[...] = w

        def merge(j, acc):
            return acc + gs_sc[j] * o_sc[j]

        otot = lax.fori_loop(0, nbs, merge, jnp.zeros((nr, ATT_WIDTH), F32))
        for t in range(ts):
            pt = jnp.exp(s_own[t] - mtot)
            ltot = ltot + pt
            otot = otot + pt.astype(BF16).astype(F32) * vnew[t:t + 1, :]
        out = otot / ltot
        row = lax.broadcasted_iota(jnp.int32, out.shape, 0)
        lane = lax.broadcasted_iota(jnp.int32, out.shape, 1)
        out = jnp.where((lane // HEAD_DIM) == (row % N_HEADS), out, 0.0)
        pieces = [jnp.sum(out[N_HEADS * t:N_HEADS * (t + 1), :], axis=0, keepdims=True) for t in range(ts)]
        pieces.append(jnp.zeros((8 - ts, ATT_WIDTH), F32))
        o_ref[0] = jnp.concatenate(pieces, axis=0)


def _moba_sample(page_table, qbd, knew8, vnew8, cfar_rows, b_last, b_own, cache_k3, cache_v3, layer):
    db, nr, _ = qbd.shape
    ts = nr // N_HEADS
    n_pages = page_table.shape[1]
    nbs = n_pages // PAGES_PER_BLOCK
    npg = SAMPLE_PAGES_PER_STEP
    nsteps = n_pages // npg

    def page_spec(u):
        return pl.BlockSpec((1, PAGE_SIZE, ATT_WIDTH), lambda b, s, pt: (pt[b, s * npg + u], 0, layer))

    full2 = lambda shp: pl.BlockSpec(shp, lambda b, s, pt: (0, 0))
    per_b = lambda r: pl.BlockSpec((1, r, ATT_WIDTH), lambda b, s, pt: (b, 0, 0))
    grid_spec = pltpu.PrefetchScalarGridSpec(
        num_scalar_prefetch=1,
        grid=(db, nsteps),
        in_specs=[per_b(nr), per_b(8), per_b(8),
                  full2((nr, 1)), full2((nr, MOBA_BLOCK)), full2((nr, 8))]
                 + [page_spec(u) for u in range(npg)] + [page_spec(u) for u in range(npg)],
        out_specs=per_b(8),
        scratch_shapes=[pltpu.VMEM((nbs, nr, 1), F32), pltpu.VMEM((nbs, nr, 1), F32),
                        pltpu.VMEM((nbs, nr, 1), F32), pltpu.VMEM((nbs, nr, ATT_WIDTH), F32)],
    )
    return pl.pallas_call(
        functools.partial(_moba_sample_body, nbs=nbs, ts=ts),
        grid_spec=grid_spec,
        out_shape=jax.ShapeDtypeStruct((db, 8, ATT_WIDTH), F32),
        compiler_params=_cparams(("arbitrary", "arbitrary")),
        name="moba_sample",
    )(page_table, qbd, knew8, vnew8, cfar_rows, b_last, b_own,
      *([cache_k3] * npg), *([cache_v3] * npg))


def _layer_norm(h, g, b):
    mu = jnp.mean(h, axis=-1, keepdims=True)
    c = h - mu
    var = jnp.mean(c * c, axis=-1, keepdims=True)
    return c * lax.rsqrt(var + LN_EPS) * g + b


def _merge_body(x_ref, yp_ref, o_ref, gp_ref, ga_ref, wpo_ref, wao_ref, wo_ref, g_ref, b_ref,
                wr_ref, br_ref, x1_ref, lg_ref):
    a = jnp.dot(yp_ref[...].astype(BF16), wpo_ref[...], preferred_element_type=F32)
    bb = jnp.dot(o_ref[...].astype(BF16), wao_ref[...], preferred_element_type=F32)
    merged = gp_ref[...] * a + ga_ref[...] * bb
    mix = jnp.dot(merged.astype(BF16), wo_ref[...], preferred_element_type=F32)
    x1 = _layer_norm(DEEPNORM_ALPHA * x_ref[...] + mix, g_ref[...], b_ref[...])
    x1_ref[...] = x1
    lg_ref[...] = jnp.dot(x1.astype(BF16), wr_ref[...], preferred_element_type=F32) + br_ref[...]


def _merge(x, yp, o, gp, ga, wpo, wao, wo, g, b, wr, br, *, tm):
    m = x.shape[0]
    row = lambda n: pl.BlockSpec((tm, n), lambda i: (i, 0))
    full = lambda r, c: pl.BlockSpec((r, c), lambda i: (0, 0))
    return pl.pallas_call(
        _merge_body,
        grid=(m // tm,),
        in_specs=[row(D_MODEL), row(POOL_WIDTH), row(ATT_WIDTH), row(D_MODEL), row(D_MODEL),
                  full(POOL_WIDTH, D_MODEL), full(ATT_WIDTH, D_MODEL), full(D_MODEL, D_MODEL),
                  full(1, D_MODEL), full(1, D_MODEL), full(D_MODEL, ROUTER_PAD), full(1, ROUTER_PAD)],
        out_specs=[row(D_MODEL), row(ROUTER_PAD)],
        out_shape=[jax.ShapeDtypeStruct((m, D_MODEL), F32), jax.ShapeDtypeStruct((m, ROUTER_PAD), F32)],
        compiler_params=_cparams(("arbitrary",)),
        name="merge",
    )(x, yp, o, gp, ga, wpo, wao, wo, g, b, wr, br)


def _experts_body(be_ref, xs_ref, wg_ref, wu_ref, wd_ref, ys_ref):
    xb = xs_ref[...].astype(BF16)
    h1 = jnp.dot(xb, wg_ref[0], preferred_element_type=F32)
    h2 = jnp.dot(xb, wu_ref[0], preferred_element_type=F32)
    act = (h1 * jax.nn.sigmoid(h1) * h2).astype(BF16)
    ys_ref[...] = jnp.dot(act, wd_ref[0], preferred_element_type=F32)


def _experts(blk_expert, xs, wg, wu, wd):
    n_rows = xs.shape[0]
    n_blk = n_rows // EXPERT_ROWS
    grid_spec = pltpu.PrefetchScalarGridSpec(
        num_scalar_prefetch=1,
        grid=(n_blk,),
        in_specs=[pl.BlockSpec((EXPERT_ROWS, D_MODEL), lambda i, be: (i, 0)),
                  pl.BlockSpec((1, D_MODEL, D_EXPERT), lambda i, be: (be[i], 0, 0)),
                  pl.BlockSpec((1, D_MODEL, D_EXPERT), lambda i, be: (be[i], 0, 0)),
                  pl.BlockSpec((1, D_EXPERT, D_MODEL), lambda i, be: (be[i], 0, 0))],
        out_specs=pl.BlockSpec((EXPERT_ROWS, D_MODEL), lambda i, be: (i, 0)),
    )
    return pl.pallas_call(
        _experts_body,
        grid_spec=grid_spec,
        out_shape=jax.ShapeDtypeStruct((n_rows, D_MODEL), F32),
        compiler_params=_cparams(("arbitrary",)),
        name="experts",
    )(blk_expert, xs, wg, wu, wd)


def _ln2_body(x_ref, f_ref, g_ref, b_ref, o_ref):
    o_ref[...] = _layer_norm(DEEPNORM_ALPHA * x_ref[...] + f_ref[...], g_ref[...], b_ref[...])


def _ln2(x, f, g, b, *, tm):
    m = x.shape[0]
    row = pl.BlockSpec((tm, D_MODEL), lambda i: (i, 0))
    vec = pl.BlockSpec((1, D_MODEL), lambda i: (0, 0))
    return pl.pallas_call(
        _ln2_body,
        grid=(m // tm,),
        in_specs=[row, row, vec, vec],
        out_specs=row,
        out_shape=jax.ShapeDtypeStruct((m, D_MODEL), F32),
        compiler_params=_cparams(("arbitrary",)),
        name="ln2",
    )(x, f, g, b)


def _route(logits):
    n = logits.shape[0]
    probs = jax.nn.softmax(logits, axis=-1).reshape(n, N_GROUPS, EXPERTS_PER_GROUP)
    group_score = lax.top_k(probs, TOP_K)[0].sum(-1)
    g = jnp.argmax(group_score, axis=-1)
    in_group = jnp.take_along_axis(probs, g[:, None, None], axis=1)[:, 0]
    wv, wi = lax.top_k(in_group, TOP_K)
    expert = g[:, None] * EXPERTS_PER_GROUP + wi
    gate = wv / wv.sum(-1, keepdims=True)
    return expert.astype(jnp.int32), gate


def _moe(h, logits, wg, wu, wd):
    n = h.shape[0]
    expert, gate = _route(logits)
    s = n * TOP_K
    e_flat = expert.reshape(-1)
    onehot = (e_flat[:, None] == jnp.arange(N_EXPERTS, dtype=jnp.int32)[None, :]).astype(jnp.int32)
    csum = jnp.cumsum(onehot, axis=0)
    counts = csum[-1]
    pos_in = jnp.take_along_axis(csum, e_flat[:, None], axis=1)[:, 0] - 1
    padded = (counts + EXPERT_ROWS - 1) // EXPERT_ROWS * EXPERT_ROWS
    pad_end = jnp.cumsum(padded)
    pad_start = pad_end - padded
    dest = pad_start[e_flat] + pos_in
    n_blk = (s + N_EXPERTS * (EXPERT_ROWS - 1) + EXPERT_ROWS - 1) // EXPERT_ROWS
    tok = jnp.arange(s, dtype=jnp.int32) // TOP_K
    xs = jnp.zeros((n_blk * EXPERT_ROWS, D_MODEL), h.dtype).at[dest].set(h[tok])
    blk_expert = jnp.minimum(
        jnp.searchsorted(pad_end, jnp.arange(n_blk, dtype=jnp.int32) * EXPERT_ROWS, side="right"),
        N_EXPERTS - 1).astype(jnp.int32)
    ys = _experts(blk_expert, xs, wg, wu, wd)
    y_slot = ys[dest] * gate.reshape(-1)[:, None]
    return y_slot.reshape(n, TOP_K, D_MODEL).sum(axis=1)


def kernel(x_prompt, x_sample, cache_k, cache_v, state_pool, page_table, rel_bias, w_router, b_router,
           w_in, pool_w, pool_scale, w_pool_out, w_attn_out, w_o, ln1_g, ln1_b, w_gate, w_up, w_down,
           ln2_g, ln2_b):
    bsz, seq, _ = x_prompt.shape
    db, ts, _ = x_sample.shape
    n_phys = cache_k.shape[0]
    past_len = page_table.shape[1] * PAGE_SIZE
    nb = seq // MOBA_BLOCK
    nbs = past_len // MOBA_BLOCK
    assert seq % MOBA_BLOCK == 0 and past_len % MOBA_BLOCK == 0 and ts <= 8
    np_rows = bsz * seq
    ns_rows = db * ts

    far_p = _far_bucket(MOBA_BLOCK + 1, seq)
    far_s = _far_bucket(MOBA_BLOCK + 1, past_len + ts)
    c = np.arange(MOBA_BLOCK)
    d_own = c[None, :] - c[:, None]
    tab_own = rel_bias[_bucket_np(np.maximum(d_own, 0))]
    b_own_p = jnp.where((d_own >= 0)[None], tab_own.transpose(2, 0, 1), NEG_INF)
    b_prev_p = rel_bias[_bucket_np(d_own + MOBA_BLOCK)].transpose(2, 0, 1)
    cfar_p = rel_bias[far_p]
    t_idx = np.repeat(np.arange(ts), N_HEADS)
    h_idx = np.tile(np.arange(N_HEADS), ts)
    d_last = MOBA_BLOCK + t_idx[:, None] - c[None, :]
    b_last_s = rel_bias[_bucket_np(d_last), h_idx[:, None]]
    tn = np.arange(8)
    d_new = t_idx[:, None] - tn[None, :]
    b_own_s = jnp.where(((d_new >= 0) & (tn[None, :] < ts)),
                        rel_bias[_bucket_np(np.maximum(d_new, 0)), h_idx[:, None]], NEG_INF)
    cfar_s = rel_bias[far_s][h_idx][:, None]

    cache_k3 = cache_k.reshape(n_phys, PAGE_SIZE, DEPTH * ATT_WIDTH)
    cache_v3 = cache_v.reshape(n_phys, PAGE_SIZE, DEPTH * ATT_WIDTH)
    wr_pad = jnp.zeros((D_MODEL, ROUTER_PAD), BF16).at[:, :N_EXPERTS].set(w_router.astype(BF16))
    br_pad = jnp.zeros((1, ROUTER_PAD), F32).at[0, :N_EXPERTS].set(b_router)
    head_mask = jnp.asarray((np.arange(ATT_WIDTH)[None, :] // HEAD_DIM) == h_idx[:, None])

    xp = x_prompt.reshape(np_rows, D_MODEL)
    xs = x_sample.reshape(ns_rows, D_MODEL)
    kp, vp, pp, ksm, vsm, psm = [], [], [], [], [], []
    for l in range(DEPTH):
        w_in_bf = w_in[l].astype(BF16)
        pool_w_bf = pool_w[l].astype(BF16)
        scale = pool_scale[l][None, :]
        wpo = w_pool_out[l].astype(BF16)
        wao = w_attn_out[l].astype(BF16)
        wo = w_o[l].astype(BF16)
        wg = w_gate[l].astype(BF16)
        wu = w_up[l].astype(BF16)
        wd = w_down[l].astype(BF16)
        g1, b1 = ln1_g[l][None, :], ln1_b[l][None, :]
        g2, b2 = ln2_g[l][None, :], ln2_b[l][None, :]

        p_p, q_p, k_p, v_p, gp_p, ga_p, kb_p, vt_p, km_p = _inproj(xp, w_in_bf, tm=MOBA_BLOCK, attn_layouts=True)
        prev0 = jnp.zeros((bsz, POOL_HALO, POOL_WIDTH), F32)
        yp_p = _pool(p_p.reshape(bsz, seq, POOL_WIDTH), prev0, pool_w_bf, scale, tq=MOBA_BLOCK, pos0=0)
        o_p = _moba_prompt(q_p.reshape(bsz, seq, ATT_WIDTH),
                           kb_p.reshape(bsz, nb, MOBA_BLOCK, ATT_WIDTH),
                           vt_p.reshape(bsz, nb, ATT_WIDTH, MOBA_BLOCK),
                           km_p.reshape(bsz, nb, ATT_WIDTH), cfar_p, b_own_p, b_prev_p)
        x1_p, lg_p = _merge(xp, yp_p.reshape(np_rows, POOL_WIDTH), o_p.reshape(np_rows, ATT_WIDTH),
                            gp_p, ga_p, wpo, wao, wo, g1, b1, wr_pad, br_pad, tm=MOBA_BLOCK)

        p_s, q_s, k_s, v_s, gp_s, ga_s = _inproj(xs, w_in_bf, tm=ns_rows, attn_layouts=False)
        state = state_pool[:, l]
        prev_s = jnp.concatenate([jnp.zeros((db, 1, POOL_WIDTH), F32), state], axis=1)
        p_s3 = p_s.reshape(db, ts, POOL_WIDTH)
        p_s8 = jnp.pad(p_s3, ((0, 0), (0, 8 - ts), (0, 0)))
        yp_s = _pool(p_s8, prev_s, pool_w_bf, scale, tq=8, pos0=past_len)[:, :ts]
        q_s3 = q_s.reshape(db, ts, ATT_WIDTH)
        qbd = jnp.where(head_mask[None], jnp.repeat(q_s3, N_HEADS, axis=1), 0.0)
        knew8 = jnp.pad(k_s.reshape(db, ts, ATT_WIDTH), ((0, 0), (0, 8 - ts), (0, 0)))
        vnew8 = jnp.pad(v_s.reshape(db, ts, ATT_WIDTH), ((0, 0), (0, 8 - ts), (0, 0)))
        o_s = _moba_sample(page_table, qbd, knew8, vnew8, cfar_s, b_last_s, b_own_s,
                           cache_k3, cache_v3, l)[:, :ts]
        x1_s, lg_s = _merge(xs, yp_s.reshape(ns_rows, POOL_WIDTH), o_s.reshape(ns_rows, ATT_WIDTH),
                            gp_s, ga_s, wpo, wao, wo, g1, b1, wr_pad, br_pad, tm=ns_rows)

        x1 = jnp.concatenate([x1_p, x1_s], axis=0)
        lg = jnp.concatenate([lg_p, lg_s], axis=0)[:, :N_EXPERTS]
        f = _moe(x1, lg, wg, wu, wd)
        x2 = _ln2(x1, f, g2, b2, tm=EXPERT_ROWS)
        xp, xs = x2[:np_rows], x2[np_rows:]

        kp.append(k_p.reshape(bsz, seq, N_HEADS, HEAD_DIM))
        vp.append(v_p.reshape(bsz, seq, N_HEADS, HEAD_DIM))
        pp.append(p_p.reshape(bsz, seq, POOL_WIDTH)[:, seq - POOL_BUF:])
        ksm.append(k_s.reshape(db, ts, N_HEADS, HEAD_DIM))
        vsm.append(v_s.reshape(db, ts, N_HEADS, HEAD_DIM))
        psm.append(jnp.concatenate([state, p_s3], axis=1)[:, -POOL_BUF:])

    return (xp.reshape(bsz, seq, D_MODEL), xs.reshape(db, ts, D_MODEL),
            jnp.stack(kp, axis=2), jnp.stack(vp, axis=2), jnp.stack(pp, axis=1),
            jnp.stack(ksm, axis=2), jnp.stack(vsm, axis=2), jnp.stack(psm, axis=1))
```

```python
import functools
import math

import numpy as np
import jax
import jax.numpy as jnp
from jax import lax
from jax.experimental import pallas as pl
from jax.experimental.pallas import tpu as pltpu

F32 = jnp.float32
BF16 = jnp.bfloat16
NEG_INF = float("-inf")

D_MODEL = 1024
DEPTH = 2
N_HEADS = 8
HEAD_DIM = 64
ATT_WIDTH = N_HEADS * HEAD_DIM
MOBA_BLOCK = 256
MOBA_TOPK = 3
N_BUCKETS = 32
REL_MAX_DIST = 128
POOL_WINDOWS = (2, 4, 8, 16)
POOL_WIDTH = 512
POOL_GW = 128
POOL_BUF = 15
POOL_HALO = 16
IN_WIDTH = POOL_WIDTH + 3 * ATT_WIDTH + 2 * D_MODEL
N_EXPERTS = 16
N_GROUPS = 4
EXPERTS_PER_GROUP = 4
TOP_K = 2
D_EXPERT = 512
EXPERT_ROWS = 128
DEEPNORM_ALPHA = (2 * DEPTH) ** 0.25
LN_EPS = 1e-5
PAGE_SIZE = 128
ROUTER_PAD = 128
LANES = 128

VMEM_LIMIT = 52 * 1024 * 1024


def _cparams(sem):
    return pltpu.CompilerParams(dimension_semantics=sem, vmem_limit_bytes=VMEM_LIMIT)


def _nt_dot(a, b):
    return lax.dot_general(a, b, (((1,), (1,)), ((), ())), preferred_element_type=F32)


def _top_mask(scores, cand, k, n):
    idx = lax.broadcasted_iota(jnp.int32, scores.shape, 0)
    sel = jnp.zeros(scores.shape, jnp.bool_)
    for _ in range(k):
        c = cand & jnp.logical_not(sel)
        cur = jnp.where(c, scores, NEG_INF)
        mx = jnp.max(cur, axis=0, keepdims=True)
        first = jnp.min(jnp.where(c & (cur == mx), idx, n), axis=0, keepdims=True)
        sel = sel | (idx == first)
    return sel


def _inproj_body(x_ref, w_ref, p_ref, q_ref, k_ref, v_ref, gp_ref, ga_ref, *attn_refs, tm):
    xb = x_ref[...].astype(BF16)

    def seg(a, b):
        return jnp.dot(xb, w_ref[:, a:b], preferred_element_type=F32)

    c0 = POOL_WIDTH
    p_ref[...] = seg(0, c0)
    q_ref[...] = seg(c0, c0 + ATT_WIDTH) * (HEAD_DIM ** -0.5)
    k = seg(c0 + ATT_WIDTH, c0 + 2 * ATT_WIDTH)
    k_ref[...] = k
    v = seg(c0 + 2 * ATT_WIDTH, c0 + 3 * ATT_WIDTH)
    v_ref[...] = v
    g0 = c0 + 3 * ATT_WIDTH
    gp_ref[...] = jax.nn.sigmoid(seg(g0, g0 + D_MODEL))
    ga_ref[...] = jax.nn.sigmoid(seg(g0 + D_MODEL, g0 + 2 * D_MODEL))
    if attn_refs:
        kb_ref, vt_ref, km_ref = attn_refs
        kb_ref[...] = k.astype(BF16)
        vt_ref[0] = v.T.astype(BF16)
        km_ref[0] = jnp.sum(k, axis=0, keepdims=True) * (1.0 / tm)


def _inproj(x, w_bf, *, tm, attn_layouts):
    m = x.shape[0]
    nt = m // tm
    row = lambda n: pl.BlockSpec((tm, n), lambda i: (i, 0))
    out_shape = [jax.ShapeDtypeStruct((m, POOL_WIDTH), F32),
                 jax.ShapeDtypeStruct((m, ATT_WIDTH), F32),
                 jax.ShapeDtypeStruct((m, ATT_WIDTH), F32),
                 jax.ShapeDtypeStruct((m, ATT_WIDTH), F32),
                 jax.ShapeDtypeStruct((m, D_MODEL), F32),
                 jax.ShapeDtypeStruct((m, D_MODEL), F32)]
    out_specs = [row(POOL_WIDTH), row(ATT_WIDTH), row(ATT_WIDTH), row(ATT_WIDTH),
                 row(D_MODEL), row(D_MODEL)]
    if attn_layouts:
        assert tm == MOBA_BLOCK
        out_shape += [jax.ShapeDtypeStruct((m, ATT_WIDTH), BF16),
                      jax.ShapeDtypeStruct((nt, ATT_WIDTH, tm), BF16),
                      jax.ShapeDtypeStruct((nt, 1, ATT_WIDTH), F32)]
        out_specs += [row(ATT_WIDTH),
                      pl.BlockSpec((1, ATT_WIDTH, tm), lambda i: (i, 0, 0)),
                      pl.BlockSpec((1, 1, ATT_WIDTH), lambda i: (i, 0, 0))]
    return pl.pallas_call(
        functools.partial(_inproj_body, tm=tm),
        grid=(nt,),
        in_specs=[row(D_MODEL), pl.BlockSpec((D_MODEL, IN_WIDTH), lambda i: (0, 0))],
        out_specs=out_specs,
        out_shape=out_shape,
        compiler_params=_cparams(("arbitrary",)),
        name="inproj",
    )(x, w_bf)


def _pool_body(prev_ref, halo_ref, p_ref, pw_ref, sc_ref, y_ref, *, tq, pos0):
    i = pl.program_id(1)
    p = p_ref[0]
    halo = jnp.where(i == 0, prev_ref[0], halo_ref[0])
    z = jnp.concatenate([halo, p], axis=0)
    lane = lax.broadcasted_iota(jnp.int32, z.shape, 1)
    x = z
    for s in (8, 4, 2, 1):
        thr = POOL_WIDTH - POOL_GW * {8: 1, 4: 2, 2: 3, 1: 4}[s]
        sh = pltpu.roll(x, s, 0)
        x = x + (jnp.where(lane >= thr, sh, 0.0) if thr > 0 else sh)
    wsum = x[POOL_HALO:, :]
    row = lax.broadcasted_iota(jnp.int32, (tq, POOL_WIDTH), 0)
    lane2 = lax.broadcasted_iota(jnp.int32, (tq, POOL_WIDTH), 1)
    wl = jnp.where(lane2 < POOL_GW, POOL_WINDOWS[0],
                   jnp.where(lane2 < 2 * POOL_GW, POOL_WINDOWS[1],
                             jnp.where(lane2 < 3 * POOL_GW, POOL_WINDOWS[2], POOL_WINDOWS[3])))
    pos = pos0 + i * tq + row
    cnt = jnp.minimum(pos + 1, wl).astype(F32)
    d = (wsum / cnt - p).astype(BF16)
    ys = [jnp.dot(d[:, g * POOL_GW:(g + 1) * POOL_GW], pw_ref[g], preferred_element_type=F32)
          for g in range(len(POOL_WINDOWS))]
    y_ref[0] = jnp.concatenate(ys, axis=1) * sc_ref[...]


def _pool(p, prev16, pool_w_bf, scale, *, tq, pos0):
    b, t, c = p.shape
    nq = t // tq
    hb = tq // POOL_HALO
    return pl.pallas_call(
        functools.partial(_pool_body, tq=tq, pos0=pos0),
        grid=(b, nq),
        in_specs=[pl.BlockSpec((1, POOL_HALO, c), lambda bi, i: (bi, 0, 0)),
                  pl.BlockSpec((1, POOL_HALO, c), lambda bi, i: (bi, jnp.maximum(i * hb - 1, 0), 0)),
                  pl.BlockSpec((1, tq, c), lambda bi, i: (bi, i, 0)),
                  pl.BlockSpec((len(POOL_WINDOWS), POOL_GW, POOL_GW), lambda bi, i: (0, 0, 0)),
                  pl.BlockSpec((1, c), lambda bi, i: (0, 0))],
        out_specs=pl.BlockSpec((1, tq, c), lambda bi, i: (bi, i, 0)),
        out_shape=jax.ShapeDtypeStruct((b, t, c), F32),
        compiler_params=_cparams(("arbitrary", "arbitrary")),
        name="pool",
    )(prev16, p, p, pool_w_bf, scale)


def _bucket_np(dist):
    n = np.maximum(dist, 0)
    max_exact = N_BUCKETS // 2
    nf = np.maximum(n, 1).astype(np.float32)
    large = max_exact + (np.log(nf / np.float32(max_exact)) / np.float32(math.log(REL_MAX_DIST / max_exact))
                         * np.float32(N_BUCKETS - max_exact)).astype(np.int32)
    large = np.minimum(large, N_BUCKETS - 1)
    return np.where(n < max_exact, n, large)


def _far_bucket(min_dist, max_dist):
    b = _bucket_np(np.arange(min_dist, max_dist + 1))
    assert (b == b[0]).all()
    return int(b[0])


def _moba_prompt_body(cfar_ref, q_ref, kb_ref, vt_ref, km_ref, bown_ref, bprev_ref, o_ref,
                      sel_ref, qz_ref, m_ref, l_ref, acc_ref, *, nb):
    i = pl.program_id(1)
    tq = MOBA_BLOCK
    blk = lax.broadcasted_iota(jnp.int32, (nb, tq), 0)
    valid = blk < i
    lane = lax.broadcasted_iota(jnp.int32, (tq, LANES), 1)
    for hp in range(N_HEADS // 2):
        qf = q_ref[0, :, LANES * hp:LANES * (hp + 1)]
        km = km_ref[0, :, LANES * hp:LANES * (hp + 1)].astype(BF16)
        for hh in range(2):
            h = 2 * hp + hh
            qz = jnp.where((lane >= HEAD_DIM * hh) & (lane < HEAD_DIM * (hh + 1)), qf, 0.0).astype(BF16)
            qz_ref[h] = qz
            sel = _top_mask(_nt_dot(km, qz), valid, MOBA_TOPK, nb).astype(F32)
            for jj in range(nb):
                sel_ref[h, jj] = sel[jj:jj + 1, :]
            m_ref[h] = jnp.full((1, tq), -1e30, F32)
            l_ref[h] = jnp.zeros((1, tq), F32)
            acc_ref[h] = jnp.zeros((HEAD_DIM, tq), F32)

    def key_block(j, bias_of, masked):
        sts = [_nt_dot(kb_ref[0, j, :, LANES * (h // 2):LANES * (h // 2 + 1)], qz_ref[h])
               for h in range(N_HEADS)]
        ps, alphas = [], []
        for h in range(N_HEADS):
            st = sts[h] + bias_of(h)
            if masked:
                st = jnp.where(sel_ref[h, j] > 0.0, st, NEG_INF)
            m = m_ref[h]
            m_new = jnp.maximum(m, jnp.max(st, axis=0, keepdims=True))
            alpha = jnp.exp(m - m_new)
            p = jnp.exp(st - m_new)
            l_ref[h] = alpha * l_ref[h] + jnp.sum(p, axis=0, keepdims=True)
            m_ref[h] = m_new
            ps.append(p.astype(BF16))
            alphas.append(alpha)
        for h in range(N_HEADS):
            vt = vt_ref[0, j, HEAD_DIM * h:HEAD_DIM * (h + 1), :]
            acc_ref[h] = alphas[h] * acc_ref[h] + jnp.dot(vt, ps[h], preferred_element_type=F32)

    def far_body(j, carry):
        key_block(j, lambda h: cfar_ref[h], True)
        return carry

    lax.fori_loop(0, i - 1, far_body, 0)
    jp = jnp.maximum(i - 1, 0)
    key_block(jp, lambda h: bprev_ref[h], True)
    key_block(i, lambda h: bown_ref[h], False)
    for hp in range(N_HEADS // 2):
        pair = jnp.concatenate([acc_ref[2 * hp] / l_ref[2 * hp], acc_ref[2 * hp + 1] / l_ref[2 * hp + 1]],
                               axis=0)
        o_ref[0, :, LANES * hp:LANES * (hp + 1)] = pair.T


def _moba_prompt(q, kb, vt, km, cfar, b_own, b_prev):
    b, t, _ = q.shape
    nb = t // MOBA_BLOCK
    once = pl.Buffered(1)
    return pl.pallas_call(
        functools.partial(_moba_prompt_body, nb=nb),
        grid=(b, nb),
        in_specs=[pl.BlockSpec(memory_space=pltpu.SMEM),
                  pl.BlockSpec((1, MOBA_BLOCK, ATT_WIDTH), lambda bi, i: (bi, i, 0)),
                  pl.BlockSpec((1, nb, MOBA_BLOCK, ATT_WIDTH), lambda bi, i: (bi, 0, 0, 0), pipeline_mode=once),
                  pl.BlockSpec((1, nb, ATT_WIDTH, MOBA_BLOCK), lambda bi, i: (bi, 0, 0, 0), pipeline_mode=once),
                  pl.BlockSpec((1, nb, ATT_WIDTH), lambda bi, i: (bi, 0, 0)),
                  pl.BlockSpec((N_HEADS, MOBA_BLOCK, MOBA_BLOCK), lambda bi, i: (0, 0, 0), pipeline_mode=once),
                  pl.BlockSpec((N_HEADS, MOBA_BLOCK, MOBA_BLOCK), lambda bi, i: (0, 0, 0), pipeline_mode=once)],
        out_specs=pl.BlockSpec((1, MOBA_BLOCK, ATT_WIDTH), lambda bi, i: (bi, i, 0)),
        out_shape=jax.ShapeDtypeStruct((b, t, ATT_WIDTH), F32),
        scratch_shapes=[pltpu.VMEM((N_HEADS, nb, 1, MOBA_BLOCK), F32),
                        pltpu.VMEM((N_HEADS, MOBA_BLOCK, LANES), BF16),
                        pltpu.VMEM((N_HEADS, 1, MOBA_BLOCK), F32),
                        pltpu.VMEM((N_HEADS, 1, MOBA_BLOCK), F32),
                        pltpu.VMEM((N_HEADS, HEAD_DIM, MOBA_BLOCK), F32)],
        compiler_params=_cparams(("arbitrary", "arbitrary")),
        name="moba_prompt",
    )(cfar, q, kb, vt, km, b_own, b_prev)


SAMPLE_PAGES_PER_STEP = 8
PAGES_PER_BLOCK = MOBA_BLOCK // PAGE_SIZE


def _moba_sample_body(pt_ref, q_ref, knew_ref, vnew_ref, cfar_ref, blast_ref, bown_ref, *rest,
                      nbs, ts):
    npg = SAMPLE_PAGES_PER_STEP
    ppb = PAGES_PER_BLOCK
    k_refs = rest[:npg]
    v_refs = rest[npg:2 * npg]
    o_ref = rest[2 * npg]
    gs_sc = rest[2 * npg + 1]
    m_sc = rest[2 * npg + 2:2 * npg + 2 + ppb]
    l_sc = rest[2 * npg + 2 + ppb:2 * npg + 2 + 2 * ppb]
    o_sc = rest[2 * npg + 2 + 2 * ppb:2 * npg + 2 + 3 * ppb]
    s = pl.program_id(1)
    nsteps = pl.num_programs(1)
    bps = npg // ppb
    nr = ts * N_HEADS
    nc = PAGE_SIZE * N_HEADS
    qb = q_ref[0].astype(BF16)
    qf = qb.astype(F32)
    row = lax.broadcasted_iota(jnp.int32, (nr, nc), 0)
    col = lax.broadcasted_iota(jnp.int32, (nr, nc), 1)
    same_head = (col % N_HEADS) == (row % N_HEADS)
    sts = []
    for jj in range(bps):
        ksum = jnp.zeros((N_HEADS, HEAD_DIM), F32)
        for u in range(ppb):
            kp = k_refs[ppb * jj + u][...]
            ksum = ksum + jnp.sum(kp, axis=0)
            sts.append(_nt_dot(qb, kp.reshape(nc, HEAD_DIM).astype(BF16)))
        kmean = (ksum * (1.0 / MOBA_BLOCK)).astype(BF16).astype(F32)
        gs_sc[s * bps + jj] = jnp.sum(qf * jnp.concatenate([kmean] * ts, axis=0), axis=1, keepdims=True)
    ps = []
    for jj in range(bps):
        j = s * bps + jj
        for u in range(ppb):
            bias = jnp.where(j == nbs - 1, blast_ref[u], cfar_ref[...])
            st = jnp.where(same_head, sts[ppb * jj + u] + bias, NEG_INF)
            m = jnp.max(st, axis=1, keepdims=True)
            p = jnp.exp(st - m)
            m_sc[u][j] = m
            l_sc[u][j] = jnp.sum(p, axis=1, keepdims=True)
            ps.append(p.astype(BF16))
    for jj in range(bps):
        for u in range(ppb):
            vflat = v_refs[ppb * jj + u][...].reshape(nc, HEAD_DIM).astype(BF16)
            o_sc[u][s * bps + jj] = jnp.dot(ps[ppb * jj + u], vflat, preferred_element_type=F32)

    @pl.when(s == nsteps - 1)
    def _():
        gs = gs_sc[...]
        sel = _top_mask(gs, jnp.ones(gs.shape, jnp.bool_), MOBA_TOPK, nbs)
        knew = knew_ref[0].astype(BF16).astype(F32)
        vnew = vnew_ref[0].astype(BF16).astype(F32)
        s_own = [jnp.sum(qf * jnp.concatenate([knew[t]] * ts, axis=0), axis=1, keepdims=True)
                 + bown_ref[:, t:t + 1] for t in range(ts)]
        mtot = s_own[0]
        for t in range(1, ts):
            mtot = jnp.maximum(mtot, s_own[t])
        for u in range(ppb):
            mtot = jnp.maximum(mtot, jnp.max(jnp.where(sel, m_sc[u][...], NEG_INF), axis=0))
        ltot = jnp.zeros((nr, 1), F32)
        for u in range(ppb):
            w = jnp.where(sel, jnp.exp(m_sc[u][...] - mtot[None]), 0.0)
            ltot = ltot + jnp.sum(w * l_sc[u][...], axis=0)
            m_sc[u][...] = w

        def merge(j, acc):
            for u in range(ppb):
                acc = acc + m_sc[u][j] * o_sc[u][j]
            return acc

        otot = lax.fori_loop(0, nbs, merge, jnp.zeros((nr, HEAD_DIM), F32))
        for t in range(ts):
            pt = jnp.exp(s_own[t] - mtot)
            ltot = ltot + pt
            otot = otot + pt.astype(BF16).astype(F32) * jnp.concatenate([vnew[t]] * ts, axis=0)
        o_ref[0] = otot / ltot


def _moba_sample(page_table, q, knew, vnew, cfar_rows, b_last, b_own, cache_k, cache_v, layer):
    db, nr, _ = q.shape
    ts = nr // N_HEADS
    n_pages = page_table.shape[1]
    nbs = n_pages // PAGES_PER_BLOCK
    npg = SAMPLE_PAGES_PER_STEP
    nsteps = n_pages // npg
    nc = PAGE_SIZE * N_HEADS

    def page_spec(u):
        return pl.BlockSpec((None, PAGE_SIZE, None, N_HEADS, HEAD_DIM),
                            lambda b, s, pt: (pt[b, s * npg + u], 0, layer, 0, 0))

    grid_spec = pltpu.PrefetchScalarGridSpec(
        num_scalar_prefetch=1,
        grid=(db, nsteps),
        in_specs=[pl.BlockSpec((1, nr, HEAD_DIM), lambda b, s, pt: (b, 0, 0)),
                  pl.BlockSpec((1, ts, N_HEADS, HEAD_DIM), lambda b, s, pt: (b, 0, 0, 0)),
                  pl.BlockSpec((1, ts, N_HEADS, HEAD_DIM), lambda b, s, pt: (b, 0, 0, 0)),
                  pl.BlockSpec((nr, 1), lambda b, s, pt: (0, 0)),
                  pl.BlockSpec((PAGES_PER_BLOCK, nr, nc), lambda b, s, pt: (0, 0, 0)),
                  pl.BlockSpec((nr, ts), lambda b, s, pt: (0, 0))]
                 + [page_spec(u) for u in range(npg)] + [page_spec(u) for u in range(npg)],
        out_specs=pl.BlockSpec((1, nr, HEAD_DIM), lambda b, s, pt: (b, 0, 0)),
        scratch_shapes=[pltpu.VMEM((nbs, nr, 1), F32)]
                       + [pltpu.VMEM((nbs, nr, 1), F32) for _ in range(2 * PAGES_PER_BLOCK)]
                       + [pltpu.VMEM((nbs, nr, HEAD_DIM), F32) for _ in range(PAGES_PER_BLOCK)],
    )
    return pl.pallas_call(
        functools.partial(_moba_sample_body, nbs=nbs, ts=ts),
        grid_spec=grid_spec,
        out_shape=jax.ShapeDtypeStruct((db, nr, HEAD_DIM), F32),
        compiler_params=_cparams(("arbitrary", "arbitrary")),
        name="moba_sample",
    )(page_table, q, knew, vnew, cfar_rows, b_last, b_own, *([cache_k] * npg), *([cache_v] * npg))


def _layer_norm(h, g, b):
    mu = jnp.mean(h, axis=-1, keepdims=True)
    c = h - mu
    var = jnp.mean(c * c, axis=-1, keepdims=True)
    return c * lax.rsqrt(var + LN_EPS) * g + b


def _argmax_first(vals):
    best, idx = vals[0], jnp.zeros(vals[0].shape, jnp.int32)
    for k in range(1, len(vals)):
        upd = vals[k] > best
        idx = jnp.where(upd, k, idx)
        best = jnp.where(upd, vals[k], best)
    return best, idx


def _route_rows(logit_rows):
    mx = functools.reduce(jnp.maximum, logit_rows)
    ex = [jnp.exp(r - mx) for r in logit_rows]
    tot = functools.reduce(lambda a, b: a + b, ex)
    probs = [e / tot for e in ex]
    scores = []
    for g in range(N_GROUPS):
        a, b, c, d = probs[EXPERTS_PER_GROUP * g:EXPERTS_PER_GROUP * (g + 1)]
        s1, t1 = jnp.maximum(a, b), jnp.minimum(a, b)
        s2, t2 = jnp.maximum(c, d), jnp.minimum(c, d)
        scores.append(jnp.maximum(s1, s2) + jnp.maximum(jnp.minimum(s1, s2), jnp.maximum(t1, t2)))
    _, gi = _argmax_first(scores)
    ing = []
    for j in range(EXPERTS_PER_GROUP):
        v = probs[j]
        for g in range(1, N_GROUPS):
            v = jnp.where(gi == g, probs[EXPERTS_PER_GROUP * g + j], v)
        ing.append(v)
    w1, i1 = _argmax_first(ing)
    w2, i2 = _argmax_first([jnp.where(i1 == j, -1.0, ing[j]) for j in range(EXPERTS_PER_GROUP)])
    den = w1 + w2
    e1 = (gi * EXPERTS_PER_GROUP + i1).astype(F32)
    e2 = (gi * EXPERTS_PER_GROUP + i2).astype(F32)
    return e1, e2, w1 / den, w2 / den


def _merge_body(x_ref, yp_ref, o_ref, gp_ref, ga_ref, wpo_ref, wao_ref, wo_ref, g_ref, b_ref,
                wr_ref, br_ref, x1_ref, rt_ref, *, tm):
    a = jnp.dot(yp_ref[...].astype(BF16), wpo_ref[...], preferred_element_type=F32)
    bb = jnp.dot(o_ref[...].astype(BF16), wao_ref[...], preferred_element_type=F32)
    merged = gp_ref[...] * a + ga_ref[...] * bb
    mix = jnp.dot(merged.astype(BF16), wo_ref[...], preferred_element_type=F32)
    x1 = _layer_norm(DEEPNORM_ALPHA * x_ref[...] + mix, g_ref[...], b_ref[...])
    x1_ref[...] = x1
    lg = jnp.dot(x1.astype(BF16), wr_ref[...], preferred_element_type=F32) + br_ref[...]
    lgt = lg.T
    e1, e2, g1, g2 = _route_rows([lgt[e:e + 1, :] for e in range(N_EXPERTS)])
    r = lax.broadcasted_iota(jnp.int32, (8, tm), 0)
    rt_ref[0] = jnp.where(r == 0, e1, jnp.where(r == 1, e2, jnp.where(r == 2, g1, jnp.where(r == 3, g2, 0.0))))


def _merge(x, yp, o, gp, ga, wpo, wao, wo, g, b, wr, br, *, tm):
    m = x.shape[0]
    row = lambda n: pl.BlockSpec((tm, n), lambda i: (i, 0))
    full = lambda r, c: pl.BlockSpec((r, c), lambda i: (0, 0))
    return pl.pallas_call(
        functools.partial(_merge_body, tm=tm),
        grid=(m // tm,),
        in_specs=[row(D_MODEL), row(POOL_WIDTH), row(ATT_WIDTH), row(D_MODEL), row(D_MODEL),
                  full(POOL_WIDTH, D_MODEL), full(ATT_WIDTH, D_MODEL), full(D_MODEL, D_MODEL),
                  full(1, D_MODEL), full(1, D_MODEL), full(D_MODEL, ROUTER_PAD), full(1, ROUTER_PAD)],
        out_specs=[row(D_MODEL), pl.BlockSpec((1, 8, tm), lambda i: (i, 0, 0))],
        out_shape=[jax.ShapeDtypeStruct((m, D_MODEL), F32), jax.ShapeDtypeStruct((m // tm, 8, tm), F32)],
        compiler_params=_cparams(("arbitrary",)),
        name="merge",
    )(x, yp, o, gp, ga, wpo, wao, wo, g, b, wr, br)


def _unpack_route(rt):
    n = rt.shape[0] * rt.shape[2]
    cols = rt[:, :4, :].transpose(0, 2, 1).reshape(n, 4)
    return cols[:, :2].astype(jnp.int32), cols[:, 2:]


def _experts_body(be_ref, xs_ref, wg_ref, wu_ref, wd_ref, ys_ref):
    xb = xs_ref[...].astype(BF16)
    h1 = jnp.dot(xb, wg_ref[0], preferred_element_type=F32)
    h2 = jnp.dot(xb, wu_ref[0], preferred_element_type=F32)
    act = (h1 * jax.nn.sigmoid(h1) * h2).astype(BF16)
    ys_ref[...] = jnp.dot(act, wd_ref[0], preferred_element_type=F32)


def _experts(blk_expert, xs, wg, wu, wd):
    n_rows = xs.shape[0]
    n_blk = n_rows // EXPERT_ROWS
    grid_spec = pltpu.PrefetchScalarGridSpec(
        num_scalar_prefetch=1,
        grid=(n_blk,),
        in_specs=[pl.BlockSpec((EXPERT_ROWS, D_MODEL), lambda i, be: (i, 0)),
                  pl.BlockSpec((1, D_MODEL, D_EXPERT), lambda i, be: (be[i], 0, 0)),
                  pl.BlockSpec((1, D_MODEL, D_EXPERT), lambda i, be: (be[i], 0, 0)),
                  pl.BlockSpec((1, D_EXPERT, D_MODEL), lambda i, be: (be[i], 0, 0))],
        out_specs=pl.BlockSpec((EXPERT_ROWS, D_MODEL), lambda i, be: (i, 0)),
    )
    return pl.pallas_call(
        _experts_body,
        grid_spec=grid_spec,
        out_shape=jax.ShapeDtypeStruct((n_rows, D_MODEL), F32),
        compiler_params=_cparams(("arbitrary",)),
        name="experts",
    )(blk_expert, xs, wg, wu, wd)


def _ln2_body(x_ref, f_ref, g_ref, b_ref, o_ref):
    o_ref[...] = _layer_norm(DEEPNORM_ALPHA * x_ref[...] + f_ref[...], g_ref[...], b_ref[...])


def _ln2(x, f, g, b, *, tm):
    m = x.shape[0]
    row = pl.BlockSpec((tm, D_MODEL), lambda i: (i, 0))
    vec = pl.BlockSpec((1, D_MODEL), lambda i: (0, 0))
    return pl.pallas_call(
        _ln2_body,
        grid=(m // tm,),
        in_specs=[row, row, vec, vec],
        out_specs=row,
        out_shape=jax.ShapeDtypeStruct((m, D_MODEL), F32),
        compiler_params=_cparams(("arbitrary",)),
        name="ln2",
    )(x, f, g, b)


def _moe(h, expert, gate, wg, wu, wd):
    n = h.shape[0]
    s = n * TOP_K
    e_flat = expert.reshape(-1)
    onehot = (e_flat[:, None] == jnp.arange(N_EXPERTS, dtype=jnp.int32)[None, :]).astype(jnp.int32)
    csum = jnp.cumsum(onehot, axis=0)
    counts = csum[-1]
    pos_in = jnp.take_along_axis(csum, e_flat[:, None], axis=1)[:, 0] - 1
    padded = (counts + EXPERT_ROWS - 1) // EXPERT_ROWS * EXPERT_ROWS
    pad_end = jnp.cumsum(padded)
    pad_start = pad_end - padded
    dest = pad_start[e_flat] + pos_in
    n_blk = (s + N_EXPERTS * (EXPERT_ROWS - 1) + EXPERT_ROWS - 1) // EXPERT_ROWS
    tok = jnp.arange(s, dtype=jnp.int32) // TOP_K
    xs = jnp.zeros((n_blk * EXPERT_ROWS, D_MODEL), h.dtype).at[dest].set(h[tok])
    blk_expert = jnp.minimum(
        jnp.searchsorted(pad_end, jnp.arange(n_blk, dtype=jnp.int32) * EXPERT_ROWS, side="right"),
        N_EXPERTS - 1).astype(jnp.int32)
    ys = _experts(blk_expert, xs, wg, wu, wd)
    y_slot = ys[dest] * gate.reshape(-1)[:, None]
    return y_slot.reshape(n, TOP_K, D_MODEL).sum(axis=1)


def kernel(x_prompt, x_sample, cache_k, cache_v, state_pool, page_table, rel_bias, w_router, b_router,
           w_in, pool_w, pool_scale, w_pool_out, w_attn_out, w_o, ln1_g, ln1_b, w_gate, w_up, w_down,
           ln2_g, ln2_b):
    bsz, seq, _ = x_prompt.shape
    db, ts, _ = x_sample.shape
    past_len = page_table.shape[1] * PAGE_SIZE
    nb = seq // MOBA_BLOCK
    nbs = past_len // MOBA_BLOCK
    assert seq % MOBA_BLOCK == 0 and past_len % MOBA_BLOCK == 0 and ts <= 8
    np_rows = bsz * seq
    ns_rows = db * ts

    far_p = _far_bucket(MOBA_BLOCK + 1, seq)
    far_s = _far_bucket(MOBA_BLOCK + 1, past_len + ts)
    c = np.arange(MOBA_BLOCK)
    d_own = c[None, :] - c[:, None]
    tab_own = rel_bias[_bucket_np(np.maximum(d_own, 0))]
    b_own_p = jnp.where((d_own >= 0)[None], tab_own.transpose(2, 0, 1), NEG_INF)
    b_prev_p = rel_bias[_bucket_np(d_own + MOBA_BLOCK)].transpose(2, 0, 1)
    cfar_p = rel_bias[far_p]
    t_idx = np.repeat(np.arange(ts), N_HEADS)
    h_idx = np.tile(np.arange(N_HEADS), ts)
    key_in_page = np.repeat(np.arange(PAGE_SIZE), N_HEADS)
    d_last = np.stack([MOBA_BLOCK + t_idx[:, None] - (u * PAGE_SIZE + key_in_page)[None, :]
                       for u in range(PAGES_PER_BLOCK)])
    b_last_s = rel_bias[_bucket_np(d_last), h_idx[None, :, None]]
    tn = np.arange(ts)
    d_new = t_idx[:, None] - tn[None, :]
    b_own_s = jnp.where(d_new >= 0, rel_bias[_bucket_np(np.maximum(d_new, 0)), h_idx[:, None]], NEG_INF)
    cfar_s = rel_bias[far_s][h_idx][:, None]

    wr_pad = jnp.zeros((D_MODEL, ROUTER_PAD), BF16).at[:, :N_EXPERTS].set(w_router.astype(BF16))
    br_pad = jnp.zeros((1, ROUTER_PAD), F32).at[0, :N_EXPERTS].set(b_router)

    xp = x_prompt.reshape(np_rows, D_MODEL)
    xs = x_sample.reshape(ns_rows, D_MODEL)
    kp, vp, pp, ksm, vsm, psm = [], [], [], [], [], []
    for l in range(DEPTH):
        w_in_bf = w_in[l].astype(BF16)
        pool_w_bf = pool_w[l].astype(BF16)
        scale = pool_scale[l][None, :]
        wpo = w_pool_out[l].astype(BF16)
        wao = w_attn_out[l].astype(BF16)
        wo = w_o[l].astype(BF16)
        wg = w_gate[l].astype(BF16)
        wu = w_up[l].astype(BF16)
        wd = w_down[l].astype(BF16)
        g1, b1 = ln1_g[l][None, :], ln1_b[l][None, :]
        g2, b2 = ln2_g[l][None, :], ln2_b[l][None, :]

        p_p, q_p, k_p, v_p, gp_p, ga_p, kb_p, vt_p, km_p = _inproj(xp, w_in_bf, tm=MOBA_BLOCK, attn_layouts=True)
        prev0 = jnp.zeros((bsz, POOL_HALO, POOL_WIDTH), F32)
        yp_p = _pool(p_p.reshape(bsz, seq, POOL_WIDTH), prev0, pool_w_bf, scale, tq=MOBA_BLOCK, pos0=0)
        o_p = _moba_prompt(q_p.reshape(bsz, seq, ATT_WIDTH),
                           kb_p.reshape(bsz, nb, MOBA_BLOCK, ATT_WIDTH),
                           vt_p.reshape(bsz, nb, ATT_WIDTH, MOBA_BLOCK),
                           km_p.reshape(bsz, nb, ATT_WIDTH), cfar_p, b_own_p, b_prev_p)
        x1_p, rt_p = _merge(xp, yp_p.reshape(np_rows, POOL_WIDTH), o_p.reshape(np_rows, ATT_WIDTH),
                            gp_p, ga_p, wpo, wao, wo, g1, b1, wr_pad, br_pad, tm=MOBA_BLOCK)

        p_s, q_s, k_s, v_s, gp_s, ga_s = _inproj(xs, w_in_bf, tm=ns_rows, attn_layouts=False)
        state = state_pool[:, l]
        prev_s = jnp.concatenate([jnp.zeros((db, 1, POOL_WIDTH), F32), state], axis=1)
        p_s3 = p_s.reshape(db, ts, POOL_WIDTH)
        p_s8 = jnp.pad(p_s3, ((0, 0), (0, 8 - ts), (0, 0)))
        yp_s = _pool(p_s8, prev_s, pool_w_bf, scale, tq=8, pos0=past_len)[:, :ts]
        o_s = _moba_sample(page_table, q_s.reshape(db, ts * N_HEADS, HEAD_DIM),
                           k_s.reshape(db, ts, N_HEADS, HEAD_DIM), v_s.reshape(db, ts, N_HEADS, HEAD_DIM),
                           cfar_s, b_last_s, b_own_s, cache_k, cache_v, l)
        x1_s, rt_s = _merge(xs, yp_s.reshape(ns_rows, POOL_WIDTH), o_s.reshape(ns_rows, ATT_WIDTH),
                            gp_s, ga_s, wpo, wao, wo, g1, b1, wr_pad, br_pad, tm=ns_rows)

        x1 = jnp.concatenate([x1_p, x1_s], axis=0)
        e_p, gt_p = _unpack_route(rt_p)
        e_s, gt_s = _unpack_route(rt_s)
        f = _moe(x1, jnp.concatenate([e_p, e_s], axis=0), jnp.concatenate([gt_p, gt_s], axis=0), wg, wu, wd)
        x2 = _ln2(x1, f, g2, b2, tm=EXPERT_ROWS)
        xp, xs = x2[:np_rows], x2[np_rows:]

        kp.append(k_p.reshape(bsz, seq, N_HEADS, HEAD_DIM))
        vp.append(v_p.reshape(bsz, seq, N_HEADS, HEAD_DIM))
        pp.append(p_p.reshape(bsz, seq, POOL_WIDTH)[:, seq - POOL_BUF:])
        ksm.append(k_s.reshape(db, ts, N_HEADS, HEAD_DIM))
        vsm.append(v_s.reshape(db, ts, N_HEADS, HEAD_DIM))
        psm.append(jnp.concatenate([state, p_s3], axis=1)[:, -POOL_BUF:])

    return (xp.reshape(bsz, seq, D_MODEL), xs.reshape(db, ts, D_MODEL),
            jnp.stack(kp, axis=2), jnp.stack(vp, axis=2), jnp.stack(pp, axis=1),
            jnp.stack(ksm, axis=2), jnp.stack(vsm, axis=2), jnp.stack(psm, axis=1))
```

```python
import functools
import math

import numpy as np
import jax
import jax.numpy as jnp
from jax import lax
from jax.experimental import pallas as pl
from jax.experimental.pallas import tpu as pltpu

F32 = jnp.float32
BF16 = jnp.bfloat16
NEG_INF = float("-inf")

D_MODEL = 1024
DEPTH = 2
N_HEADS = 8
HEAD_DIM = 64
ATT_WIDTH = N_HEADS * HEAD_DIM
MOBA_BLOCK = 256
MOBA_TOPK = 3
N_BUCKETS = 32
REL_MAX_DIST = 128
POOL_WINDOWS = (2, 4, 8, 16)
POOL_WIDTH = 512
POOL_GW = 128
POOL_BUF = 15
POOL_HALO = 16
IN_WIDTH = POOL_WIDTH + 3 * ATT_WIDTH + 2 * D_MODEL
N_EXPERTS = 16
N_GROUPS = 4
EXPERTS_PER_GROUP = 4
TOP_K = 2
D_EXPERT = 512
EXPERT_ROWS = 128
DEEPNORM_ALPHA = (2 * DEPTH) ** 0.25
LN_EPS = 1e-5
PAGE_SIZE = 128
ROUTER_PAD = 128
LANES = 128

VMEM_LIMIT = 52 * 1024 * 1024


def _cparams(sem):
    return pltpu.CompilerParams(dimension_semantics=sem, vmem_limit_bytes=VMEM_LIMIT)


def _nt_dot(a, b):
    return lax.dot_general(a, b, (((1,), (1,)), ((), ())), preferred_element_type=F32)


def _top_mask(scores, cand, k, n):
    idx = lax.broadcasted_iota(jnp.int32, scores.shape, 0)
    sel = jnp.zeros(scores.shape, jnp.bool_)
    for _ in range(k):
        c = cand & jnp.logical_not(sel)
        cur = jnp.where(c, scores, NEG_INF)
        mx = jnp.max(cur, axis=0, keepdims=True)
        first = jnp.min(jnp.where(c & (cur == mx), idx, n), axis=0, keepdims=True)
        sel = sel | (idx == first)
    return sel


def _inproj_body(x_ref, w_ref, p_ref, q_ref, k_ref, v_ref, gp_ref, ga_ref, *attn_refs, tm):
    xb = x_ref[...].astype(BF16)

    def seg(a, b):
        return jnp.dot(xb, w_ref[:, a:b], preferred_element_type=F32)

    c0 = POOL_WIDTH
    p_ref[...] = seg(0, c0)
    q_ref[...] = seg(c0, c0 + ATT_WIDTH) * (HEAD_DIM ** -0.5)
    k = seg(c0 + ATT_WIDTH, c0 + 2 * ATT_WIDTH)
    k_ref[...] = k
    v = seg(c0 + 2 * ATT_WIDTH, c0 + 3 * ATT_WIDTH)
    v_ref[...] = v
    g0 = c0 + 3 * ATT_WIDTH
    gp_ref[...] = jax.nn.sigmoid(seg(g0, g0 + D_MODEL))
    ga_ref[...] = jax.nn.sigmoid(seg(g0 + D_MODEL, g0 + 2 * D_MODEL))
    if attn_refs:
        kb_ref, vt_ref, km_ref = attn_refs
        kb_ref[...] = k.astype(BF16)
        vt_ref[0] = v.T.astype(BF16)
        km_ref[0] = jnp.sum(k, axis=0, keepdims=True) * (1.0 / tm)


def _inproj(x, w_bf, *, tm, attn_layouts):
    m = x.shape[0]
    nt = m // tm
    row = lambda n: pl.BlockSpec((tm, n), lambda i: (i, 0))
    out_shape = [jax.ShapeDtypeStruct((m, POOL_WIDTH), F32),
                 jax.ShapeDtypeStruct((m, ATT_WIDTH), F32),
                 jax.ShapeDtypeStruct((m, ATT_WIDTH), F32),
                 jax.ShapeDtypeStruct((m, ATT_WIDTH), F32),
                 jax.ShapeDtypeStruct((m, D_MODEL), F32),
                 jax.ShapeDtypeStruct((m, D_MODEL), F32)]
    out_specs = [row(POOL_WIDTH), row(ATT_WIDTH), row(ATT_WIDTH), row(ATT_WIDTH),
                 row(D_MODEL), row(D_MODEL)]
    if attn_layouts:
        assert tm == MOBA_BLOCK
        out_shape += [jax.ShapeDtypeStruct((m, ATT_WIDTH), BF16),
                      jax.ShapeDtypeStruct((nt, ATT_WIDTH, tm), BF16),
                      jax.ShapeDtypeStruct((nt, 1, ATT_WIDTH), F32)]
        out_specs += [row(ATT_WIDTH),
                      pl.BlockSpec((1, ATT_WIDTH, tm), lambda i: (i, 0, 0)),
                      pl.BlockSpec((1, 1, ATT_WIDTH), lambda i: (i, 0, 0))]
    return pl.pallas_call(
        functools.partial(_inproj_body, tm=tm),
        grid=(nt,),
        in_specs=[row(D_MODEL), pl.BlockSpec((D_MODEL, IN_WIDTH), lambda i: (0, 0))],
        out_specs=out_specs,
        out_shape=out_shape,
        compiler_params=_cparams(("arbitrary",)),
        name="inproj",
    )(x, w_bf)


def _pool_body(prev_ref, halo_ref, p_ref, pw_ref, sc_ref, y_ref, *, tq, pos0):
    i = pl.program_id(1)
    p = p_ref[0]
    halo = jnp.where(i == 0, prev_ref[0], halo_ref[0])
    z = jnp.concatenate([halo, p], axis=0)
    lane = lax.broadcasted_iota(jnp.int32, z.shape, 1)
    x = z
    for s in (8, 4, 2, 1):
        thr = POOL_WIDTH - POOL_GW * {8: 1, 4: 2, 2: 3, 1: 4}[s]
        sh = pltpu.roll(x, s, 0)
        x = x + (jnp.where(lane >= thr, sh, 0.0) if thr > 0 else sh)
    wsum = x[POOL_HALO:, :]
    row = lax.broadcasted_iota(jnp.int32, (tq, POOL_WIDTH), 0)
    lane2 = lax.broadcasted_iota(jnp.int32, (tq, POOL_WIDTH), 1)
    wl = jnp.where(lane2 < POOL_GW, POOL_WINDOWS[0],
                   jnp.where(lane2 < 2 * POOL_GW, POOL_WINDOWS[1],
                             jnp.where(lane2 < 3 * POOL_GW, POOL_WINDOWS[2], POOL_WINDOWS[3])))
    pos = pos0 + i * tq + row
    cnt = jnp.minimum(pos + 1, wl).astype(F32)
    d = (wsum / cnt - p).astype(BF16)
    ys = [jnp.dot(d[:, g * POOL_GW:(g + 1) * POOL_GW], pw_ref[g], preferred_element_type=F32)
          for g in range(len(POOL_WINDOWS))]
    y_ref[0] = jnp.concatenate(ys, axis=1) * sc_ref[...]


def _pool(p, prev16, pool_w_bf, scale, *, tq, pos0):
    b, t, c = p.shape
    nq = t // tq
    hb = tq // POOL_HALO
    return pl.pallas_call(
        functools.partial(_pool_body, tq=tq, pos0=pos0),
        grid=(b, nq),
        in_specs=[pl.BlockSpec((1, POOL_HALO, c), lambda bi, i: (bi, 0, 0)),
                  pl.BlockSpec((1, POOL_HALO, c), lambda bi, i: (bi, jnp.maximum(i * hb - 1, 0), 0)),
                  pl.BlockSpec((1, tq, c), lambda bi, i: (bi, i, 0)),
                  pl.BlockSpec((len(POOL_WINDOWS), POOL_GW, POOL_GW), lambda bi, i: (0, 0, 0)),
                  pl.BlockSpec((1, c), lambda bi, i: (0, 0))],
        out_specs=pl.BlockSpec((1, tq, c), lambda bi, i: (bi, i, 0)),
        out_shape=jax.ShapeDtypeStruct((b, t, c), F32),
        compiler_params=_cparams(("arbitrary", "arbitrary")),
        name="pool",
    )(prev16, p, p, pool_w_bf, scale)


def _bucket_np(dist):
    n = np.maximum(dist, 0)
    max_exact = N_BUCKETS // 2
    nf = np.maximum(n, 1).astype(np.float32)
    large = max_exact + (np.log(nf / np.float32(max_exact)) / np.float32(math.log(REL_MAX_DIST / max_exact))
                         * np.float32(N_BUCKETS - max_exact)).astype(np.int32)
    large = np.minimum(large, N_BUCKETS - 1)
    return np.where(n < max_exact, n, large)


def _far_bucket(min_dist, max_dist):
    b = _bucket_np(np.arange(min_dist, max_dist + 1))
    assert (b == b[0]).all()
    return int(b[0])


def _moba_prompt_body(cfar_ref, q_ref, kb_ref, vt_ref, km_ref, bown_ref, bprev_ref, o_ref,
                      sel_ref, qz_ref, m_ref, l_ref, acc_ref, *, nb):
    i = pl.program_id(1)
    tq = MOBA_BLOCK
    blk = lax.broadcasted_iota(jnp.int32, (nb, tq), 0)
    valid = blk < i
    lane = lax.broadcasted_iota(jnp.int32, (tq, LANES), 1)
    for hp in range(N_HEADS // 2):
        qf = q_ref[0, :, LANES * hp:LANES * (hp + 1)]
        km = km_ref[0, :, LANES * hp:LANES * (hp + 1)].astype(BF16)
        for hh in range(2):
            h = 2 * hp + hh
            qz = jnp.where((lane >= HEAD_DIM * hh) & (lane < HEAD_DIM * (hh + 1)), qf, 0.0).astype(BF16)
            qz_ref[h] = qz
            sel = _top_mask(_nt_dot(km, qz), valid, MOBA_TOPK, nb).astype(F32)
            for jj in range(nb):
                sel_ref[h, jj] = sel[jj:jj + 1, :]
            m_ref[h] = jnp.full((1, tq), -1e30, F32)
            l_ref[h] = jnp.zeros((1, tq), F32)
            acc_ref[h] = jnp.zeros((HEAD_DIM, tq), F32)

    def key_block(j, bias_of, masked):
        sts = [_nt_dot(kb_ref[0, j, :, LANES * (h // 2):LANES * (h // 2 + 1)], qz_ref[h])
               for h in range(N_HEADS)]
        ps, alphas = [], []
        for h in range(N_HEADS):
            st = sts[h] + bias_of(h)
            if masked:
                st = jnp.where(sel_ref[h, j] > 0.0, st, NEG_INF)
            m = m_ref[h]
            m_new = jnp.maximum(m, jnp.max(st, axis=0, keepdims=True))
            alpha = jnp.exp(m - m_new)
            p = jnp.exp(st - m_new)
            l_ref[h] = alpha * l_ref[h] + jnp.sum(p, axis=0, keepdims=True)
            m_ref[h] = m_new
            ps.append(p.astype(BF16))
            alphas.append(alpha)
        for h in range(N_HEADS):
            vt = vt_ref[0, j, HEAD_DIM * h:HEAD_DIM * (h + 1), :]
            acc_ref[h] = alphas[h] * acc_ref[h] + jnp.dot(vt, ps[h], preferred_element_type=F32)

    def far_body(j, carry):
        key_block(j, lambda h: cfar_ref[h], True)
        return carry

    lax.fori_loop(0, i - 1, far_body, 0)
    jp = jnp.maximum(i - 1, 0)
    key_block(jp, lambda h: bprev_ref[h], True)
    key_block(i, lambda h: bown_ref[h], False)
    for hp in range(N_HEADS // 2):
        pair = jnp.concatenate([acc_ref[2 * hp] / l_ref[2 * hp], acc_ref[2 * hp + 1] / l_ref[2 * hp + 1]],
                               axis=0)
        o_ref[0, :, LANES * hp:LANES * (hp + 1)] = pair.T


def _moba_prompt(q, kb, vt, km, cfar, b_own, b_prev):
    b, t, _ = q.shape
    nb = t // MOBA_BLOCK
    once = pl.Buffered(1)
    return pl.pallas_call(
        functools.partial(_moba_prompt_body, nb=nb),
        grid=(b, nb),
        in_specs=[pl.BlockSpec(memory_space=pltpu.SMEM),
                  pl.BlockSpec((1, MOBA_BLOCK, ATT_WIDTH), lambda bi, i: (bi, i, 0)),
                  pl.BlockSpec((1, nb, MOBA_BLOCK, ATT_WIDTH), lambda bi, i: (bi, 0, 0, 0), pipeline_mode=once),
                  pl.BlockSpec((1, nb, ATT_WIDTH, MOBA_BLOCK), lambda bi, i: (bi, 0, 0, 0), pipeline_mode=once),
                  pl.BlockSpec((1, nb, ATT_WIDTH), lambda bi, i: (bi, 0, 0)),
                  pl.BlockSpec((N_HEADS, MOBA_BLOCK, MOBA_BLOCK), lambda bi, i: (0, 0, 0), pipeline_mode=once),
                  pl.BlockSpec((N_HEADS, MOBA_BLOCK, MOBA_BLOCK), lambda bi, i: (0, 0, 0), pipeline_mode=once)],
        out_specs=pl.BlockSpec((1, MOBA_BLOCK, ATT_WIDTH), lambda bi, i: (bi, i, 0)),
        out_shape=jax.ShapeDtypeStruct((b, t, ATT_WIDTH), F32),
        scratch_shapes=[pltpu.VMEM((N_HEADS, nb, 1, MOBA_BLOCK), F32),
                        pltpu.VMEM((N_HEADS, MOBA_BLOCK, LANES), BF16),
                        pltpu.VMEM((N_HEADS, 1, MOBA_BLOCK), F32),
                        pltpu.VMEM((N_HEADS, 1, MOBA_BLOCK), F32),
                        pltpu.VMEM((N_HEADS, HEAD_DIM, MOBA_BLOCK), F32)],
        compiler_params=_cparams(("arbitrary", "arbitrary")),
        name="moba_prompt",
    )(cfar, q, kb, vt, km, b_own, b_prev)


SAMPLE_PAGES_PER_STEP = 16
PAGES_PER_BLOCK = MOBA_BLOCK // PAGE_SIZE


def _moba_sample_body(pt_ref, qbd_ref, knew_ref, vnew_ref, cfar_ref, blast_ref, bown_ref, *rest,
                      nbs, ts):
    npg = SAMPLE_PAGES_PER_STEP
    ppb = PAGES_PER_BLOCK
    k_refs = rest[:npg]
    v_refs = rest[npg:2 * npg]
    o_ref = rest[2 * npg]
    gs_sc, m_sc, l_sc, o_sc = rest[2 * npg + 1:]
    s = pl.program_id(1)
    nsteps = pl.num_programs(1)
    bps = npg // ppb
    nr = ts * N_HEADS
    qb = qbd_ref[0].astype(BF16)
    lane0 = lax.broadcasted_iota(jnp.int32, (ATT_WIDTH, LANES), 1) == 0
    sts = []
    for jj in range(bps):
        kts = [k_refs[ppb * jj + u][...].reshape(ATT_WIDTH, PAGE_SIZE) for u in range(ppb)]
        ksum = functools.reduce(lambda a, b: a + b, [jnp.sum(kt, axis=1, keepdims=True) for kt in kts])
        kmean = ksum * (1.0 / MOBA_BLOCK)
        kmcol = jnp.where(lane0, kmean, 0.0).astype(BF16)
        w = jnp.concatenate([kt.astype(BF16) for kt in kts] + [kmcol], axis=1)
        sts.append(jnp.dot(qb, w, preferred_element_type=F32))
    ps = []
    for jj in range(bps):
        j = s * bps + jj
        gs_sc[j] = sts[jj][:, MOBA_BLOCK:MOBA_BLOCK + 1]
        bias = jnp.where(j == nbs - 1, blast_ref[...], cfar_ref[...])
        st = sts[jj][:, :MOBA_BLOCK] + bias
        m = jnp.max(st, axis=1, keepdims=True)
        p = jnp.exp(st - m)
        m_sc[j] = m
        l_sc[j] = jnp.sum(p, axis=1, keepdims=True)
        ps.append(p.astype(BF16))
    for jj in range(bps):
        o = None
        for u in range(ppb):
            vt = v_refs[ppb * jj + u][...].reshape(ATT_WIDTH, PAGE_SIZE).astype(BF16)
            part = _nt_dot(ps[jj][:, PAGE_SIZE * u:PAGE_SIZE * (u + 1)], vt)
            o = part if o is None else o + part
        o_sc[s * bps + jj] = o

    @pl.when(s == nsteps - 1)
    def _():
        qf = qb.astype(F32)
        gs = gs_sc[...]
        sel = _top_mask(gs, jnp.ones(gs.shape, jnp.bool_), MOBA_TOPK, nbs)
        knew = knew_ref[0].astype(BF16).astype(F32)
        vnew = vnew_ref[0].astype(BF16).astype(F32)
        s_own = [jnp.sum(qf * knew[t:t + 1, :], axis=1, keepdims=True) + bown_ref[:, t:t + 1]
                 for t in range(ts)]
        m_all = m_sc[...]
        mtot = jnp.max(jnp.where(sel, m_all, NEG_INF), axis=0)
        for t in range(ts):
            mtot = jnp.maximum(mtot, s_own[t])
        w = jnp.where(sel, jnp.exp(m_all - mtot[None]), 0.0)
        ltot = jnp.sum(w * l_sc[...], axis=0)
        gs_sc[...] = w

        def merge(j, acc):
            return acc + gs_sc[j] * o_sc[j]

        otot = lax.fori_loop(0, nbs, merge, jnp.zeros((nr, ATT_WIDTH), F32))
        for t in range(ts):
            pt = jnp.exp(s_own[t] - mtot)
            ltot = ltot + pt
            otot = otot + pt.astype(BF16).astype(F32) * vnew[t:t + 1, :]
        out = otot / ltot
        row = lax.broadcasted_iota(jnp.int32, out.shape, 0)
        lane = lax.broadcasted_iota(jnp.int32, out.shape, 1)
        out = jnp.where((lane // HEAD_DIM) == (row % N_HEADS), out, 0.0)
        pieces = [jnp.sum(out[N_HEADS * t:N_HEADS * (t + 1), :], axis=0, keepdims=True) for t in range(ts)]
        pieces.append(jnp.zeros((8 - ts, ATT_WIDTH), F32))
        o_ref[0] = jnp.concatenate(pieces, axis=0)


def _moba_sample(page_table, qbd, knew8, vnew8, cfar_rows, b_last, b_own, cache_kt, cache_vt, layer):
    db, nr, _ = qbd.shape
    ts = nr // N_HEADS
    n_pages = page_table.shape[1]
    nbs = n_pages // PAGES_PER_BLOCK
    npg = SAMPLE_PAGES_PER_STEP
    nsteps = n_pages // npg

    def page_spec(u):
        return pl.BlockSpec((None, None, N_HEADS, HEAD_DIM, PAGE_SIZE),
                            lambda b, s, pt: (pt[b, s * npg + u], layer, 0, 0, 0))

    full2 = lambda shp: pl.BlockSpec(shp, lambda b, s, pt: (0, 0))
    per_b = lambda r: pl.BlockSpec((1, r, ATT_WIDTH), lambda b, s, pt: (b, 0, 0))
    grid_spec = pltpu.PrefetchScalarGridSpec(
        num_scalar_prefetch=1,
        grid=(db, nsteps),
        in_specs=[per_b(nr), per_b(8), per_b(8),
                  full2((nr, 1)), full2((nr, MOBA_BLOCK)), full2((nr, 8))]
                 + [page_spec(u) for u in range(npg)] + [page_spec(u) for u in range(npg)],
        out_specs=per_b(8),
        scratch_shapes=[pltpu.VMEM((nbs, nr, 1), F32), pltpu.VMEM((nbs, nr, 1), F32),
                        pltpu.VMEM((nbs, nr, 1), F32), pltpu.VMEM((nbs, nr, ATT_WIDTH), F32)],
    )
    return pl.pallas_call(
        functools.partial(_moba_sample_body, nbs=nbs, ts=ts),
        grid_spec=grid_spec,
        out_shape=jax.ShapeDtypeStruct((db, 8, ATT_WIDTH), F32),
        compiler_params=_cparams(("arbitrary", "arbitrary")),
        name="moba_sample",
    )(page_table, qbd, knew8, vnew8, cfar_rows, b_last, b_own,
      *([cache_kt] * npg), *([cache_vt] * npg))


def _layer_norm(h, g, b):
    mu = jnp.mean(h, axis=-1, keepdims=True)
    c = h - mu
    var = jnp.mean(c * c, axis=-1, keepdims=True)
    return c * lax.rsqrt(var + LN_EPS) * g + b


def _argmax_first(vals):
    best, idx = vals[0], jnp.zeros(vals[0].shape, jnp.int32)
    for k in range(1, len(vals)):
        upd = vals[k] > best
        idx = jnp.where(upd, k, idx)
        best = jnp.where(upd, vals[k], best)
    return best, idx


def _route_rows(logit_rows):
    mx = functools.reduce(jnp.maximum, logit_rows)
    ex = [jnp.exp(r - mx) for r in logit_rows]
    tot = functools.reduce(lambda a, b: a + b, ex)
    probs = [e / tot for e in ex]
    scores = []
    for g in range(N_GROUPS):
        a, b, c, d = probs[EXPERTS_PER_GROUP * g:EXPERTS_PER_GROUP * (g + 1)]
        s1, t1 = jnp.maximum(a, b), jnp.minimum(a, b)
        s2, t2 = jnp.maximum(c, d), jnp.minimum(c, d)
        scores.append(jnp.maximum(s1, s2) + jnp.maximum(jnp.minimum(s1, s2), jnp.maximum(t1, t2)))
    _, gi = _argmax_first(scores)
    ing = []
    for j in range(EXPERTS_PER_GROUP):
        v = probs[j]
        for g in range(1, N_GROUPS):
            v = jnp.where(gi == g, probs[EXPERTS_PER_GROUP * g + j], v)
        ing.append(v)
    w1, i1 = _argmax_first(ing)
    w2, i2 = _argmax_first([jnp.where(i1 == j, -1.0, ing[j]) for j in range(EXPERTS_PER_GROUP)])
    den = w1 + w2
    e1 = (gi * EXPERTS_PER_GROUP + i1).astype(F32)
    e2 = (gi * EXPERTS_PER_GROUP + i2).astype(F32)
    return e1, e2, w1 / den, w2 / den


def _merge_body(x_ref, yp_ref, o_ref, gp_ref, ga_ref, wpo_ref, wao_ref, wo_ref, g_ref, b_ref,
                wr_ref, br_ref, x1_ref, rt_ref, *, tm):
    a = jnp.dot(yp_ref[...].astype(BF16), wpo_ref[...], preferred_element_type=F32)
    bb = jnp.dot(o_ref[...].astype(BF16), wao_ref[...], preferred_element_type=F32)
    merged = gp_ref[...] * a + ga_ref[...] * bb
    mix = jnp.dot(merged.astype(BF16), wo_ref[...], preferred_element_type=F32)
    x1 = _layer_norm(DEEPNORM_ALPHA * x_ref[...] + mix, g_ref[...], b_ref[...])
    x1_ref[...] = x1
    lg = jnp.dot(x1.astype(BF16), wr_ref[...], preferred_element_type=F32) + br_ref[...]
    lgt = lg.T
    e1, e2, g1, g2 = _route_rows([lgt[e:e + 1, :] for e in range(N_EXPERTS)])
    r = lax.broadcasted_iota(jnp.int32, (8, tm), 0)
    rt_ref[0] = jnp.where(r == 0, e1, jnp.where(r == 1, e2, jnp.where(r == 2, g1, jnp.where(r == 3, g2, 0.0))))


def _merge(x, yp, o, gp, ga, wpo, wao, wo, g, b, wr, br, *, tm):
    m = x.shape[0]
    row = lambda n: pl.BlockSpec((tm, n), lambda i: (i, 0))
    full = lambda r, c: pl.BlockSpec((r, c), lambda i: (0, 0))
    return pl.pallas_call(
        functools.partial(_merge_body, tm=tm),
        grid=(m // tm,),
        in_specs=[row(D_MODEL), row(POOL_WIDTH), row(ATT_WIDTH), row(D_MODEL), row(D_MODEL),
                  full(POOL_WIDTH, D_MODEL), full(ATT_WIDTH, D_MODEL), full(D_MODEL, D_MODEL),
                  full(1, D_MODEL), full(1, D_MODEL), full(D_MODEL, ROUTER_PAD), full(1, ROUTER_PAD)],
        out_specs=[row(D_MODEL), pl.BlockSpec((1, 8, tm), lambda i: (i, 0, 0))],
        out_shape=[jax.ShapeDtypeStruct((m, D_MODEL), F32), jax.ShapeDtypeStruct((m // tm, 8, tm), F32)],
        compiler_params=_cparams(("arbitrary",)),
        name="merge",
    )(x, yp, o, gp, ga, wpo, wao, wo, g, b, wr, br)


def _unpack_route(rt):
    n = rt.shape[0] * rt.shape[2]
    cols = rt[:, :4, :].transpose(0, 2, 1).reshape(n, 4)
    return cols[:, :2].astype(jnp.int32), cols[:, 2:]


def _experts_body(be_ref, xs_ref, wg_ref, wu_ref, wd_ref, ys_ref):
    xb = xs_ref[...].astype(BF16)
    h1 = jnp.dot(xb, wg_ref[0], preferred_element_type=F32)
    h2 = jnp.dot(xb, wu_ref[0], preferred_element_type=F32)
    act = (h1 * jax.nn.sigmoid(h1) * h2).astype(BF16)
    ys_ref[...] = jnp.dot(act, wd_ref[0], preferred_element_type=F32)


def _experts(blk_expert, xs, wg, wu, wd):
    n_rows = xs.shape[0]
    n_blk = n_rows // EXPERT_ROWS
    grid_spec = pltpu.PrefetchScalarGridSpec(
        num_scalar_prefetch=1,
        grid=(n_blk,),
        in_specs=[pl.BlockSpec((EXPERT_ROWS, D_MODEL), lambda i, be: (i, 0)),
                  pl.BlockSpec((1, D_MODEL, D_EXPERT), lambda i, be: (be[i], 0, 0)),
                  pl.BlockSpec((1, D_MODEL, D_EXPERT), lambda i, be: (be[i], 0, 0)),
                  pl.BlockSpec((1, D_EXPERT, D_MODEL), lambda i, be: (be[i], 0, 0))],
        out_specs=pl.BlockSpec((EXPERT_ROWS, D_MODEL), lambda i, be: (i, 0)),
    )
    return pl.pallas_call(
        _experts_body,
        grid_spec=grid_spec,
        out_shape=jax.ShapeDtypeStruct((n_rows, D_MODEL), F32),
        compiler_params=_cparams(("arbitrary",)),
        name="experts",
    )(blk_expert, xs, wg, wu, wd)


def _ln2_body(x_ref, f_ref, g_ref, b_ref, o_ref):
    o_ref[...] = _layer_norm(DEEPNORM_ALPHA * x_ref[...] + f_ref[...], g_ref[...], b_ref[...])


def _ln2(x, f, g, b, *, tm):
    m = x.shape[0]
    row = pl.BlockSpec((tm, D_MODEL), lambda i: (i, 0))
    vec = pl.BlockSpec((1, D_MODEL), lambda i: (0, 0))
    return pl.pallas_call(
        _ln2_body,
        grid=(m // tm,),
        in_specs=[row, row, vec, vec],
        out_specs=row,
        out_shape=jax.ShapeDtypeStruct((m, D_MODEL), F32),
        compiler_params=_cparams(("arbitrary",)),
        name="ln2",
    )(x, f, g, b)


def _moe(h, expert, gate, wg, wu, wd):
    n = h.shape[0]
    s = n * TOP_K
    e_flat = expert.reshape(-1)
    onehot = (e_flat[:, None] == jnp.arange(N_EXPERTS, dtype=jnp.int32)[None, :]).astype(jnp.int32)
    csum = jnp.cumsum(onehot, axis=0)
    counts = csum[-1]
    pos_in = jnp.take_along_axis(csum, e_flat[:, None], axis=1)[:, 0] - 1
    padded = (counts + EXPERT_ROWS - 1) // EXPERT_ROWS * EXPERT_ROWS
    pad_end = jnp.cumsum(padded)
    pad_start = pad_end - padded
    dest = pad_start[e_flat] + pos_in
    n_blk = (s + N_EXPERTS * (EXPERT_ROWS - 1) + EXPERT_ROWS - 1) // EXPERT_ROWS
    tok = jnp.arange(s, dtype=jnp.int32) // TOP_K
    row_tok = jnp.zeros((n_blk * EXPERT_ROWS,), jnp.int32).at[dest].set(tok)
    xs = h[row_tok]
    blk_expert = jnp.minimum(
        jnp.searchsorted(pad_end, jnp.arange(n_blk, dtype=jnp.int32) * EXPERT_ROWS, side="right"),
        N_EXPERTS - 1).astype(jnp.int32)
    ys = _experts(blk_expert, xs, wg, wu, wd)
    y_slot = ys[dest] * gate.reshape(-1)[:, None]
    return y_slot.reshape(n, TOP_K, D_MODEL).sum(axis=1)


def kernel(x_prompt, x_sample, cache_k, cache_v, state_pool, page_table, rel_bias, w_router, b_router,
           w_in, pool_w, pool_scale, w_pool_out, w_attn_out, w_o, ln1_g, ln1_b, w_gate, w_up, w_down,
           ln2_g, ln2_b):
    bsz, seq, _ = x_prompt.shape
    db, ts, _ = x_sample.shape
    past_len = page_table.shape[1] * PAGE_SIZE
    nb = seq // MOBA_BLOCK
    nbs = past_len // MOBA_BLOCK
    assert seq % MOBA_BLOCK == 0 and past_len % MOBA_BLOCK == 0 and ts <= 8
    np_rows = bsz * seq
    ns_rows = db * ts

    far_p = _far_bucket(MOBA_BLOCK + 1, seq)
    far_s = _far_bucket(MOBA_BLOCK + 1, past_len + ts)
    c = np.arange(MOBA_BLOCK)
    d_own = c[None, :] - c[:, None]
    tab_own = rel_bias[_bucket_np(np.maximum(d_own, 0))]
    b_own_p = jnp.where((d_own >= 0)[None], tab_own.transpose(2, 0, 1), NEG_INF)
    b_prev_p = rel_bias[_bucket_np(d_own + MOBA_BLOCK)].transpose(2, 0, 1)
    cfar_p = rel_bias[far_p]
    t_idx = np.repeat(np.arange(ts), N_HEADS)
    h_idx = np.tile(np.arange(N_HEADS), ts)
    d_last = MOBA_BLOCK + t_idx[:, None] - c[None, :]
    b_last_s = rel_bias[_bucket_np(d_last), h_idx[:, None]]
    tn = np.arange(8)
    d_new = t_idx[:, None] - tn[None, :]
    b_own_s = jnp.where((d_new >= 0) & (tn[None, :] < ts),
                        rel_bias[_bucket_np(np.maximum(d_new, 0)), h_idx[:, None]], NEG_INF)
    cfar_s = rel_bias[far_s][h_idx][:, None]
    head_mask = jnp.asarray((np.arange(ATT_WIDTH)[None, :] // HEAD_DIM) == h_idx[:, None])
    cache_kt = jnp.transpose(cache_k, (0, 2, 3, 4, 1))
    cache_vt = jnp.transpose(cache_v, (0, 2, 3, 4, 1))

    wr_pad = jnp.zeros((D_MODEL, ROUTER_PAD), BF16).at[:, :N_EXPERTS].set(w_router.astype(BF16))
    br_pad = jnp.zeros((1, ROUTER_PAD), F32).at[0, :N_EXPERTS].set(b_router)

    xp = x_prompt.reshape(np_rows, D_MODEL)
    xs = x_sample.reshape(ns_rows, D_MODEL)
    kp, vp, pp, ksm, vsm, psm = [], [], [], [], [], []
    for l in range(DEPTH):
        w_in_bf = w_in[l].astype(BF16)
        pool_w_bf = pool_w[l].astype(BF16)
        scale = pool_scale[l][None, :]
        wpo = w_pool_out[l].astype(BF16)
        wao = w_attn_out[l].astype(BF16)
        wo = w_o[l].astype(BF16)
        wg = w_gate[l].astype(BF16)
        wu = w_up[l].astype(BF16)
        wd = w_down[l].astype(BF16)
        g1, b1 = ln1_g[l][None, :], ln1_b[l][None, :]
        g2, b2 = ln2_g[l][None, :], ln2_b[l][None, :]

        p_p, q_p, k_p, v_p, gp_p, ga_p, kb_p, vt_p, km_p = _inproj(xp, w_in_bf, tm=MOBA_BLOCK, attn_layouts=True)
        prev0 = jnp.zeros((bsz, POOL_HALO, POOL_WIDTH), F32)
        yp_p = _pool(p_p.reshape(bsz, seq, POOL_WIDTH), prev0, pool_w_bf, scale, tq=MOBA_BLOCK, pos0=0)
        o_p = _moba_prompt(q_p.reshape(bsz, seq, ATT_WIDTH),
                           kb_p.reshape(bsz, nb, MOBA_BLOCK, ATT_WIDTH),
                           vt_p.reshape(bsz, nb, ATT_WIDTH, MOBA_BLOCK),
                           km_p.reshape(bsz, nb, ATT_WIDTH), cfar_p, b_own_p, b_prev_p)
        x1_p, rt_p = _merge(xp, yp_p.reshape(np_rows, POOL_WIDTH), o_p.reshape(np_rows, ATT_WIDTH),
                            gp_p, ga_p, wpo, wao, wo, g1, b1, wr_pad, br_pad, tm=MOBA_BLOCK)

        p_s, q_s, k_s, v_s, gp_s, ga_s = _inproj(xs, w_in_bf, tm=ns_rows, attn_layouts=False)
        state = state_pool[:, l]
        prev_s = jnp.concatenate([jnp.zeros((db, 1, POOL_WIDTH), F32), state], axis=1)
        p_s3 = p_s.reshape(db, ts, POOL_WIDTH)
        p_s8 = jnp.pad(p_s3, ((0, 0), (0, 8 - ts), (0, 0)))
        yp_s = _pool(p_s8, prev_s, pool_w_bf, scale, tq=8, pos0=past_len)[:, :ts]
        qbd = jnp.where(head_mask[None], jnp.repeat(q_s.reshape(db, ts, ATT_WIDTH), N_HEADS, axis=1), 0.0)
        knew8 = jnp.pad(k_s.reshape(db, ts, ATT_WIDTH), ((0, 0), (0, 8 - ts), (0, 0)))
        vnew8 = jnp.pad(v_s.reshape(db, ts, ATT_WIDTH), ((0, 0), (0, 8 - ts), (0, 0)))
        o_s = _moba_sample(page_table, qbd, knew8, vnew8, cfar_s, b_last_s, b_own_s,
                           cache_kt, cache_vt, l)[:, :ts]
        x1_s, rt_s = _merge(xs, yp_s.reshape(ns_rows, POOL_WIDTH), o_s.reshape(ns_rows, ATT_WIDTH),
                            gp_s, ga_s, wpo, wao, wo, g1, b1, wr_pad, br_pad, tm=ns_rows)

        x1 = jnp.concatenate([x1_p, x1_s], axis=0)
        e_p, gt_p = _unpack_route(rt_p)
        e_s, gt_s = _unpack_route(rt_s)
        f = _moe(x1, jnp.concatenate([e_p, e_s], axis=0), jnp.concatenate([gt_p, gt_s], axis=0), wg, wu, wd)
        x2 = _ln2(x1, f, g2, b2, tm=EXPERT_ROWS)
        xp, xs = x2[:np_rows], x2[np_rows:]

        kp.append(k_p.reshape(bsz, seq, N_HEADS, HEAD_DIM))
        vp.append(v_p.reshape(bsz, seq, N_HEADS, HEAD_DIM))
        pp.append(p_p.reshape(bsz, seq, POOL_WIDTH)[:, seq - POOL_BUF:])
        ksm.append(k_s.reshape(db, ts, N_HEADS, HEAD_DIM))
        vsm.append(v_s.reshape(db, ts, N_HEADS, HEAD_DIM))
        psm.append(jnp.concatenate([state, p_s3], axis=1)[:, -POOL_BUF:])

    return (xp.reshape(bsz, seq, D_MODEL), xs.reshape(db, ts, D_MODEL),
            jnp.stack(kp, axis=2), jnp.stack(vp, axis=2), jnp.stack(pp, axis=1),
            jnp.stack(ksm, axis=2), jnp.stack(vsm, axis=2), jnp.stack(psm, axis=1))
```

```python
import functools
import math

import numpy as np
import jax
import jax.numpy as jnp
from jax import lax
from jax.experimental import pallas as pl
from jax.experimental.pallas import tpu as pltpu

F32 = jnp.float32
BF16 = jnp.bfloat16
NEG_INF = float("-inf")

D_MODEL = 1024
DEPTH = 2
N_HEADS = 8
HEAD_DIM = 64
ATT_WIDTH = N_HEADS * HEAD_DIM
MOBA_BLOCK = 256
MOBA_TOPK = 3
N_BUCKETS = 32
REL_MAX_DIST = 128
POOL_WINDOWS = (2, 4, 8, 16)
POOL_WIDTH = 512
POOL_GW = 128
POOL_BUF = 15
POOL_HALO = 16
IN_WIDTH = POOL_WIDTH + 3 * ATT_WIDTH + 2 * D_MODEL
N_EXPERTS = 16
N_GROUPS = 4
EXPERTS_PER_GROUP = 4
TOP_K = 2
D_EXPERT = 512
EXPERT_ROWS = 128
DEEPNORM_ALPHA = (2 * DEPTH) ** 0.25
LN_EPS = 1e-5
PAGE_SIZE = 128
ROUTER_PAD = 128
LANES = 128

VMEM_LIMIT = 52 * 1024 * 1024


def _cparams(sem):
    return pltpu.CompilerParams(dimension_semantics=sem, vmem_limit_bytes=VMEM_LIMIT)


def _nt_dot(a, b):
    return lax.dot_general(a, b, (((1,), (1,)), ((), ())), preferred_element_type=F32)


def _top_mask(scores, cand, k, n):
    idx = lax.broadcasted_iota(jnp.int32, scores.shape, 0)
    sel = jnp.zeros(scores.shape, jnp.bool_)
    for _ in range(k):
        c = cand & jnp.logical_not(sel)
        cur = jnp.where(c, scores, NEG_INF)
        mx = jnp.max(cur, axis=0, keepdims=True)
        first = jnp.min(jnp.where(c & (cur == mx), idx, n), axis=0, keepdims=True)
        sel = sel | (idx == first)
    return sel


def _inproj_body(x_ref, w_ref, p_ref, q_ref, k_ref, v_ref, gp_ref, ga_ref, *attn_refs, tm):
    xb = x_ref[...].astype(BF16)

    def seg(a, b):
        return jnp.dot(xb, w_ref[:, a:b], preferred_element_type=F32)

    c0 = POOL_WIDTH
    p_ref[...] = seg(0, c0)
    q_ref[...] = seg(c0, c0 + ATT_WIDTH) * (HEAD_DIM ** -0.5)
    k = seg(c0 + ATT_WIDTH, c0 + 2 * ATT_WIDTH)
    k_ref[...] = k
    v = seg(c0 + 2 * ATT_WIDTH, c0 + 3 * ATT_WIDTH)
    v_ref[...] = v
    g0 = c0 + 3 * ATT_WIDTH
    gp_ref[...] = jax.nn.sigmoid(seg(g0, g0 + D_MODEL))
    ga_ref[...] = jax.nn.sigmoid(seg(g0 + D_MODEL, g0 + 2 * D_MODEL))
    if attn_refs:
        kb_ref, vt_ref, km_ref = attn_refs
        kb_ref[...] = k.astype(BF16)
        vt_ref[0] = v.T.astype(BF16)
        km_ref[0] = jnp.sum(k, axis=0, keepdims=True) * (1.0 / tm)


def _inproj(x, w_bf, *, tm, attn_layouts):
    m = x.shape[0]
    nt = m // tm
    row = lambda n: pl.BlockSpec((tm, n), lambda i: (i, 0))
    out_shape = [jax.ShapeDtypeStruct((m, POOL_WIDTH), F32),
                 jax.ShapeDtypeStruct((m, ATT_WIDTH), F32),
                 jax.ShapeDtypeStruct((m, ATT_WIDTH), F32),
                 jax.ShapeDtypeStruct((m, ATT_WIDTH), F32),
                 jax.ShapeDtypeStruct((m, D_MODEL), F32),
                 jax.ShapeDtypeStruct((m, D_MODEL), F32)]
    out_specs = [row(POOL_WIDTH), row(ATT_WIDTH), row(ATT_WIDTH), row(ATT_WIDTH),
                 row(D_MODEL), row(D_MODEL)]
    if attn_layouts:
        assert tm == MOBA_BLOCK
        out_shape += [jax.ShapeDtypeStruct((m, ATT_WIDTH), BF16),
                      jax.ShapeDtypeStruct((nt, ATT_WIDTH, tm), BF16),
                      jax.ShapeDtypeStruct((nt, 1, ATT_WIDTH), F32)]
        out_specs += [row(ATT_WIDTH),
                      pl.BlockSpec((1, ATT_WIDTH, tm), lambda i: (i, 0, 0)),
                      pl.BlockSpec((1, 1, ATT_WIDTH), lambda i: (i, 0, 0))]
    return pl.pallas_call(
        functools.partial(_inproj_body, tm=tm),
        grid=(nt,),
        in_specs=[row(D_MODEL), pl.BlockSpec((D_MODEL, IN_WIDTH), lambda i: (0, 0))],
        out_specs=out_specs,
        out_shape=out_shape,
        compiler_params=_cparams(("arbitrary",)),
        name="inproj",
    )(x, w_bf)


def _pool_body(prev_ref, halo_ref, p_ref, pw_ref, sc_ref, y_ref, *, tq, pos0):
    i = pl.program_id(1)
    p = p_ref[0]
    halo = jnp.where(i == 0, prev_ref[0], halo_ref[0])
    z = jnp.concatenate([halo, p], axis=0)
    lane = lax.broadcasted_iota(jnp.int32, z.shape, 1)
    x = z
    for s in (8, 4, 2, 1):
        thr = POOL_WIDTH - POOL_GW * {8: 1, 4: 2, 2: 3, 1: 4}[s]
        sh = pltpu.roll(x, s, 0)
        x = x + (jnp.where(lane >= thr, sh, 0.0) if thr > 0 else sh)
    wsum = x[POOL_HALO:, :]
    row = lax.broadcasted_iota(jnp.int32, (tq, POOL_WIDTH), 0)
    lane2 = lax.broadcasted_iota(jnp.int32, (tq, POOL_WIDTH), 1)
    wl = jnp.where(lane2 < POOL_GW, POOL_WINDOWS[0],
                   jnp.where(lane2 < 2 * POOL_GW, POOL_WINDOWS[1],
                             jnp.where(lane2 < 3 * POOL_GW, POOL_WINDOWS[2], POOL_WINDOWS[3])))
    pos = pos0 + i * tq + row
    cnt = jnp.minimum(pos + 1, wl).astype(F32)
    d = (wsum / cnt - p).astype(BF16)
    ys = [jnp.dot(d[:, g * POOL_GW:(g + 1) * POOL_GW], pw_ref[g], preferred_element_type=F32)
          for g in range(len(POOL_WINDOWS))]
    y_ref[0] = jnp.concatenate(ys, axis=1) * sc_ref[...]


def _pool(p, prev16, pool_w_bf, scale, *, tq, pos0):
    b, t, c = p.shape
    nq = t // tq
    hb = tq // POOL_HALO
    return pl.pallas_call(
        functools.partial(_pool_body, tq=tq, pos0=pos0),
        grid=(b, nq),
        in_specs=[pl.BlockSpec((1, POOL_HALO, c), lambda bi, i: (bi, 0, 0)),
                  pl.BlockSpec((1, POOL_HALO, c), lambda bi, i: (bi, jnp.maximum(i * hb - 1, 0), 0)),
                  pl.BlockSpec((1, tq, c), lambda bi, i: (bi, i, 0)),
                  pl.BlockSpec((len(POOL_WINDOWS), POOL_GW, POOL_GW), lambda bi, i: (0, 0, 0)),
                  pl.BlockSpec((1, c), lambda bi, i: (0, 0))],
        out_specs=pl.BlockSpec((1, tq, c), lambda bi, i: (bi, i, 0)),
        out_shape=jax.ShapeDtypeStruct((b, t, c), F32),
        compiler_params=_cparams(("arbitrary", "arbitrary")),
        name="pool",
    )(prev16, p, p, pool_w_bf, scale)


def _bucket_np(dist):
    n = np.maximum(dist, 0)
    max_exact = N_BUCKETS // 2
    nf = np.maximum(n, 1).astype(np.float32)
    large = max_exact + (np.log(nf / np.float32(max_exact)) / np.float32(math.log(REL_MAX_DIST / max_exact))
                         * np.float32(N_BUCKETS - max_exact)).astype(np.int32)
    large = np.minimum(large, N_BUCKETS - 1)
    return np.where(n < max_exact, n, large)


def _far_bucket(min_dist, max_dist):
    b = _bucket_np(np.arange(min_dist, max_dist + 1))
    assert (b == b[0]).all()
    return int(b[0])


def _moba_prompt_body(rb_ref, q_ref, kb_ref, vt_ref, km_ref, io_ref, ip_ref, o_ref,
                      sel_ref, qz_ref, m_ref, l_ref, acc_ref, bown_ref, bprev_ref, *, nb, far):
    i = pl.program_id(1)
    tq = MOBA_BLOCK

    @pl.when((pl.program_id(0) == 0) & (i == 0))
    def _():
        for h in range(N_HEADS):
            bown_ref[h] = jnp.full((tq, tq), NEG_INF, F32)
            bprev_ref[h] = jnp.zeros((tq, tq), F32)
        io = io_ref[...]
        ip = ip_ref[...]

        def fill(bkt, carry):
            mo = io == bkt
            mp = ip == bkt
            for h in range(N_HEADS):
                val = rb_ref[bkt, h] - rb_ref[far, h]
                bown_ref[h] = jnp.where(mo, val, bown_ref[h])
                bprev_ref[h] = jnp.where(mp, val, bprev_ref[h])
            return carry

        lax.fori_loop(0, N_BUCKETS, fill, 0)

    blk = lax.broadcasted_iota(jnp.int32, (nb, tq), 0)
    valid = blk < i
    lane = lax.broadcasted_iota(jnp.int32, (tq, LANES), 1)
    for hp in range(N_HEADS // 2):
        qf = q_ref[0, :, LANES * hp:LANES * (hp + 1)]
        km = km_ref[0, :, LANES * hp:LANES * (hp + 1)].astype(BF16)
        for hh in range(2):
            h = 2 * hp + hh
            qz = jnp.where((lane >= HEAD_DIM * hh) & (lane < HEAD_DIM * (hh + 1)), qf, 0.0).astype(BF16)
            qz_ref[h] = qz
            sel = _top_mask(_nt_dot(km, qz), valid, MOBA_TOPK, nb)
            selb = jnp.where(sel, 0.0, NEG_INF)
            for jj in range(nb):
                sel_ref[h, jj] = selb[jj:jj + 1, :]
            m_ref[h] = jnp.full((1, tq), -1e30, F32)
            l_ref[h] = jnp.zeros((1, tq), F32)
            acc_ref[h] = jnp.zeros((HEAD_DIM, tq), F32)

    def key_block(j, bias_ref, masked):
        sts = [_nt_dot(kb_ref[0, j, :, LANES * (h // 2):LANES * (h // 2 + 1)], qz_ref[h])
               for h in range(N_HEADS)]
        ps, alphas = [], []
        for h in range(N_HEADS):
            st = sts[h] if bias_ref is None else sts[h] + bias_ref[h]
            cm = jnp.max(st, axis=0, keepdims=True)
            m = m_ref[h]
            if masked:
                mb = sel_ref[h, j]
                m_new = jnp.maximum(m, cm + mb)
                p = jnp.exp(st + (mb - m_new))
            else:
                m_new = jnp.maximum(m, cm)
                p = jnp.exp(st - m_new)
            alpha = jnp.exp(m - m_new)
            l_ref[h] = alpha * l_ref[h] + jnp.sum(p, axis=0, keepdims=True)
            m_ref[h] = m_new
            ps.append(p.astype(BF16))
            alphas.append(alpha)
        for h in range(N_HEADS):
            vt = vt_ref[0, j, HEAD_DIM * h:HEAD_DIM * (h + 1), :]
            acc_ref[h] = alphas[h] * acc_ref[h] + jnp.dot(vt, ps[h], preferred_element_type=F32)

    def far_body(j, carry):
        key_block(j, None, True)
        return carry

    lax.fori_loop(0, i - 1, far_body, 0)
    jp = jnp.maximum(i - 1, 0)
    key_block(jp, bprev_ref, True)
    key_block(i, bown_ref, False)
    for hp in range(N_HEADS // 2):
        pair = jnp.concatenate([acc_ref[2 * hp] / l_ref[2 * hp], acc_ref[2 * hp + 1] / l_ref[2 * hp + 1]],
                               axis=0)
        o_ref[0, :, LANES * hp:LANES * (hp + 1)] = pair.T


def _moba_prompt(q, kb, vt, km, rel_bias):
    b, t, _ = q.shape
    nb = t // MOBA_BLOCK
    far = _far_bucket(MOBA_BLOCK + 1, t)
    c = np.arange(MOBA_BLOCK)
    d_own = c[None, :] - c[:, None]
    idx_own = jnp.asarray(np.where(d_own >= 0, _bucket_np(d_own), -1).astype(np.int32))
    idx_prev = jnp.asarray(_bucket_np(d_own + MOBA_BLOCK).astype(np.int32))
    once = pl.Buffered(1)
    tile = pl.BlockSpec((MOBA_BLOCK, MOBA_BLOCK), lambda bi, i: (0, 0), pipeline_mode=once)
    return pl.pallas_call(
        functools.partial(_moba_prompt_body, nb=nb, far=far),
        grid=(b, nb),
        in_specs=[pl.BlockSpec(memory_space=pltpu.SMEM),
                  pl.BlockSpec((1, MOBA_BLOCK, ATT_WIDTH), lambda bi, i: (bi, i, 0)),
                  pl.BlockSpec((1, nb, MOBA_BLOCK, ATT_WIDTH), lambda bi, i: (bi, 0, 0, 0), pipeline_mode=once),
                  pl.BlockSpec((1, nb, ATT_WIDTH, MOBA_BLOCK), lambda bi, i: (bi, 0, 0, 0), pipeline_mode=once),
                  pl.BlockSpec((1, nb, ATT_WIDTH), lambda bi, i: (bi, 0, 0)),
                  tile, tile],
        out_specs=pl.BlockSpec((1, MOBA_BLOCK, ATT_WIDTH), lambda bi, i: (bi, i, 0)),
        out_shape=jax.ShapeDtypeStruct((b, t, ATT_WIDTH), F32),
        scratch_shapes=[pltpu.VMEM((N_HEADS, nb, 1, MOBA_BLOCK), F32),
                        pltpu.VMEM((N_HEADS, MOBA_BLOCK, LANES), BF16),
                        pltpu.VMEM((N_HEADS, 1, MOBA_BLOCK), F32),
                        pltpu.VMEM((N_HEADS, 1, MOBA_BLOCK), F32),
                        pltpu.VMEM((N_HEADS, HEAD_DIM, MOBA_BLOCK), F32),
                        pltpu.VMEM((N_HEADS, MOBA_BLOCK, MOBA_BLOCK), F32),
                        pltpu.VMEM((N_HEADS, MOBA_BLOCK, MOBA_BLOCK), F32)],
        compiler_params=_cparams(("arbitrary", "arbitrary")),
        name="moba_prompt",
    )(rel_bias, q, kb, vt, km, idx_own, idx_prev)


SAMPLE_PAGES_PER_STEP = 16
PAGES_PER_BLOCK = MOBA_BLOCK // PAGE_SIZE


def _moba_sample_body(pt_ref, qbd_ref, knew_ref, vnew_ref, cfar_ref, blast_ref, bown_ref, *rest,
                      nbs, ts):
    npg = SAMPLE_PAGES_PER_STEP
    ppb = PAGES_PER_BLOCK
    k_refs = rest[:npg]
    v_refs = rest[npg:2 * npg]
    o_ref = rest[2 * npg]
    gs_sc, m_sc, l_sc, o_sc = rest[2 * npg + 1:]
    s = pl.program_id(1)
    nsteps = pl.num_programs(1)
    bps = npg // ppb
    nr = ts * N_HEADS
    qb = qbd_ref[0].astype(BF16)
    lane0 = lax.broadcasted_iota(jnp.int32, (ATT_WIDTH, LANES), 1) == 0
    sts = []
    for jj in range(bps):
        kts = [k_refs[ppb * jj + u][...].reshape(ATT_WIDTH, PAGE_SIZE) for u in range(ppb)]
        ksum = functools.reduce(lambda a, b: a + b, [jnp.sum(kt, axis=1, keepdims=True) for kt in kts])
        kmean = ksum * (1.0 / MOBA_BLOCK)
        kmcol = jnp.where(lane0, kmean, 0.0).astype(BF16)
        w = jnp.concatenate([kt.astype(BF16) for kt in kts] + [kmcol], axis=1)
        sts.append(jnp.dot(qb, w, preferred_element_type=F32))
    ps = []
    for jj in range(bps):
        j = s * bps + jj
        gs_sc[j] = sts[jj][:, MOBA_BLOCK:MOBA_BLOCK + 1]
        bias = jnp.where(j == nbs - 1, blast_ref[...], cfar_ref[...])
        st = sts[jj][:, :MOBA_BLOCK] + bias
        m = jnp.max(st, axis=1, keepdims=True)
        p = jnp.exp(st - m)
        m_sc[j] = m
        l_sc[j] = jnp.sum(p, axis=1, keepdims=True)
        ps.append(p.astype(BF16))
    for jj in range(bps):
        o = None
        for u in range(ppb):
            vt = v_refs[ppb * jj + u][...].reshape(ATT_WIDTH, PAGE_SIZE).astype(BF16)
            part = _nt_dot(ps[jj][:, PAGE_SIZE * u:PAGE_SIZE * (u + 1)], vt)
            o = part if o is None else o + part
        o_sc[s * bps + jj] = o

    @pl.when(s == nsteps - 1)
    def _():
        qf = qb.astype(F32)
        gs = gs_sc[...]
        sel = _top_mask(gs, jnp.ones(gs.shape, jnp.bool_), MOBA_TOPK, nbs)
        knew = knew_ref[0].astype(BF16).astype(F32)
        vnew = vnew_ref[0].astype(BF16).astype(F32)
        s_own = [jnp.sum(qf * knew[t:t + 1, :], axis=1, keepdims=True) + bown_ref[:, t:t + 1]
                 for t in range(ts)]
        m_all = m_sc[...]
        mtot = jnp.max(jnp.where(sel, m_all, NEG_INF), axis=0)
        for t in range(ts):
            mtot = jnp.maximum(mtot, s_own[t])
        w = jnp.where(sel, jnp.exp(m_all - mtot[None]), 0.0)
        ltot = jnp.sum(w * l_sc[...], axis=0)
        gs_sc[...] = w

        def merge(j, acc):
            return acc + gs_sc[j] * o_sc[j]

        otot = lax.fori_loop(0, nbs, merge, jnp.zeros((nr, ATT_WIDTH), F32))
        for t in range(ts):
            pt = jnp.exp(s_own[t] - mtot)
            ltot = ltot + pt
            otot = otot + pt.astype(BF16).astype(F32) * vnew[t:t + 1, :]
        out = otot / ltot
        row = lax.broadcasted_iota(jnp.int32, out.shape, 0)
        lane = lax.broadcasted_iota(jnp.int32, out.shape, 1)
        out = jnp.where((lane // HEAD_DIM) == (row % N_HEADS), out, 0.0)
        pieces = [jnp.sum(out[N_HEADS * t:N_HEADS * (t + 1), :], axis=0, keepdims=True) for t in range(ts)]
        pieces.append(jnp.zeros((8 - ts, ATT_WIDTH), F32))
        o_ref[0] = jnp.concatenate(pieces, axis=0)


def _moba_sample(page_table, qbd, knew8, vnew8, cfar_rows, b_last, b_own, cache_kt, cache_vt, layer):
    db, nr, _ = qbd.shape
    ts = nr // N_HEADS
    n_pages = page_table.shape[1]
    nbs = n_pages // PAGES_PER_BLOCK
    npg = SAMPLE_PAGES_PER_STEP
    nsteps = n_pages // npg

    def page_spec(u):
        return pl.BlockSpec((None, None, N_HEADS, HEAD_DIM, PAGE_SIZE),
                            lambda b, s, pt: (pt[b, s * npg + u], layer, 0, 0, 0))

    full2 = lambda shp: pl.BlockSpec(shp, lambda b, s, pt: (0, 0))
    per_b = lambda r: pl.BlockSpec((1, r, ATT_WIDTH), lambda b, s, pt: (b, 0, 0))
    grid_spec = pltpu.PrefetchScalarGridSpec(
        num_scalar_prefetch=1,
        grid=(db, nsteps),
        in_specs=[per_b(nr), per_b(8), per_b(8),
                  full2((nr, 1)), full2((nr, MOBA_BLOCK)), full2((nr, 8))]
                 + [page_spec(u) for u in range(npg)] + [page_spec(u) for u in range(npg)],
        out_specs=per_b(8),
        scratch_shapes=[pltpu.VMEM((nbs, nr, 1), F32), pltpu.VMEM((nbs, nr, 1), F32),
                        pltpu.VMEM((nbs, nr, 1), F32), pltpu.VMEM((nbs, nr, ATT_WIDTH), F32)],
    )
    return pl.pallas_call(
        functools.partial(_moba_sample_body, nbs=nbs, ts=ts),
        grid_spec=grid_spec,
        out_shape=jax.ShapeDtypeStruct((db, 8, ATT_WIDTH), F32),
        compiler_params=_cparams(("arbitrary", "arbitrary")),
        name="moba_sample",
    )(page_table, qbd, knew8, vnew8, cfar_rows, b_last, b_own,
      *([cache_kt] * npg), *([cache_vt] * npg))


def _layer_norm(h, g, b):
    mu = jnp.mean(h, axis=-1, keepdims=True)
    c = h - mu
    var = jnp.mean(c * c, axis=-1, keepdims=True)
    return c * lax.rsqrt(var + LN_EPS) * g + b


def _argmax_first(vals):
    best, idx = vals[0], jnp.zeros(vals[0].shape, jnp.int32)
    for k in range(1, len(vals)):
        upd = vals[k] > best
        idx = jnp.where(upd, k, idx)
        best = jnp.where(upd, vals[k], best)
    return best, idx


def _route_rows(logit_rows):
    mx = functools.reduce(jnp.maximum, logit_rows)
    ex = [jnp.exp(r - mx) for r in logit_rows]
    tot = functools.reduce(lambda a, b: a + b, ex)
    probs = [e / tot for e in ex]
    scores = []
    for g in range(N_GROUPS):
        a, b, c, d = probs[EXPERTS_PER_GROUP * g:EXPERTS_PER_GROUP * (g + 1)]
        s1, t1 = jnp.maximum(a, b), jnp.minimum(a, b)
        s2, t2 = jnp.maximum(c, d), jnp.minimum(c, d)
        scores.append(jnp.maximum(s1, s2) + jnp.maximum(jnp.minimum(s1, s2), jnp.maximum(t1, t2)))
    _, gi = _argmax_first(scores)
    ing = []
    for j in range(EXPERTS_PER_GROUP):
        v = probs[j]
        for g in range(1, N_GROUPS):
            v = jnp.where(gi == g, probs[EXPERTS_PER_GROUP * g + j], v)
        ing.append(v)
    w1, i1 = _argmax_first(ing)
    w2, i2 = _argmax_first([jnp.where(i1 == j, -1.0, ing[j]) for j in range(EXPERTS_PER_GROUP)])
    den = w1 + w2
    e1 = (gi * EXPERTS_PER_GROUP + i1).astype(F32)
    e2 = (gi * EXPERTS_PER_GROUP + i2).astype(F32)
    return e1, e2, w1 / den, w2 / den


def _merge_body(x_ref, yp_ref, o_ref, gp_ref, ga_ref, wpo_ref, wao_ref, wo_ref, g_ref, b_ref,
                wr_ref, br_ref, x1_ref, rt_ref, *, tm):
    a = jnp.dot(yp_ref[...].astype(BF16), wpo_ref[...], preferred_element_type=F32)
    bb = jnp.dot(o_ref[...].astype(BF16), wao_ref[...], preferred_element_type=F32)
    merged = gp_ref[...] * a + ga_ref[...] * bb
    mix = jnp.dot(merged.astype(BF16), wo_ref[...], preferred_element_type=F32)
    x1 = _layer_norm(DEEPNORM_ALPHA * x_ref[...] + mix, g_ref[...], b_ref[...])
    x1_ref[...] = x1
    lg = jnp.dot(x1.astype(BF16), wr_ref[...], preferred_element_type=F32) + br_ref[...]
    lgt = lg.T
    e1, e2, g1, g2 = _route_rows([lgt[e:e + 1, :] for e in range(N_EXPERTS)])
    r = lax.broadcasted_iota(jnp.int32, (8, tm), 0)
    rt_ref[0] = jnp.where(r == 0, e1, jnp.where(r == 1, e2, jnp.where(r == 2, g1, jnp.where(r == 3, g2, 0.0))))


def _merge(x, yp, o, gp, ga, wpo, wao, wo, g, b, wr, br, *, tm):
    m = x.shape[0]
    row = lambda n: pl.BlockSpec((tm, n), lambda i: (i, 0))
    full = lambda r, c: pl.BlockSpec((r, c), lambda i: (0, 0))
    return pl.pallas_call(
        functools.partial(_merge_body, tm=tm),
        grid=(m // tm,),
        in_specs=[row(D_MODEL), row(POOL_WIDTH), row(ATT_WIDTH), row(D_MODEL), row(D_MODEL),
                  full(POOL_WIDTH, D_MODEL), full(ATT_WIDTH, D_MODEL), full(D_MODEL, D_MODEL),
                  full(1, D_MODEL), full(1, D_MODEL), full(D_MODEL, ROUTER_PAD), full(1, ROUTER_PAD)],
        out_specs=[row(D_MODEL), pl.BlockSpec((1, 8, tm), lambda i: (i, 0, 0))],
        out_shape=[jax.ShapeDtypeStruct((m, D_MODEL), F32), jax.ShapeDtypeStruct((m // tm, 8, tm), F32)],
        compiler_params=_cparams(("arbitrary",)),
        name="merge",
    )(x, yp, o, gp, ga, wpo, wao, wo, g, b, wr, br)


def _unpack_route(rt):
    n = rt.shape[0] * rt.shape[2]
    cols = rt[:, :4, :].transpose(0, 2, 1).reshape(n, 4)
    return cols[:, :2].astype(jnp.int32), cols[:, 2:]


def _experts_body(be_ref, xs_ref, wg_ref, wu_ref, wd_ref, ys_ref, wg_sc, wu_sc, wd_sc):
    i = pl.program_id(0)

    @pl.when((i == 0) | (be_ref[i] != be_ref[jnp.maximum(i - 1, 0)]))
    def _():
        wg_sc[...] = wg_ref[0].astype(BF16)
        wu_sc[...] = wu_ref[0].astype(BF16)
        wd_sc[...] = wd_ref[0].astype(BF16)

    xb = xs_ref[...].astype(BF16)
    h1 = jnp.dot(xb, wg_sc[...], preferred_element_type=F32)
    h2 = jnp.dot(xb, wu_sc[...], preferred_element_type=F32)
    act = (h1 * jax.nn.sigmoid(h1) * h2).astype(BF16)
    ys_ref[...] = jnp.dot(act, wd_sc[...], preferred_element_type=F32)


def _experts(blk_expert, xs, wg, wu, wd):
    n_rows = xs.shape[0]
    n_blk = n_rows // EXPERT_ROWS
    grid_spec = pltpu.PrefetchScalarGridSpec(
        num_scalar_prefetch=1,
        grid=(n_blk,),
        in_specs=[pl.BlockSpec((EXPERT_ROWS, D_MODEL), lambda i, be: (i, 0)),
                  pl.BlockSpec((1, D_MODEL, D_EXPERT), lambda i, be: (be[i], 0, 0)),
                  pl.BlockSpec((1, D_MODEL, D_EXPERT), lambda i, be: (be[i], 0, 0)),
                  pl.BlockSpec((1, D_EXPERT, D_MODEL), lambda i, be: (be[i], 0, 0))],
        out_specs=pl.BlockSpec((EXPERT_ROWS, D_MODEL), lambda i, be: (i, 0)),
        scratch_shapes=[pltpu.VMEM((D_MODEL, D_EXPERT), BF16), pltpu.VMEM((D_MODEL, D_EXPERT), BF16),
                        pltpu.VMEM((D_EXPERT, D_MODEL), BF16)],
    )
    return pl.pallas_call(
        _experts_body,
        grid_spec=grid_spec,
        out_shape=jax.ShapeDtypeStruct((n_rows, D_MODEL), F32),
        compiler_params=_cparams(("arbitrary",)),
        name="experts",
    )(blk_expert, xs, wg, wu, wd)


def _ln2_body(xp_ref, xs_ref, y0_ref, y1_ref, gt_ref, g_ref, b_ref, op_ref, os_ref, *, n_ptiles):
    i = pl.program_id(0)
    gt = gt_ref[...]
    f = gt[:, 0:1] * y0_ref[...] + gt[:, 1:2] * y1_ref[...]

    @pl.when(i < n_ptiles)
    def _():
        op_ref[...] = _layer_norm(DEEPNORM_ALPHA * xp_ref[...] + f, g_ref[...], b_ref[...])

    @pl.when(i >= n_ptiles)
    def _():
        os_ref[...] = _layer_norm(DEEPNORM_ALPHA * xs_ref[...] + f, g_ref[...], b_ref[...])


def _ln2(x_p, x_s, y0, y1, gate, g, b, *, tm):
    n_p, n_s = x_p.shape[0], x_s.shape[0]
    n_ptiles, n_stiles = n_p // tm, n_s // tm
    assert n_p % tm == 0 and n_s % tm == 0
    row = lambda n: pl.BlockSpec((tm, n), lambda i: (i, 0))
    p_row = pl.BlockSpec((tm, D_MODEL), lambda i: (jnp.minimum(i, n_ptiles - 1), 0))
    s_row = pl.BlockSpec((tm, D_MODEL), lambda i: (jnp.maximum(i - n_ptiles, 0), 0))
    vec = pl.BlockSpec((1, D_MODEL), lambda i: (0, 0))
    return pl.pallas_call(
        functools.partial(_ln2_body, n_ptiles=n_ptiles),
        grid=(n_ptiles + n_stiles,),
        in_specs=[p_row, s_row, row(D_MODEL), row(D_MODEL), row(TOP_K), vec, vec],
        out_specs=[p_row, s_row],
        out_shape=[jax.ShapeDtypeStruct((n_p, D_MODEL), F32), jax.ShapeDtypeStruct((n_s, D_MODEL), F32)],
        compiler_params=_cparams(("arbitrary",)),
        name="ln2",
    )(x_p, x_s, y0, y1, gate, g, b)


def _moe(h, expert, wg, wu, wd):
    n = h.shape[0]
    s = n * TOP_K
    e_flat = expert.reshape(-1)
    onehot = (e_flat[:, None] == jnp.arange(N_EXPERTS, dtype=jnp.int32)[None, :]).astype(jnp.int32)
    csum = jnp.cumsum(onehot, axis=0)
    counts = csum[-1]
    pos_in = jnp.take_along_axis(csum, e_flat[:, None], axis=1)[:, 0] - 1
    padded = (counts + EXPERT_ROWS - 1) // EXPERT_ROWS * EXPERT_ROWS
    pad_end = jnp.cumsum(padded)
    pad_start = pad_end - padded
    dest = pad_start[e_flat] + pos_in
    n_blk = (s + N_EXPERTS * (EXPERT_ROWS - 1) + EXPERT_ROWS - 1) // EXPERT_ROWS
    tok = jnp.arange(s, dtype=jnp.int32) // TOP_K
    row_tok = jnp.zeros((n_blk * EXPERT_ROWS,), jnp.int32).at[dest].set(tok)
    xs = h[row_tok]
    blk_start = jnp.arange(n_blk, dtype=jnp.int32) * EXPERT_ROWS
    blk_expert = jnp.minimum(jnp.sum((pad_end[None, :] <= blk_start[:, None]).astype(jnp.int32), axis=1),
                             N_EXPERTS - 1)
    ys = _experts(blk_expert, xs, wg, wu, wd)
    dest2 = dest.reshape(n, TOP_K)
    return ys[dest2[:, 0]], ys[dest2[:, 1]]


def kernel(x_prompt, x_sample, cache_k, cache_v, state_pool, page_table, rel_bias, w_router, b_router,
           w_in, pool_w, pool_scale, w_pool_out, w_attn_out, w_o, ln1_g, ln1_b, w_gate, w_up, w_down,
           ln2_g, ln2_b):
    bsz, seq, _ = x_prompt.shape
    db, ts, _ = x_sample.shape
    past_len = page_table.shape[1] * PAGE_SIZE
    nb = seq // MOBA_BLOCK
    nbs = past_len // MOBA_BLOCK
    assert seq % MOBA_BLOCK == 0 and past_len % MOBA_BLOCK == 0 and ts <= 8
    np_rows = bsz * seq
    ns_rows = db * ts

    far_s = _far_bucket(MOBA_BLOCK + 1, past_len + ts)
    c = np.arange(MOBA_BLOCK)
    t_idx = np.repeat(np.arange(ts), N_HEADS)
    h_idx = np.tile(np.arange(N_HEADS), ts)
    d_last = MOBA_BLOCK + t_idx[:, None] - c[None, :]
    b_last_s = rel_bias[_bucket_np(d_last), h_idx[:, None]]
    tn = np.arange(8)
    d_new = t_idx[:, None] - tn[None, :]
    b_own_s = jnp.where((d_new >= 0) & (tn[None, :] < ts),
                        rel_bias[_bucket_np(np.maximum(d_new, 0)), h_idx[:, None]], NEG_INF)
    cfar_s = rel_bias[far_s][h_idx][:, None]
    head_mask = jnp.asarray((np.arange(ATT_WIDTH)[None, :] // HEAD_DIM) == h_idx[:, None])
    cache_kt = jnp.transpose(cache_k, (0, 2, 3, 4, 1))
    cache_vt = jnp.transpose(cache_v, (0, 2, 3, 4, 1))

    wr_pad = jnp.zeros((D_MODEL, ROUTER_PAD), BF16).at[:, :N_EXPERTS].set(w_router.astype(BF16))
    br_pad = jnp.zeros((1, ROUTER_PAD), F32).at[0, :N_EXPERTS].set(b_router)

    xp = x_prompt.reshape(np_rows, D_MODEL)
    xs = x_sample.reshape(ns_rows, D_MODEL)
    kp, vp, pp, ksm, vsm, psm = [], [], [], [], [], []
    for l in range(DEPTH):
        w_in_bf = w_in[l].astype(BF16)
        pool_w_bf = pool_w[l].astype(BF16)
        scale = pool_scale[l][None, :]
        wpo = w_pool_out[l].astype(BF16)
        wao = w_attn_out[l].astype(BF16)
        wo = w_o[l].astype(BF16)
        g1, b1 = ln1_g[l][None, :], ln1_b[l][None, :]
        g2, b2 = ln2_g[l][None, :], ln2_b[l][None, :]

        p_p, q_p, k_p, v_p, gp_p, ga_p, kb_p, vt_p, km_p = _inproj(xp, w_in_bf, tm=MOBA_BLOCK, attn_layouts=True)
        prev0 = jnp.zeros((bsz, POOL_HALO, POOL_WIDTH), F32)
        yp_p = _pool(p_p.reshape(bsz, seq, POOL_WIDTH), prev0, pool_w_bf, scale, tq=MOBA_BLOCK, pos0=0)
        o_p = _moba_prompt(q_p.reshape(bsz, seq, ATT_WIDTH),
                           kb_p.reshape(bsz, nb, MOBA_BLOCK, ATT_WIDTH),
                           vt_p.reshape(bsz, nb, ATT_WIDTH, MOBA_BLOCK),
                           km_p.reshape(bsz, nb, ATT_WIDTH), rel_bias)
        x1_p, rt_p = _merge(xp, yp_p.reshape(np_rows, POOL_WIDTH), o_p.reshape(np_rows, ATT_WIDTH),
                            gp_p, ga_p, wpo, wao, wo, g1, b1, wr_pad, br_pad, tm=MOBA_BLOCK)

        p_s, q_s, k_s, v_s, gp_s, ga_s = _inproj(xs, w_in_bf, tm=ns_rows, attn_layouts=False)
        state = state_pool[:, l]
        prev_s = jnp.concatenate([jnp.zeros((db, 1, POOL_WIDTH), F32), state], axis=1)
        p_s3 = p_s.reshape(db, ts, POOL_WIDTH)
        p_s8 = jnp.pad(p_s3, ((0, 0), (0, 8 - ts), (0, 0)))
        yp_s = _pool(p_s8, prev_s, pool_w_bf, scale, tq=8, pos0=past_len)[:, :ts]
        qbd = jnp.where(head_mask[None], jnp.repeat(q_s.reshape(db, ts, ATT_WIDTH), N_HEADS, axis=1), 0.0)
        knew8 = jnp.pad(k_s.reshape(db, ts, ATT_WIDTH), ((0, 0), (0, 8 - ts), (0, 0)))
        vnew8 = jnp.pad(v_s.reshape(db, ts, ATT_WIDTH), ((0, 0), (0, 8 - ts), (0, 0)))
        o_s = _moba_sample(page_table, qbd, knew8, vnew8, cfar_s, b_last_s, b_own_s,
                           cache_kt, cache_vt, l)[:, :ts]
        x1_s, rt_s = _merge(xs, yp_s.reshape(ns_rows, POOL_WIDTH), o_s.reshape(ns_rows, ATT_WIDTH),
                            gp_s, ga_s, wpo, wao, wo, g1, b1, wr_pad, br_pad, tm=ns_rows)

        x1 = jnp.concatenate([x1_p, x1_s], axis=0)
        e_p, gt_p = _unpack_route(rt_p)
        e_s, gt_s = _unpack_route(rt_s)
        y0, y1 = _moe(x1, jnp.concatenate([e_p, e_s], axis=0), w_gate[l], w_up[l], w_down[l])
        xp, xs = _ln2(x1_p, x1_s, y0, y1, jnp.concatenate([gt_p, gt_s], axis=0), g2, b2, tm=EXPERT_ROWS)

        kp.append(k_p.reshape(bsz, seq, N_HEADS, HEAD_DIM))
        vp.append(v_p.reshape(bsz, seq, N_HEADS, HEAD_DIM))
        pp.append(p_p.reshape(bsz, seq, POOL_WIDTH)[:, seq - POOL_BUF:])
        ksm.append(k_s.reshape(db, ts, N_HEADS, HEAD_DIM))
        vsm.append(v_s.reshape(db, ts, N_HEADS, HEAD_DIM))
        psm.append(jnp.concatenate([state, p_s3], axis=1)[:, -POOL_BUF:])

    return (xp.reshape(bsz, seq, D_MODEL), xs.reshape(db, ts, D_MODEL),
            jnp.stack(kp, axis=2), jnp.stack(vp, axis=2), jnp.stack(pp, axis=1),
            jnp.stack(ksm, axis=2), jnp.stack(vsm, axis=2), jnp.stack(psm, axis=1))
```

```python
import functools
import math

import numpy as np
import jax
import jax.numpy as jnp
from jax import lax
from jax.experimental import pallas as pl
from jax.experimental.pallas import tpu as pltpu

F32 = jnp.float32
BF16 = jnp.bfloat16
NEG_INF = float("-inf")

D_MODEL = 1024
DEPTH = 2
N_HEADS = 8
HEAD_DIM = 64
ATT_WIDTH = N_HEADS * HEAD_DIM
MOBA_BLOCK = 256
MOBA_TOPK = 3
N_BUCKETS = 32
REL_MAX_DIST = 128
POOL_WINDOWS = (2, 4, 8, 16)
POOL_WIDTH = 512
POOL_GW = 128
POOL_BUF = 15
POOL_HALO = 16
IN_WIDTH = POOL_WIDTH + 3 * ATT_WIDTH + 2 * D_MODEL
N_EXPERTS = 16
N_GROUPS = 4
EXPERTS_PER_GROUP = 4
TOP_K = 2
D_EXPERT = 512
EXPERT_ROWS = 512
ROW_TILE = 128
DEEPNORM_ALPHA = (2 * DEPTH) ** 0.25
LN_EPS = 1e-5
PAGE_SIZE = 128
ROUTER_PAD = 128
LANES = 128

VMEM_LIMIT = 52 * 1024 * 1024


def _cparams(sem):
    return pltpu.CompilerParams(dimension_semantics=sem, vmem_limit_bytes=VMEM_LIMIT)


def _nt_dot(a, b):
    return lax.dot_general(a, b, (((1,), (1,)), ((), ())), preferred_element_type=F32)


def _top_mask(scores, cand, k, n):
    idx = lax.broadcasted_iota(jnp.int32, scores.shape, 0)
    sel = jnp.zeros(scores.shape, jnp.bool_)
    for _ in range(k):
        c = cand & jnp.logical_not(sel)
        cur = jnp.where(c, scores, NEG_INF)
        mx = jnp.max(cur, axis=0, keepdims=True)
        first = jnp.min(jnp.where(c & (cur == mx), idx, n), axis=0, keepdims=True)
        sel = sel | (idx == first)
    return sel


def _inproj_body(x_ref, w_ref, p_ref, q_ref, k_ref, v_ref, gp_ref, ga_ref, *attn_refs, tm):
    xb = x_ref[...].astype(BF16)

    def seg(a, b):
        return jnp.dot(xb, w_ref[:, a:b], preferred_element_type=F32)

    c0 = POOL_WIDTH
    p_ref[...] = seg(0, c0)
    q_ref[...] = seg(c0, c0 + ATT_WIDTH) * (HEAD_DIM ** -0.5)
    k = seg(c0 + ATT_WIDTH, c0 + 2 * ATT_WIDTH)
    v = seg(c0 + 2 * ATT_WIDTH, c0 + 3 * ATT_WIDTH)
    g0 = c0 + 3 * ATT_WIDTH
    gp_ref[...] = jax.nn.sigmoid(seg(g0, g0 + D_MODEL))
    ga_ref[...] = jax.nn.sigmoid(seg(g0 + D_MODEL, g0 + 2 * D_MODEL))
    if attn_refs:
        kb_ref, vt_ref, km_ref = attn_refs
        k_ref[...] = k.T
        vt = v.T
        v_ref[...] = vt
        kb_ref[...] = k.astype(BF16)
        vt_ref[0] = vt.astype(BF16)
        km_ref[0] = jnp.sum(k, axis=0, keepdims=True) * (1.0 / tm)
    else:
        k_ref[...] = k
        v_ref[...] = v


def _inproj(x, w_bf, *, tm, attn_layouts, seq=None):
    m = x.shape[0]
    nt = m // tm
    row = lambda n: pl.BlockSpec((tm, n), lambda i: (i, 0))
    out_shape = [jax.ShapeDtypeStruct((m, POOL_WIDTH), F32),
                 jax.ShapeDtypeStruct((m, ATT_WIDTH), F32),
                 jax.ShapeDtypeStruct((m, ATT_WIDTH), F32),
                 jax.ShapeDtypeStruct((m, ATT_WIDTH), F32),
                 jax.ShapeDtypeStruct((m, D_MODEL), F32),
                 jax.ShapeDtypeStruct((m, D_MODEL), F32)]
    out_specs = [row(POOL_WIDTH), row(ATT_WIDTH), row(ATT_WIDTH), row(ATT_WIDTH),
                 row(D_MODEL), row(D_MODEL)]
    if attn_layouts:
        assert tm == MOBA_BLOCK
        tps = seq // tm
        for idx in (2, 3):
            out_shape[idx] = jax.ShapeDtypeStruct((m // seq, ATT_WIDTH, seq), F32)
            out_specs[idx] = pl.BlockSpec((None, ATT_WIDTH, tm), lambda i: (i // tps, 0, i % tps))
        out_shape += [jax.ShapeDtypeStruct((m, ATT_WIDTH), BF16),
                      jax.ShapeDtypeStruct((nt, ATT_WIDTH, tm), BF16),
                      jax.ShapeDtypeStruct((nt, 1, ATT_WIDTH), F32)]
        out_specs += [row(ATT_WIDTH),
                      pl.BlockSpec((1, ATT_WIDTH, tm), lambda i: (i, 0, 0)),
                      pl.BlockSpec((1, 1, ATT_WIDTH), lambda i: (i, 0, 0))]
    return pl.pallas_call(
        functools.partial(_inproj_body, tm=tm),
        grid=(nt,),
        in_specs=[row(D_MODEL), pl.BlockSpec((D_MODEL, IN_WIDTH), lambda i: (0, 0))],
        out_specs=out_specs,
        out_shape=out_shape,
        compiler_params=_cparams(("arbitrary",)),
        name="inproj",
    )(x, w_bf)


def _pool_body(prev_ref, halo_ref, p_ref, pw_ref, sc_ref, y_ref, *, tq, pos0):
    i = pl.program_id(1)
    p = p_ref[0]
    halo = jnp.where(i == 0, prev_ref[0], halo_ref[0])
    z = jnp.concatenate([halo, p], axis=0)
    lane = lax.broadcasted_iota(jnp.int32, z.shape, 1)
    x = z
    for s in (8, 4, 2, 1):
        thr = POOL_WIDTH - POOL_GW * {8: 1, 4: 2, 2: 3, 1: 4}[s]
        sh = pltpu.roll(x, s, 0)
        x = x + (jnp.where(lane >= thr, sh, 0.0) if thr > 0 else sh)
    wsum = x[POOL_HALO:, :]
    row = lax.broadcasted_iota(jnp.int32, (tq, POOL_WIDTH), 0)
    lane2 = lax.broadcasted_iota(jnp.int32, (tq, POOL_WIDTH), 1)
    wl = jnp.where(lane2 < POOL_GW, POOL_WINDOWS[0],
                   jnp.where(lane2 < 2 * POOL_GW, POOL_WINDOWS[1],
                             jnp.where(lane2 < 3 * POOL_GW, POOL_WINDOWS[2], POOL_WINDOWS[3])))
    pos = pos0 + i * tq + row
    cnt = jnp.minimum(pos + 1, wl).astype(F32)
    d = (wsum / cnt - p).astype(BF16)
    ys = [jnp.dot(d[:, g * POOL_GW:(g + 1) * POOL_GW], pw_ref[g], preferred_element_type=F32)
          for g in range(len(POOL_WINDOWS))]
    y_ref[0] = jnp.concatenate(ys, axis=1) * sc_ref[...]


def _pool(p, prev16, pool_w_bf, scale, *, tq, pos0):
    b, t, c = p.shape
    nq = t // tq
    hb = tq // POOL_HALO
    return pl.pallas_call(
        functools.partial(_pool_body, tq=tq, pos0=pos0),
        grid=(b, nq),
        in_specs=[pl.BlockSpec((1, POOL_HALO, c), lambda bi, i: (bi, 0, 0)),
                  pl.BlockSpec((1, POOL_HALO, c), lambda bi, i: (bi, jnp.maximum(i * hb - 1, 0), 0)),
                  pl.BlockSpec((1, tq, c), lambda bi, i: (bi, i, 0)),
                  pl.BlockSpec((len(POOL_WINDOWS), POOL_GW, POOL_GW), lambda bi, i: (0, 0, 0)),
                  pl.BlockSpec((1, c), lambda bi, i: (0, 0))],
        out_specs=pl.BlockSpec((1, tq, c), lambda bi, i: (bi, i, 0)),
        out_shape=jax.ShapeDtypeStruct((b, t, c), F32),
        compiler_params=_cparams(("arbitrary", "arbitrary")),
        name="pool",
    )(prev16, p, p, pool_w_bf, scale)


def _bucket_np(dist):
    n = np.maximum(dist, 0)
    max_exact = N_BUCKETS // 2
    nf = np.maximum(n, 1).astype(np.float32)
    large = max_exact + (np.log(nf / np.float32(max_exact)) / np.float32(math.log(REL_MAX_DIST / max_exact))
                         * np.float32(N_BUCKETS - max_exact)).astype(np.int32)
    large = np.minimum(large, N_BUCKETS - 1)
    return np.where(n < max_exact, n, large)


def _far_bucket(min_dist, max_dist):
    b = _bucket_np(np.arange(min_dist, max_dist + 1))
    assert (b == b[0]).all()
    return int(b[0])


def _moba_prompt_body(rb_ref, q_ref, kb_ref, vt_ref, km_ref, io_ref, ip_ref, o_ref,
                      sel_ref, qz_ref, m_ref, l_ref, acc_ref, bown_ref, bprev_ref, *, nb, far):
    i = pl.program_id(1)
    tq = MOBA_BLOCK

    @pl.when((pl.program_id(0) == 0) & (i == 0))
    def _():
        for h in range(N_HEADS):
            bown_ref[h] = jnp.full((tq, tq), NEG_INF, F32)
            bprev_ref[h] = jnp.zeros((tq, tq), F32)
        io = io_ref[...]
        ip = ip_ref[...]

        def fill(bkt, carry):
            mo = io == bkt
            mp = ip == bkt
            for h in range(N_HEADS):
                val = rb_ref[bkt, h] - rb_ref[far, h]
                bown_ref[h] = jnp.where(mo, val, bown_ref[h])
                bprev_ref[h] = jnp.where(mp, val, bprev_ref[h])
            return carry

        lax.fori_loop(0, N_BUCKETS, fill, 0)

    blk = lax.broadcasted_iota(jnp.int32, (nb, tq), 0)
    valid = blk < i
    lane = lax.broadcasted_iota(jnp.int32, (tq, LANES), 1)
    for hp in range(N_HEADS // 2):
        qf = q_ref[0, :, LANES * hp:LANES * (hp + 1)]
        km = km_ref[0, :, LANES * hp:LANES * (hp + 1)].astype(BF16)
        for hh in range(2):
            h = 2 * hp + hh
            qz = jnp.where((lane >= HEAD_DIM * hh) & (lane < HEAD_DIM * (hh + 1)), qf, 0.0).astype(BF16)
            qz_ref[h] = qz
            sel = _top_mask(_nt_dot(km, qz), valid, MOBA_TOPK, nb)
            selb = jnp.where(sel, 0.0, NEG_INF)
            for jj in range(nb):
                sel_ref[h, jj] = selb[jj:jj + 1, :]
            m_ref[h] = jnp.full((1, tq), -1e30, F32)
            l_ref[h] = jnp.zeros((1, tq), F32)
            acc_ref[h] = jnp.zeros((HEAD_DIM, tq), F32)

    def key_block(j, bias_ref, masked):
        sts = [_nt_dot(kb_ref[0, j, :, LANES * (h // 2):LANES * (h // 2 + 1)], qz_ref[h])
               for h in range(N_HEADS)]
        ps, alphas = [], []
        for h in range(N_HEADS):
            st = sts[h] if bias_ref is None else sts[h] + bias_ref[h]
            cm = jnp.max(st, axis=0, keepdims=True)
            m = m_ref[h]
            if masked:
                mb = sel_ref[h, j]
                m_new = jnp.maximum(m, cm + mb)
                p = jnp.exp(st + (mb - m_new))
            else:
                m_new = jnp.maximum(m, cm)
                p = jnp.exp(st - m_new)
            alpha = jnp.exp(m - m_new)
            l_ref[h] = alpha * l_ref[h] + jnp.sum(p, axis=0, keepdims=True)
            m_ref[h] = m_new
            ps.append(p.astype(BF16))
            alphas.append(alpha)
        for h in range(N_HEADS):
            vt = vt_ref[0, j, HEAD_DIM * h:HEAD_DIM * (h + 1), :]
            acc_ref[h] = alphas[h] * acc_ref[h] + jnp.dot(vt, ps[h], preferred_element_type=F32)

    def far_body(j, carry):
        key_block(j, None, True)
        return carry

    lax.fori_loop(0, i - 1, far_body, 0)
    jp = jnp.maximum(i - 1, 0)
    key_block(jp, bprev_ref, True)
    key_block(i, bown_ref, False)
    for hp in range(N_HEADS // 2):
        pair = jnp.concatenate([acc_ref[2 * hp] / l_ref[2 * hp], acc_ref[2 * hp + 1] / l_ref[2 * hp + 1]],
                               axis=0)
        o_ref[0, :, LANES * hp:LANES * (hp + 1)] = pair.T


def _moba_prompt(q, kb, vt, km, rel_bias):
    b, t, _ = q.shape
    nb = t // MOBA_BLOCK
    far = _far_bucket(MOBA_BLOCK + 1, t)
    c = np.arange(MOBA_BLOCK)
    d_own = c[None, :] - c[:, None]
    idx_own = jnp.asarray(np.where(d_own >= 0, _bucket_np(d_own), -1).astype(np.int32))
    idx_prev = jnp.asarray(_bucket_np(d_own + MOBA_BLOCK).astype(np.int32))
    once = pl.Buffered(1)
    tile = pl.BlockSpec((MOBA_BLOCK, MOBA_BLOCK), lambda bi, i: (0, 0), pipeline_mode=once)
    return pl.pallas_call(
        functools.partial(_moba_prompt_body, nb=nb, far=far),
        grid=(b, nb),
        in_specs=[pl.BlockSpec(memory_space=pltpu.SMEM),
                  pl.BlockSpec((1, MOBA_BLOCK, ATT_WIDTH), lambda bi, i: (bi, i, 0)),
                  pl.BlockSpec((1, nb, MOBA_BLOCK, ATT_WIDTH), lambda bi, i: (bi, 0, 0, 0), pipeline_mode=once),
                  pl.BlockSpec((1, nb, ATT_WIDTH, MOBA_BLOCK), lambda bi, i: (bi, 0, 0, 0), pipeline_mode=once),
                  pl.BlockSpec((1, nb, ATT_WIDTH), lambda bi, i: (bi, 0, 0)),
                  tile, tile],
        out_specs=pl.BlockSpec((1, MOBA_BLOCK, ATT_WIDTH), lambda bi, i: (bi, i, 0)),
        out_shape=jax.ShapeDtypeStruct((b, t, ATT_WIDTH), F32),
        scratch_shapes=[pltpu.VMEM((N_HEADS, nb, 1, MOBA_BLOCK), F32),
                        pltpu.VMEM((N_HEADS, MOBA_BLOCK, LANES), BF16),
                        pltpu.VMEM((N_HEADS, 1, MOBA_BLOCK), F32),
                        pltpu.VMEM((N_HEADS, 1, MOBA_BLOCK), F32),
                        pltpu.VMEM((N_HEADS, HEAD_DIM, MOBA_BLOCK), F32),
                        pltpu.VMEM((N_HEADS, MOBA_BLOCK, MOBA_BLOCK), F32),
                        pltpu.VMEM((N_HEADS, MOBA_BLOCK, MOBA_BLOCK), F32)],
        compiler_params=_cparams(("arbitrary", "arbitrary")),
        name="moba_prompt",
    )(rel_bias, q, kb, vt, km, idx_own, idx_prev)


SAMPLE_PAGES_PER_STEP = 16
PAGES_PER_BLOCK = MOBA_BLOCK // PAGE_SIZE


def _moba_sample_body(pt_ref, qbd_ref, knew_ref, vnew_ref, cfar_ref, blast_ref, bown_ref, *rest,
                      nbs, ts):
    npg = SAMPLE_PAGES_PER_STEP
    ppb = PAGES_PER_BLOCK
    k_refs = rest[:npg]
    v_refs = rest[npg:2 * npg]
    o_ref = rest[2 * npg]
    gs_sc, m_sc, l_sc, o_sc = rest[2 * npg + 1:]
    s = pl.program_id(1)
    nsteps = pl.num_programs(1)
    bps = npg // ppb
    nr = ts * N_HEADS
    qb = qbd_ref[0].astype(BF16)
    lane0 = lax.broadcasted_iota(jnp.int32, (ATT_WIDTH, LANES), 1) == 0
    sts = []
    for jj in range(bps):
        kts = [k_refs[ppb * jj + u][...].reshape(ATT_WIDTH, PAGE_SIZE) for u in range(ppb)]
        ksum = functools.reduce(lambda a, b: a + b, [jnp.sum(kt, axis=1, keepdims=True) for kt in kts])
        kmean = ksum * (1.0 / MOBA_BLOCK)
        kmcol = jnp.where(lane0, kmean, 0.0).astype(BF16)
        w = jnp.concatenate([kt.astype(BF16) for kt in kts] + [kmcol], axis=1)
        sts.append(jnp.dot(qb, w, preferred_element_type=F32))
    ps = []
    for jj in range(bps):
        j = s * bps + jj
        gs_sc[j] = sts[jj][:, MOBA_BLOCK:MOBA_BLOCK + 1]
        bias = jnp.where(j == nbs - 1, blast_ref[...], cfar_ref[...])
        st = sts[jj][:, :MOBA_BLOCK] + bias
        m = jnp.max(st, axis=1, keepdims=True)
        p = jnp.exp(st - m)
        m_sc[j] = m
        l_sc[j] = jnp.sum(p, axis=1, keepdims=True)
        ps.append(p.astype(BF16))
    for jj in range(bps):
        o = None
        for u in range(ppb):
            vt = v_refs[ppb * jj + u][...].reshape(ATT_WIDTH, PAGE_SIZE).astype(BF16)
            part = _nt_dot(ps[jj][:, PAGE_SIZE * u:PAGE_SIZE * (u + 1)], vt)
            o = part if o is None else o + part
        o_sc[s * bps + jj] = o

    @pl.when(s == nsteps - 1)
    def _():
        qf = qb.astype(F32)
        gs = gs_sc[...]
        sel = _top_mask(gs, jnp.ones(gs.shape, jnp.bool_), MOBA_TOPK, nbs)
        knew = knew_ref[0].astype(BF16).astype(F32)
        vnew = vnew_ref[0].astype(BF16).astype(F32)
        s_own = [jnp.sum(qf * knew[t:t + 1, :], axis=1, keepdims=True) + bown_ref[:, t:t + 1]
                 for t in range(ts)]
        m_all = m_sc[...]
        mtot = jnp.max(jnp.where(sel, m_all, NEG_INF), axis=0)
        for t in range(ts):
            mtot = jnp.maximum(mtot, s_own[t])
        w = jnp.where(sel, jnp.exp(m_all - mtot[None]), 0.0)
        ltot = jnp.sum(w * l_sc[...], axis=0)
        gs_sc[...] = w

        def merge(j, acc):
            return acc + gs_sc[j] * o_sc[j]

        otot = lax.fori_loop(0, nbs, merge, jnp.zeros((nr, ATT_WIDTH), F32))
        for t in range(ts):
            pt = jnp.exp(s_own[t] - mtot)
            ltot = ltot + pt
            otot = otot + pt.astype(BF16).astype(F32) * vnew[t:t + 1, :]
        out = otot / ltot
        row = lax.broadcasted_iota(jnp.int32, out.shape, 0)
        lane = lax.broadcasted_iota(jnp.int32, out.shape, 1)
        out = jnp.where((lane // HEAD_DIM) == (row % N_HEADS), out, 0.0)
        pieces = [jnp.sum(out[N_HEADS * t:N_HEADS * (t + 1), :], axis=0, keepdims=True) for t in range(ts)]
        pieces.append(jnp.zeros((8 - ts, ATT_WIDTH), F32))
        o_ref[0] = jnp.concatenate(pieces, axis=0)


def _moba_sample(page_table, qbd, knew8, vnew8, cfar_rows, b_last, b_own, cache_kt, cache_vt, layer):
    db, nr, _ = qbd.shape
    ts = nr // N_HEADS
    n_pages = page_table.shape[1]
    nbs = n_pages // PAGES_PER_BLOCK
    npg = SAMPLE_PAGES_PER_STEP
    nsteps = n_pages // npg

    def page_spec(u):
        return pl.BlockSpec((None, None, N_HEADS, HEAD_DIM, PAGE_SIZE),
                            lambda b, s, pt: (pt[b, s * npg + u], layer, 0, 0, 0))

    full2 = lambda shp: pl.BlockSpec(shp, lambda b, s, pt: (0, 0))
    per_b = lambda r: pl.BlockSpec((1, r, ATT_WIDTH), lambda b, s, pt: (b, 0, 0))
    grid_spec = pltpu.PrefetchScalarGridSpec(
        num_scalar_prefetch=1,
        grid=(db, nsteps),
        in_specs=[per_b(nr), per_b(8), per_b(8),
                  full2((nr, 1)), full2((nr, MOBA_BLOCK)), full2((nr, 8))]
                 + [page_spec(u) for u in range(npg)] + [page_spec(u) for u in range(npg)],
        out_specs=per_b(8),
        scratch_shapes=[pltpu.VMEM((nbs, nr, 1), F32), pltpu.VMEM((nbs, nr, 1), F32),
                        pltpu.VMEM((nbs, nr, 1), F32), pltpu.VMEM((nbs, nr, ATT_WIDTH), F32)],
    )
    return pl.pallas_call(
        functools.partial(_moba_sample_body, nbs=nbs, ts=ts),
        grid_spec=grid_spec,
        out_shape=jax.ShapeDtypeStruct((db, 8, ATT_WIDTH), F32),
        compiler_params=_cparams(("arbitrary", "arbitrary")),
        name="moba_sample",
    )(page_table, qbd, knew8, vnew8, cfar_rows, b_last, b_own,
      *([cache_kt] * npg), *([cache_vt] * npg))


def _layer_norm(h, g, b):
    mu = jnp.mean(h, axis=-1, keepdims=True)
    c = h - mu
    var = jnp.mean(c * c, axis=-1, keepdims=True)
    return c * lax.rsqrt(var + LN_EPS) * g + b


def _argmax_first(vals):
    best, idx = vals[0], jnp.zeros(vals[0].shape, jnp.int32)
    for k in range(1, len(vals)):
        upd = vals[k] > best
        idx = jnp.where(upd, k, idx)
        best = jnp.where(upd, vals[k], best)
    return best, idx


def _route_rows(logit_rows):
    mx = functools.reduce(jnp.maximum, logit_rows)
    ex = [jnp.exp(r - mx) for r in logit_rows]
    tot = functools.reduce(lambda a, b: a + b, ex)
    probs = [e / tot for e in ex]
    scores = []
    for g in range(N_GROUPS):
        a, b, c, d = probs[EXPERTS_PER_GROUP * g:EXPERTS_PER_GROUP * (g + 1)]
        s1, t1 = jnp.maximum(a, b), jnp.minimum(a, b)
        s2, t2 = jnp.maximum(c, d), jnp.minimum(c, d)
        scores.append(jnp.maximum(s1, s2) + jnp.maximum(jnp.minimum(s1, s2), jnp.maximum(t1, t2)))
    _, gi = _argmax_first(scores)
    ing = []
    for j in range(EXPERTS_PER_GROUP):
        v = probs[j]
        for g in range(1, N_GROUPS):
            v = jnp.where(gi == g, probs[EXPERTS_PER_GROUP * g + j], v)
        ing.append(v)
    w1, i1 = _argmax_first(ing)
    w2, i2 = _argmax_first([jnp.where(i1 == j, -1.0, ing[j]) for j in range(EXPERTS_PER_GROUP)])
    den = w1 + w2
    e1 = (gi * EXPERTS_PER_GROUP + i1).astype(F32)
    e2 = (gi * EXPERTS_PER_GROUP + i2).astype(F32)
    return e1, e2, w1 / den, w2 / den


def _merge_body(x_ref, yp_ref, o_ref, gp_ref, ga_ref, wpo_ref, wao_ref, wo_ref, g_ref, b_ref,
                wr_ref, br_ref, x1_ref, rt_ref, *, tm):
    a = jnp.dot(yp_ref[...].astype(BF16), wpo_ref[...], preferred_element_type=F32)
    bb = jnp.dot(o_ref[...].astype(BF16), wao_ref[...], preferred_element_type=F32)
    merged = gp_ref[...] * a + ga_ref[...] * bb
    mix = jnp.dot(merged.astype(BF16), wo_ref[...], preferred_element_type=F32)
    x1 = _layer_norm(DEEPNORM_ALPHA * x_ref[...] + mix, g_ref[...], b_ref[...])
    x1_ref[...] = x1
    lg = jnp.dot(x1.astype(BF16), wr_ref[...], preferred_element_type=F32) + br_ref[...]
    lgt = lg.T
    e1, e2, g1, g2 = _route_rows([lgt[e:e + 1, :] for e in range(N_EXPERTS)])
    r = lax.broadcasted_iota(jnp.int32, (8, tm), 0)
    rt_ref[0] = jnp.where(r == 0, e1, jnp.where(r == 1, e2, jnp.where(r == 2, g1, jnp.where(r == 3, g2, 0.0))))


def _merge(x, yp, o, gp, ga, wpo, wao, wo, g, b, wr, br, *, tm):
    m = x.shape[0]
    row = lambda n: pl.BlockSpec((tm, n), lambda i: (i, 0))
    full = lambda r, c: pl.BlockSpec((r, c), lambda i: (0, 0))
    return pl.pallas_call(
        functools.partial(_merge_body, tm=tm),
        grid=(m // tm,),
        in_specs=[row(D_MODEL), row(POOL_WIDTH), row(ATT_WIDTH), row(D_MODEL), row(D_MODEL),
                  full(POOL_WIDTH, D_MODEL), full(ATT_WIDTH, D_MODEL), full(D_MODEL, D_MODEL),
                  full(1, D_MODEL), full(1, D_MODEL), full(D_MODEL, ROUTER_PAD), full(1, ROUTER_PAD)],
        out_specs=[row(D_MODEL), pl.BlockSpec((1, 8, tm), lambda i: (i, 0, 0))],
        out_shape=[jax.ShapeDtypeStruct((m, D_MODEL), F32), jax.ShapeDtypeStruct((m // tm, 8, tm), F32)],
        compiler_params=_cparams(("arbitrary",)),
        name="merge",
    )(x, yp, o, gp, ga, wpo, wao, wo, g, b, wr, br)


def _unpack_route(rt):
    n = rt.shape[0] * rt.shape[2]
    cols = rt[:, :4, :].transpose(0, 2, 1).reshape(n, 4)
    return cols[:, :2].astype(jnp.int32), cols[:, 2:]


def _experts_body(be_ref, xs_ref, wg_ref, wu_ref, wd_ref, ys_ref, wg_sc, wu_sc, wd_sc):
    i = pl.program_id(0)

    @pl.when((i == 0) | (be_ref[i] != be_ref[jnp.maximum(i - 1, 0)]))
    def _():
        wg_sc[...] = wg_ref[0].astype(BF16)
        wu_sc[...] = wu_ref[0].astype(BF16)
        wd_sc[...] = wd_ref[0].astype(BF16)

    xb = xs_ref[...].astype(BF16)
    h1 = jnp.dot(xb, wg_sc[...], preferred_element_type=F32)
    h2 = jnp.dot(xb, wu_sc[...], preferred_element_type=F32)
    act = (h1 * jax.nn.sigmoid(h1) * h2).astype(BF16)
    ys_ref[...] = jnp.dot(act, wd_sc[...], preferred_element_type=F32)


def _experts(blk_expert, xs, wg, wu, wd, layer):
    n_rows = xs.shape[0]
    n_blk = n_rows // EXPERT_ROWS
    grid_spec = pltpu.PrefetchScalarGridSpec(
        num_scalar_prefetch=1,
        grid=(n_blk,),
        in_specs=[pl.BlockSpec((EXPERT_ROWS, D_MODEL), lambda i, be: (i, 0)),
                  pl.BlockSpec((None, 1, D_MODEL, D_EXPERT), lambda i, be: (layer, be[i], 0, 0)),
                  pl.BlockSpec((None, 1, D_MODEL, D_EXPERT), lambda i, be: (layer, be[i], 0, 0)),
                  pl.BlockSpec((None, 1, D_EXPERT, D_MODEL), lambda i, be: (layer, be[i], 0, 0))],
        out_specs=pl.BlockSpec((EXPERT_ROWS, D_MODEL), lambda i, be: (i, 0)),
        scratch_shapes=[pltpu.VMEM((D_MODEL, D_EXPERT), BF16), pltpu.VMEM((D_MODEL, D_EXPERT), BF16),
                        pltpu.VMEM((D_EXPERT, D_MODEL), BF16)],
    )
    return pl.pallas_call(
        _experts_body,
        grid_spec=grid_spec,
        out_shape=jax.ShapeDtypeStruct((n_rows, D_MODEL), F32),
        compiler_params=_cparams(("arbitrary",)),
        name="experts",
    )(blk_expert, xs, wg, wu, wd)


def _ln2_body(xp_ref, xs_ref, y0_ref, y1_ref, gt_ref, g_ref, b_ref, op_ref, os_ref, *, n_ptiles):
    i = pl.program_id(0)
    gt = gt_ref[...]
    f = gt[:, 0:1] * y0_ref[...] + gt[:, 1:2] * y1_ref[...]

    @pl.when(i < n_ptiles)
    def _():
        op_ref[...] = _layer_norm(DEEPNORM_ALPHA * xp_ref[...] + f, g_ref[...], b_ref[...])

    @pl.when(i >= n_ptiles)
    def _():
        os_ref[...] = _layer_norm(DEEPNORM_ALPHA * xs_ref[...] + f, g_ref[...], b_ref[...])


def _ln2(x_p, x_s, y0, y1, gate, g, b, *, tm):
    n_p, n_s = x_p.shape[0], x_s.shape[0]
    n_ptiles, n_stiles = n_p // tm, n_s // tm
    assert n_p % tm == 0 and n_s % tm == 0
    row = lambda n: pl.BlockSpec((tm, n), lambda i: (i, 0))
    p_row = pl.BlockSpec((tm, D_MODEL), lambda i: (jnp.minimum(i, n_ptiles - 1), 0))
    s_row = pl.BlockSpec((tm, D_MODEL), lambda i: (jnp.maximum(i - n_ptiles, 0), 0))
    vec = pl.BlockSpec((1, D_MODEL), lambda i: (0, 0))
    return pl.pallas_call(
        functools.partial(_ln2_body, n_ptiles=n_ptiles),
        grid=(n_ptiles + n_stiles,),
        in_specs=[p_row, s_row, row(D_MODEL), row(D_MODEL), row(TOP_K), vec, vec],
        out_specs=[p_row, s_row],
        out_shape=[jax.ShapeDtypeStruct((n_p, D_MODEL), F32), jax.ShapeDtypeStruct((n_s, D_MODEL), F32)],
        compiler_params=_cparams(("arbitrary",)),
        name="ln2",
    )(x_p, x_s, y0, y1, gate, g, b)


def _moe(h, expert, wg, wu, wd, layer):
    n = h.shape[0]
    s = n * TOP_K
    e_flat = expert.reshape(-1)
    onehot = (e_flat[:, None] == jnp.arange(N_EXPERTS, dtype=jnp.int32)[None, :]).astype(jnp.int32)
    csum = jnp.cumsum(onehot, axis=0)
    counts = csum[-1]
    pos_in = jnp.take_along_axis(csum, e_flat[:, None], axis=1)[:, 0] - 1
    padded = (counts + EXPERT_ROWS - 1) // EXPERT_ROWS * EXPERT_ROWS
    pad_end = jnp.cumsum(padded)
    pad_start = pad_end - padded
    dest = pad_start[e_flat] + pos_in
    n_blk = (s + N_EXPERTS * (EXPERT_ROWS - 1) + EXPERT_ROWS - 1) // EXPERT_ROWS
    tok = jnp.arange(s, dtype=jnp.int32) // TOP_K
    row_tok = jnp.zeros((n_blk * EXPERT_ROWS,), jnp.int32).at[dest].set(tok)
    xs = h[row_tok]
    blk_start = jnp.arange(n_blk, dtype=jnp.int32) * EXPERT_ROWS
    blk_expert = jnp.minimum(jnp.sum((pad_end[None, :] <= blk_start[:, None]).astype(jnp.int32), axis=1),
                             N_EXPERTS - 1)
    ys = _experts(blk_expert, xs, wg, wu, wd, layer)
    dest2 = dest.reshape(n, TOP_K)
    return ys[dest2[:, 0]], ys[dest2[:, 1]]


def kernel(x_prompt, x_sample, cache_k, cache_v, state_pool, page_table, rel_bias, w_router, b_router,
           w_in, pool_w, pool_scale, w_pool_out, w_attn_out, w_o, ln1_g, ln1_b, w_gate, w_up, w_down,
           ln2_g, ln2_b):
    bsz, seq, _ = x_prompt.shape
    db, ts, _ = x_sample.shape
    past_len = page_table.shape[1] * PAGE_SIZE
    nb = seq // MOBA_BLOCK
    nbs = past_len // MOBA_BLOCK
    assert seq % MOBA_BLOCK == 0 and past_len % MOBA_BLOCK == 0 and ts <= 8
    np_rows = bsz * seq
    ns_rows = db * ts

    far_s = _far_bucket(MOBA_BLOCK + 1, past_len + ts)
    c = np.arange(MOBA_BLOCK)
    t_idx = np.repeat(np.arange(ts), N_HEADS)
    h_idx = np.tile(np.arange(N_HEADS), ts)
    d_last = MOBA_BLOCK + t_idx[:, None] - c[None, :]
    b_last_s = rel_bias[_bucket_np(d_last), h_idx[:, None]]
    tn = np.arange(8)
    d_new = t_idx[:, None] - tn[None, :]
    b_own_s = jnp.where((d_new >= 0) & (tn[None, :] < ts),
                        rel_bias[_bucket_np(np.maximum(d_new, 0)), h_idx[:, None]], NEG_INF)
    cfar_s = rel_bias[far_s][h_idx][:, None]
    head_mask = jnp.asarray((np.arange(ATT_WIDTH)[None, :] // HEAD_DIM) == h_idx[:, None])
    cache_kt = jnp.transpose(cache_k, (0, 2, 3, 4, 1))
    cache_vt = jnp.transpose(cache_v, (0, 2, 3, 4, 1))

    wr_pad = jnp.zeros((D_MODEL, ROUTER_PAD), BF16).at[:, :N_EXPERTS].set(w_router.astype(BF16))
    br_pad = jnp.zeros((1, ROUTER_PAD), F32).at[0, :N_EXPERTS].set(b_router)

    xp = x_prompt.reshape(np_rows, D_MODEL)
    xs = x_sample.reshape(ns_rows, D_MODEL)
    kp, vp, pp, ksm, vsm, psm = [], [], [], [], [], []
    for l in range(DEPTH):
        w_in_bf = w_in[l].astype(BF16)
        pool_w_bf = pool_w[l].astype(BF16)
        scale = pool_scale[l][None, :]
        wpo = w_pool_out[l].astype(BF16)
        wao = w_attn_out[l].astype(BF16)
        wo = w_o[l].astype(BF16)
        g1, b1 = ln1_g[l][None, :], ln1_b[l][None, :]
        g2, b2 = ln2_g[l][None, :], ln2_b[l][None, :]

        p_p, q_p, k_p, v_p, gp_p, ga_p, kb_p, vt_p, km_p = _inproj(xp, w_in_bf, tm=MOBA_BLOCK, attn_layouts=True,
                                                                   seq=seq)
        prev0 = jnp.zeros((bsz, POOL_HALO, POOL_WIDTH), F32)
        yp_p = _pool(p_p.reshape(bsz, seq, POOL_WIDTH), prev0, pool_w_bf, scale, tq=MOBA_BLOCK, pos0=0)
        o_p = _moba_prompt(q_p.reshape(bsz, seq, ATT_WIDTH),
                           kb_p.reshape(bsz, nb, MOBA_BLOCK, ATT_WIDTH),
                           vt_p.reshape(bsz, nb, ATT_WIDTH, MOBA_BLOCK),
                           km_p.reshape(bsz, nb, ATT_WIDTH), rel_bias)
        x1_p, rt_p = _merge(xp, yp_p.reshape(np_rows, POOL_WIDTH), o_p.reshape(np_rows, ATT_WIDTH),
                            gp_p, ga_p, wpo, wao, wo, g1, b1, wr_pad, br_pad, tm=MOBA_BLOCK)

        p_s, q_s, k_s, v_s, gp_s, ga_s = _inproj(xs, w_in_bf, tm=ns_rows, attn_layouts=False)
        state = state_pool[:, l]
        prev_s = jnp.concatenate([jnp.zeros((db, 1, POOL_WIDTH), F32), state], axis=1)
        p_s3 = p_s.reshape(db, ts, POOL_WIDTH)
        p_s8 = jnp.pad(p_s3, ((0, 0), (0, 8 - ts), (0, 0)))
        yp_s = _pool(p_s8, prev_s, pool_w_bf, scale, tq=8, pos0=past_len)[:, :ts]
        qbd = jnp.where(head_mask[None], jnp.repeat(q_s.reshape(db, ts, ATT_WIDTH), N_HEADS, axis=1), 0.0)
        knew8 = jnp.pad(k_s.reshape(db, ts, ATT_WIDTH), ((0, 0), (0, 8 - ts), (0, 0)))
        vnew8 = jnp.pad(v_s.reshape(db, ts, ATT_WIDTH), ((0, 0), (0, 8 - ts), (0, 0)))
        o_s = _moba_sample(page_table, qbd, knew8, vnew8, cfar_s, b_last_s, b_own_s,
                           cache_kt, cache_vt, l)[:, :ts]
        x1_s, rt_s = _merge(xs, yp_s.reshape(ns_rows, POOL_WIDTH), o_s.reshape(ns_rows, ATT_WIDTH),
                            gp_s, ga_s, wpo, wao, wo, g1, b1, wr_pad, br_pad, tm=ns_rows)

        x1 = jnp.concatenate([x1_p, x1_s], axis=0)
        e_p, gt_p = _unpack_route(rt_p)
        e_s, gt_s = _unpack_route(rt_s)
        y0, y1 = _moe(x1, jnp.concatenate([e_p, e_s], axis=0), w_gate, w_up, w_down, l)
        xp, xs = _ln2(x1_p, x1_s, y0, y1, jnp.concatenate([gt_p, gt_s], axis=0), g2, b2, tm=ROW_TILE)

        kp.append(k_p)
        vp.append(v_p)
        pp.append(p_p.reshape(bsz, seq, POOL_WIDTH)[:, seq - POOL_BUF:])
        ksm.append(k_s.reshape(db, ts, N_HEADS, HEAD_DIM))
        vsm.append(v_s.reshape(db, ts, N_HEADS, HEAD_DIM))
        psm.append(jnp.concatenate([state, p_s3], axis=1)[:, -POOL_BUF:])

    def rows_out(parts):
        a = jnp.stack(parts, axis=1).reshape(bsz, DEPTH, N_HEADS, HEAD_DIM, seq)
        return jnp.transpose(a, (0, 4, 1, 2, 3))

    return (xp.reshape(bsz, seq, D_MODEL), xs.reshape(db, ts, D_MODEL),
            rows_out(kp), rows_out(vp), jnp.stack(pp, axis=1),
            jnp.stack(ksm, axis=2), jnp.stack(vsm, axis=2), jnp.stack(psm, axis=1))
```

```python
import functools
import math

import numpy as np
import jax
import jax.numpy as jnp
from jax import lax
from jax.experimental import pallas as pl
from jax.experimental.pallas import tpu as pltpu

F32 = jnp.float32
BF16 = jnp.bfloat16
NEG_INF = float("-inf")

D_MODEL = 1024
DEPTH = 2
N_HEADS = 8
HEAD_DIM = 64
ATT_WIDTH = N_HEADS * HEAD_DIM
MOBA_BLOCK = 256
MOBA_TOPK = 3
N_BUCKETS = 32
REL_MAX_DIST = 128
POOL_WINDOWS = (2, 4, 8, 16)
POOL_WIDTH = 512
POOL_GW = 128
POOL_BUF = 15
POOL_HALO = 16
IN_WIDTH = POOL_WIDTH + 3 * ATT_WIDTH + 2 * D_MODEL
N_EXPERTS = 16
N_GROUPS = 4
EXPERTS_PER_GROUP = 4
TOP_K = 2
D_EXPERT = 512
EXPERT_ROWS = 512
ROW_TILE = 128
DEEPNORM_ALPHA = (2 * DEPTH) ** 0.25
LN_EPS = 1e-5
PAGE_SIZE = 128
ROUTER_PAD = 128
LANES = 128

VMEM_LIMIT = 52 * 1024 * 1024


def _cparams(sem):
    return pltpu.CompilerParams(dimension_semantics=sem, vmem_limit_bytes=VMEM_LIMIT)


def _nt_dot(a, b):
    return lax.dot_general(a, b, (((1,), (1,)), ((), ())), preferred_element_type=F32)


def _top_mask(scores, cand, k, n):
    idx = lax.broadcasted_iota(jnp.int32, scores.shape, 0)
    sel = jnp.zeros(scores.shape, jnp.bool_)
    for _ in range(k):
        c = cand & jnp.logical_not(sel)
        cur = jnp.where(c, scores, NEG_INF)
        mx = jnp.max(cur, axis=0, keepdims=True)
        first = jnp.min(jnp.where(c & (cur == mx), idx, n), axis=0, keepdims=True)
        sel = sel | (idx == first)
    return sel


def _inproj_body(x_ref, w_ref, p_ref, q_ref, k_ref, v_ref, gp_ref, ga_ref, *attn_refs, tm):
    xb = x_ref[...].astype(BF16)

    def seg(a, b):
        return jnp.dot(xb, w_ref[:, a:b], preferred_element_type=F32)

    c0 = POOL_WIDTH
    p_ref[...] = seg(0, c0)
    q_ref[...] = seg(c0, c0 + ATT_WIDTH) * (HEAD_DIM ** -0.5)
    k = seg(c0 + ATT_WIDTH, c0 + 2 * ATT_WIDTH)
    v = seg(c0 + 2 * ATT_WIDTH, c0 + 3 * ATT_WIDTH)
    g0 = c0 + 3 * ATT_WIDTH
    gp_ref[...] = jax.nn.sigmoid(seg(g0, g0 + D_MODEL))
    ga_ref[...] = jax.nn.sigmoid(seg(g0 + D_MODEL, g0 + 2 * D_MODEL))
    if attn_refs:
        kb_ref, vt_ref, km_ref = attn_refs
        k_ref[...] = k.T
        vt = v.T
        v_ref[...] = vt
        kb_ref[...] = k.astype(BF16)
        vt_ref[0] = vt.astype(BF16)
        km_ref[0] = jnp.sum(k, axis=0, keepdims=True) * (1.0 / tm)
    else:
        k_ref[...] = k
        v_ref[...] = v


def _inproj(x, w_bf, *, tm, attn_layouts, seq=None):
    m = x.shape[0]
    nt = m // tm
    row = lambda n: pl.BlockSpec((tm, n), lambda i: (i, 0))
    out_shape = [jax.ShapeDtypeStruct((m, POOL_WIDTH), F32),
                 jax.ShapeDtypeStruct((m, ATT_WIDTH), F32),
                 jax.ShapeDtypeStruct((m, ATT_WIDTH), F32),
                 jax.ShapeDtypeStruct((m, ATT_WIDTH), F32),
                 jax.ShapeDtypeStruct((m, D_MODEL), F32),
                 jax.ShapeDtypeStruct((m, D_MODEL), F32)]
    out_specs = [row(POOL_WIDTH), row(ATT_WIDTH), row(ATT_WIDTH), row(ATT_WIDTH),
                 row(D_MODEL), row(D_MODEL)]
    if attn_layouts:
        assert tm == MOBA_BLOCK
        tps = seq // tm
        for idx in (2, 3):
            out_shape[idx] = jax.ShapeDtypeStruct((m // seq, ATT_WIDTH, seq), F32)
            out_specs[idx] = pl.BlockSpec((None, ATT_WIDTH, tm), lambda i: (i // tps, 0, i % tps))
        out_shape += [jax.ShapeDtypeStruct((m, ATT_WIDTH), BF16),
                      jax.ShapeDtypeStruct((nt, ATT_WIDTH, tm), BF16),
                      jax.ShapeDtypeStruct((nt, 1, ATT_WIDTH), F32)]
        out_specs += [row(ATT_WIDTH),
                      pl.BlockSpec((1, ATT_WIDTH, tm), lambda i: (i, 0, 0)),
                      pl.BlockSpec((1, 1, ATT_WIDTH), lambda i: (i, 0, 0))]
    return pl.pallas_call(
        functools.partial(_inproj_body, tm=tm),
        grid=(nt,),
        in_specs=[row(D_MODEL), pl.BlockSpec((D_MODEL, IN_WIDTH), lambda i: (0, 0))],
        out_specs=out_specs,
        out_shape=out_shape,
        compiler_params=_cparams(("arbitrary",)),
        name="inproj",
    )(x, w_bf)


def _pool_body(prev_ref, halo_ref, p_ref, pw_ref, sc_ref, y_ref, *, tq, pos0):
    i = pl.program_id(1)
    p = p_ref[0]
    halo = jnp.where(i == 0, prev_ref[0], halo_ref[0])
    z = jnp.concatenate([halo, p], axis=0)
    lane = lax.broadcasted_iota(jnp.int32, z.shape, 1)
    x = z
    for s in (8, 4, 2, 1):
        thr = POOL_WIDTH - POOL_GW * {8: 1, 4: 2, 2: 3, 1: 4}[s]
        sh = pltpu.roll(x, s, 0)
        x = x + (jnp.where(lane >= thr, sh, 0.0) if thr > 0 else sh)
    wsum = x[POOL_HALO:, :]
    row = lax.broadcasted_iota(jnp.int32, (tq, POOL_WIDTH), 0)
    lane2 = lax.broadcasted_iota(jnp.int32, (tq, POOL_WIDTH), 1)
    wl = jnp.where(lane2 < POOL_GW, POOL_WINDOWS[0],
                   jnp.where(lane2 < 2 * POOL_GW, POOL_WINDOWS[1],
                             jnp.where(lane2 < 3 * POOL_GW, POOL_WINDOWS[2], POOL_WINDOWS[3])))
    pos = pos0 + i * tq + row
    cnt = jnp.minimum(pos + 1, wl).astype(F32)
    d = (wsum / cnt - p).astype(BF16)
    ys = [jnp.dot(d[:, g * POOL_GW:(g + 1) * POOL_GW], pw_ref[g], preferred_element_type=F32)
          for g in range(len(POOL_WINDOWS))]
    y_ref[0] = jnp.concatenate(ys, axis=1) * sc_ref[...]


def _pool(p, prev16, pool_w_bf, scale, *, tq, pos0):
    b, t, c = p.shape
    nq = t // tq
    hb = tq // POOL_HALO
    return pl.pallas_call(
        functools.partial(_pool_body, tq=tq, pos0=pos0),
        grid=(b, nq),
        in_specs=[pl.BlockSpec((1, POOL_HALO, c), lambda bi, i: (bi, 0, 0)),
                  pl.BlockSpec((1, POOL_HALO, c), lambda bi, i: (bi, jnp.maximum(i * hb - 1, 0), 0)),
                  pl.BlockSpec((1, tq, c), lambda bi, i: (bi, i, 0)),
                  pl.BlockSpec((len(POOL_WINDOWS), POOL_GW, POOL_GW), lambda bi, i: (0, 0, 0)),
                  pl.BlockSpec((1, c), lambda bi, i: (0, 0))],
        out_specs=pl.BlockSpec((1, tq, c), lambda bi, i: (bi, i, 0)),
        out_shape=jax.ShapeDtypeStruct((b, t, c), F32),
        compiler_params=_cparams(("arbitrary", "arbitrary")),
        name="pool",
    )(prev16, p, p, pool_w_bf, scale)


def _bucket_np(dist):
    n = np.maximum(dist, 0)
    max_exact = N_BUCKETS // 2
    nf = np.maximum(n, 1).astype(np.float32)
    large = max_exact + (np.log(nf / np.float32(max_exact)) / np.float32(math.log(REL_MAX_DIST / max_exact))
                         * np.float32(N_BUCKETS - max_exact)).astype(np.int32)
    large = np.minimum(large, N_BUCKETS - 1)
    return np.where(n < max_exact, n, large)


def _far_bucket(min_dist, max_dist):
    b = _bucket_np(np.arange(min_dist, max_dist + 1))
    assert (b == b[0]).all()
    return int(b[0])


def _moba_prompt_body(rb_ref, q_ref, kb_ref, vt_ref, km_ref, io_ref, ip_ref, o_ref,
                      sel_ref, qz_ref, m_ref, l_ref, acc_ref, bown_ref, bprev_ref, *, nb, far):
    i = pl.program_id(1)
    tq = MOBA_BLOCK

    @pl.when((pl.program_id(0) == 0) & (i == 0))
    def _():
        for h in range(N_HEADS):
            bown_ref[h] = jnp.full((tq, tq), NEG_INF, F32)
            bprev_ref[h] = jnp.zeros((tq, tq), F32)
        io = io_ref[...]
        ip = ip_ref[...]

        def fill(bkt, carry):
            mo = io == bkt
            mp = ip == bkt
            for h in range(N_HEADS):
                val = rb_ref[bkt, h] - rb_ref[far, h]
                bown_ref[h] = jnp.where(mo, val, bown_ref[h])
                bprev_ref[h] = jnp.where(mp, val, bprev_ref[h])
            return carry

        lax.fori_loop(0, N_BUCKETS, fill, 0)

    blk = lax.broadcasted_iota(jnp.int32, (nb, tq), 0)
    valid = blk < i
    lane = lax.broadcasted_iota(jnp.int32, (tq, LANES), 1)
    for hp in range(N_HEADS // 2):
        qf = q_ref[0, :, LANES * hp:LANES * (hp + 1)]
        km = km_ref[0, :, LANES * hp:LANES * (hp + 1)].astype(BF16)
        for hh in range(2):
            h = 2 * hp + hh
            qz = jnp.where((lane >= HEAD_DIM * hh) & (lane < HEAD_DIM * (hh + 1)), qf, 0.0).astype(BF16)
            qz_ref[h] = qz
            sel = _top_mask(_nt_dot(km, qz), valid, MOBA_TOPK, nb)
            selb = jnp.where(sel, 0.0, NEG_INF)
            for jj in range(nb):
                sel_ref[h, jj] = selb[jj:jj + 1, :]
            m_ref[h] = jnp.full((1, tq), -1e30, F32)
            l_ref[h] = jnp.zeros((1, tq), F32)
            acc_ref[h] = jnp.zeros((HEAD_DIM, tq), F32)

    def key_block(j, bias_ref, masked):
        sts = [_nt_dot(kb_ref[0, j, :, LANES * (h // 2):LANES * (h // 2 + 1)], qz_ref[h])
               for h in range(N_HEADS)]
        ps, alphas = [], []
        for h in range(N_HEADS):
            st = sts[h] if bias_ref is None else sts[h] + bias_ref[h]
            cm = jnp.max(st, axis=0, keepdims=True)
            m = m_ref[h]
            if masked:
                mb = sel_ref[h, j]
                m_new = jnp.maximum(m, cm + mb)
                p = jnp.exp(st + (mb - m_new))
            else:
                m_new = jnp.maximum(m, cm)
                p = jnp.exp(st - m_new)
            alpha = jnp.exp(m - m_new)
            l_ref[h] = alpha * l_ref[h] + jnp.sum(p, axis=0, keepdims=True)
            m_ref[h] = m_new
            ps.append(p.astype(BF16))
            alphas.append(alpha)
        for h in range(N_HEADS):
            vt = vt_ref[0, j, HEAD_DIM * h:HEAD_DIM * (h + 1), :]
            acc_ref[h] = alphas[h] * acc_ref[h] + jnp.dot(vt, ps[h], preferred_element_type=F32)

    def far_body(j, carry):
        key_block(j, None, True)
        return carry

    lax.fori_loop(0, i - 1, far_body, 0)
    jp = jnp.maximum(i - 1, 0)
    key_block(jp, bprev_ref, True)
    key_block(i, bown_ref, False)
    for hp in range(N_HEADS // 2):
        pair = jnp.concatenate([acc_ref[2 * hp] / l_ref[2 * hp], acc_ref[2 * hp + 1] / l_ref[2 * hp + 1]],
                               axis=0)
        o_ref[0, :, LANES * hp:LANES * (hp + 1)] = pair.T


def _moba_prompt(q, kb, vt, km, rel_bias):
    b, t, _ = q.shape
    nb = t // MOBA_BLOCK
    far = _far_bucket(MOBA_BLOCK + 1, t)
    c = np.arange(MOBA_BLOCK)
    d_own = c[None, :] - c[:, None]
    idx_own = jnp.asarray(np.where(d_own >= 0, _bucket_np(d_own), -1).astype(np.int32))
    idx_prev = jnp.asarray(_bucket_np(d_own + MOBA_BLOCK).astype(np.int32))
    once = pl.Buffered(1)
    tile = pl.BlockSpec((MOBA_BLOCK, MOBA_BLOCK), lambda bi, i: (0, 0), pipeline_mode=once)
    return pl.pallas_call(
        functools.partial(_moba_prompt_body, nb=nb, far=far),
        grid=(b, nb),
        in_specs=[pl.BlockSpec(memory_space=pltpu.SMEM),
                  pl.BlockSpec((1, MOBA_BLOCK, ATT_WIDTH), lambda bi, i: (bi, i, 0)),
                  pl.BlockSpec((1, nb, MOBA_BLOCK, ATT_WIDTH), lambda bi, i: (bi, 0, 0, 0), pipeline_mode=once),
                  pl.BlockSpec((1, nb, ATT_WIDTH, MOBA_BLOCK), lambda bi, i: (bi, 0, 0, 0), pipeline_mode=once),
                  pl.BlockSpec((1, nb, ATT_WIDTH), lambda bi, i: (bi, 0, 0)),
                  tile, tile],
        out_specs=pl.BlockSpec((1, MOBA_BLOCK, ATT_WIDTH), lambda bi, i: (bi, i, 0)),
        out_shape=jax.ShapeDtypeStruct((b, t, ATT_WIDTH), F32),
        scratch_shapes=[pltpu.VMEM((N_HEADS, nb, 1, MOBA_BLOCK), F32),
                        pltpu.VMEM((N_HEADS, MOBA_BLOCK, LANES), BF16),
                        pltpu.VMEM((N_HEADS, 1, MOBA_BLOCK), F32),
                        pltpu.VMEM((N_HEADS, 1, MOBA_BLOCK), F32),
                        pltpu.VMEM((N_HEADS, HEAD_DIM, MOBA_BLOCK), F32),
                        pltpu.VMEM((N_HEADS, MOBA_BLOCK, MOBA_BLOCK), F32),
                        pltpu.VMEM((N_HEADS, MOBA_BLOCK, MOBA_BLOCK), F32)],
        compiler_params=_cparams(("arbitrary", "arbitrary")),
        name="moba_prompt",
    )(rel_bias, q, kb, vt, km, idx_own, idx_prev)


SAMPLE_PAGES_PER_STEP = 16
PAGES_PER_BLOCK = MOBA_BLOCK // PAGE_SIZE


def _moba_sample_body(pt_ref, qbd_ref, knew_ref, vnew_ref, cfar_ref, blast_ref, bown_ref, *rest,
                      nbs, ts):
    npg = SAMPLE_PAGES_PER_STEP
    ppb = PAGES_PER_BLOCK
    k_refs = rest[:npg]
    v_refs = rest[npg:2 * npg]
    o_ref = rest[2 * npg]
    gs_sc, m_sc, l_sc, o_sc = rest[2 * npg + 1:]
    s = pl.program_id(1)
    nsteps = pl.num_programs(1)
    bps = npg // ppb
    nr = ts * N_HEADS
    qb = qbd_ref[0].astype(BF16)
    lane0 = lax.broadcasted_iota(jnp.int32, (ATT_WIDTH, LANES), 1) == 0
    sts = []
    for jj in range(bps):
        kts = [k_refs[ppb * jj + u][...].reshape(ATT_WIDTH, PAGE_SIZE) for u in range(ppb)]
        ksum = functools.reduce(lambda a, b: a + b, [jnp.sum(kt, axis=1, keepdims=True) for kt in kts])
        kmean = ksum * (1.0 / MOBA_BLOCK)
        kmcol = jnp.where(lane0, kmean, 0.0).astype(BF16)
        w = jnp.concatenate([kt.astype(BF16) for kt in kts] + [kmcol], axis=1)
        sts.append(jnp.dot(qb, w, preferred_element_type=F32))
    ps = []
    for jj in range(bps):
        j = s * bps + jj
        gs_sc[j] = sts[jj][:, MOBA_BLOCK:MOBA_BLOCK + 1]
        bias = jnp.where(j == nbs - 1, blast_ref[...], cfar_ref[...])
        st = sts[jj][:, :MOBA_BLOCK] + bias
        m = jnp.max(st, axis=1, keepdims=True)
        p = jnp.exp(st - m)
        m_sc[j] = m
        l_sc[j] = jnp.sum(p, axis=1, keepdims=True)
        ps.append(p.astype(BF16))
    for jj in range(bps):
        o = None
        for u in range(ppb):
            vt = v_refs[ppb * jj + u][...].reshape(ATT_WIDTH, PAGE_SIZE).astype(BF16)
            part = _nt_dot(ps[jj][:, PAGE_SIZE * u:PAGE_SIZE * (u + 1)], vt)
            o = part if o is None else o + part
        o_sc[s * bps + jj] = o

    @pl.when(s == nsteps - 1)
    def _():
        qf = qb.astype(F32)
        gs = gs_sc[...]
        sel = _top_mask(gs, jnp.ones(gs.shape, jnp.bool_), MOBA_TOPK, nbs)
        knew = knew_ref[0].astype(BF16).astype(F32)
        vnew = vnew_ref[0].astype(BF16).astype(F32)
        s_own = [jnp.sum(qf * knew[t:t + 1, :], axis=1, keepdims=True) + bown_ref[:, t:t + 1]
                 for t in range(ts)]
        m_all = m_sc[...]
        mtot = jnp.max(jnp.where(sel, m_all, NEG_INF), axis=0)
        for t in range(ts):
            mtot = jnp.maximum(mtot, s_own[t])
        w = jnp.where(sel, jnp.exp(m_all - mtot[None]), 0.0)
        ltot = jnp.sum(w * l_sc[...], axis=0)
        gs_sc[...] = w

        def merge(j, acc):
            return acc + gs_sc[j] * o_sc[j]

        otot = lax.fori_loop(0, nbs, merge, jnp.zeros((nr, ATT_WIDTH), F32))
        for t in range(ts):
            pt = jnp.exp(s_own[t] - mtot)
            ltot = ltot + pt
            otot = otot + pt.astype(BF16).astype(F32) * vnew[t:t + 1, :]
        out = otot / ltot
        row = lax.broadcasted_iota(jnp.int32, out.shape, 0)
        lane = lax.broadcasted_iota(jnp.int32, out.shape, 1)
        out = jnp.where((lane // HEAD_DIM) == (row % N_HEADS), out, 0.0)
        pieces = [jnp.sum(out[N_HEADS * t:N_HEADS * (t + 1), :], axis=0, keepdims=True) for t in range(ts)]
        pieces.append(jnp.zeros((8 - ts, ATT_WIDTH), F32))
        o_ref[0] = jnp.concatenate(pieces, axis=0)


def _moba_sample(page_table, qbd, knew8, vnew8, cfar_rows, b_last, b_own, cache_kt, cache_vt, layer):
    db, nr, _ = qbd.shape
    ts = nr // N_HEADS
    n_pages = page_table.shape[1]
    nbs = n_pages // PAGES_PER_BLOCK
    npg = SAMPLE_PAGES_PER_STEP
    nsteps = n_pages // npg

    def page_spec(u):
        return pl.BlockSpec((None, None, N_HEADS, HEAD_DIM, PAGE_SIZE),
                            lambda b, s, pt: (pt[b, s * npg + u], layer, 0, 0, 0))

    full2 = lambda shp: pl.BlockSpec(shp, lambda b, s, pt: (0, 0))
    per_b = lambda r: pl.BlockSpec((1, r, ATT_WIDTH), lambda b, s, pt: (b, 0, 0))
    grid_spec = pltpu.PrefetchScalarGridSpec(
        num_scalar_prefetch=1,
        grid=(db, nsteps),
        in_specs=[per_b(nr), per_b(8), per_b(8),
                  full2((nr, 1)), full2((nr, MOBA_BLOCK)), full2((nr, 8))]
                 + [page_spec(u) for u in range(npg)] + [page_spec(u) for u in range(npg)],
        out_specs=per_b(8),
        scratch_shapes=[pltpu.VMEM((nbs, nr, 1), F32), pltpu.VMEM((nbs, nr, 1), F32),
                        pltpu.VMEM((nbs, nr, 1), F32), pltpu.VMEM((nbs, nr, ATT_WIDTH), F32)],
    )
    return pl.pallas_call(
        functools.partial(_moba_sample_body, nbs=nbs, ts=ts),
        grid_spec=grid_spec,
        out_shape=jax.ShapeDtypeStruct((db, 8, ATT_WIDTH), F32),
        compiler_params=_cparams(("arbitrary", "arbitrary")),
        name="moba_sample",
    )(page_table, qbd, knew8, vnew8, cfar_rows, b_last, b_own,
      *([cache_kt] * npg), *([cache_vt] * npg))


def _layer_norm(h, g, b):
    mu = jnp.mean(h, axis=-1, keepdims=True)
    c = h - mu
    var = jnp.mean(c * c, axis=-1, keepdims=True)
    return c * lax.rsqrt(var + LN_EPS) * g + b


def _argmax_first(vals):
    best, idx = vals[0], jnp.zeros(vals[0].shape, jnp.int32)
    for k in range(1, len(vals)):
        upd = vals[k] > best
        idx = jnp.where(upd, k, idx)
        best = jnp.where(upd, vals[k], best)
    return best, idx


def _route_rows(logit_rows):
    mx = functools.reduce(jnp.maximum, logit_rows)
    ex = [jnp.exp(r - mx) for r in logit_rows]
    tot = functools.reduce(lambda a, b: a + b, ex)
    probs = [e / tot for e in ex]
    scores = []
    for g in range(N_GROUPS):
        a, b, c, d = probs[EXPERTS_PER_GROUP * g:EXPERTS_PER_GROUP * (g + 1)]
        s1, t1 = jnp.maximum(a, b), jnp.minimum(a, b)
        s2, t2 = jnp.maximum(c, d), jnp.minimum(c, d)
        scores.append(jnp.maximum(s1, s2) + jnp.maximum(jnp.minimum(s1, s2), jnp.maximum(t1, t2)))
    _, gi = _argmax_first(scores)
    ing = []
    for j in range(EXPERTS_PER_GROUP):
        v = probs[j]
        for g in range(1, N_GROUPS):
            v = jnp.where(gi == g, probs[EXPERTS_PER_GROUP * g + j], v)
        ing.append(v)
    w1, i1 = _argmax_first(ing)
    w2, i2 = _argmax_first([jnp.where(i1 == j, -1.0, ing[j]) for j in range(EXPERTS_PER_GROUP)])
    den = w1 + w2
    e1 = (gi * EXPERTS_PER_GROUP + i1).astype(F32)
    e2 = (gi * EXPERTS_PER_GROUP + i2).astype(F32)
    return e1, e2, w1 / den, w2 / den


def _merge_body(x_ref, yp_ref, o_ref, gp_ref, ga_ref, wpo_ref, wao_ref, wo_ref, g_ref, b_ref,
                wr_ref, br_ref, x1_ref, rt_ref, *, tm):
    a = jnp.dot(yp_ref[...].astype(BF16), wpo_ref[...], preferred_element_type=F32)
    bb = jnp.dot(o_ref[...].astype(BF16), wao_ref[...], preferred_element_type=F32)
    merged = gp_ref[...] * a + ga_ref[...] * bb
    mix = jnp.dot(merged.astype(BF16), wo_ref[...], preferred_element_type=F32)
    x1 = _layer_norm(DEEPNORM_ALPHA * x_ref[...] + mix, g_ref[...], b_ref[...])
    x1_ref[...] = x1
    lg = jnp.dot(x1.astype(BF16), wr_ref[...], preferred_element_type=F32) + br_ref[...]
    lgt = lg.T
    e1, e2, g1, g2 = _route_rows([lgt[e:e + 1, :] for e in range(N_EXPERTS)])
    r = lax.broadcasted_iota(jnp.int32, (8, tm), 0)
    rt_ref[0] = jnp.where(r == 0, e1, jnp.where(r == 1, e2, jnp.where(r == 2, g1, jnp.where(r == 3, g2, 0.0))))


def _merge(x, yp, o, gp, ga, wpo, wao, wo, g, b, wr, br, *, tm):
    m = x.shape[0]
    row = lambda n: pl.BlockSpec((tm, n), lambda i: (i, 0))
    full = lambda r, c: pl.BlockSpec((r, c), lambda i: (0, 0))
    return pl.pallas_call(
        functools.partial(_merge_body, tm=tm),
        grid=(m // tm,),
        in_specs=[row(D_MODEL), row(POOL_WIDTH), row(ATT_WIDTH), row(D_MODEL), row(D_MODEL),
                  full(POOL_WIDTH, D_MODEL), full(ATT_WIDTH, D_MODEL), full(D_MODEL, D_MODEL),
                  full(1, D_MODEL), full(1, D_MODEL), full(D_MODEL, ROUTER_PAD), full(1, ROUTER_PAD)],
        out_specs=[row(D_MODEL), pl.BlockSpec((1, 8, tm), lambda i: (i, 0, 0))],
        out_shape=[jax.ShapeDtypeStruct((m, D_MODEL), F32), jax.ShapeDtypeStruct((m // tm, 8, tm), F32)],
        compiler_params=_cparams(("arbitrary",)),
        name="merge",
    )(x, yp, o, gp, ga, wpo, wao, wo, g, b, wr, br)


def _unpack_route(rt):
    n = rt.shape[0] * rt.shape[2]
    cols = rt[:, :4, :].transpose(0, 2, 1).reshape(n, 4)
    return cols[:, :2].astype(jnp.int32), cols[:, 2:]


def _experts_body(be_ref, nv_ref, tok_ref, tok_next_ref, slot_ref, h_hbm, wg_ref, wu_ref, wd_ref, y_hbm,
                  xbuf, ybuf, sem_in, sem_out, wg_sc, wu_sc, wd_sc):
    i = pl.program_id(0)
    n = pl.num_programs(0)
    cur = i % 2

    def gather_copy(tok, r, b):
        return pltpu.make_async_copy(h_hbm.at[pl.ds(tok, 1), :], xbuf.at[b, pl.ds(r, 1), :], sem_in.at[b])

    def scatter_copy(dst, r, b):
        return pltpu.make_async_copy(ybuf.at[b, pl.ds(r, 1), :], y_hbm.at[pl.ds(dst, 1), :], sem_out.at[b])

    def start_gather(idx_ref, b):
        def body(r, c):
            gather_copy(idx_ref[0, r], r, b).start()
            return c
        lax.fori_loop(0, EXPERT_ROWS, body, 0, unroll=8)

    def drain_scatter(count, b):
        def body(r, c):
            scatter_copy(0, 0, b).wait()
            return c
        lax.fori_loop(0, count, body, 0)

    @pl.when(i == 0)
    def _():
        start_gather(tok_ref, 0)

    @pl.when(i + 1 < n)
    def _():
        start_gather(tok_next_ref, 1 - cur)

    @pl.when((i == 0) | (be_ref[i] != be_ref[jnp.maximum(i - 1, 0)]))
    def _():
        wg_sc[...] = wg_ref[0].astype(BF16)
        wu_sc[...] = wu_ref[0].astype(BF16)
        wd_sc[...] = wd_ref[0].astype(BF16)

    def wait_row(r, c):
        gather_copy(0, 0, cur).wait()
        return c
    lax.fori_loop(0, EXPERT_ROWS, wait_row, 0, unroll=8)

    @pl.when(i >= 2)
    def _():
        drain_scatter(nv_ref[jnp.maximum(i - 2, 0)], cur)

    xb = xbuf[cur].astype(BF16)
    h1 = jnp.dot(xb, wg_sc[...], preferred_element_type=F32)
    h2 = jnp.dot(xb, wu_sc[...], preferred_element_type=F32)
    act = (h1 * jax.nn.sigmoid(h1) * h2).astype(BF16)
    ybuf[cur] = jnp.dot(act, wd_sc[...], preferred_element_type=F32)

    def put_row(r, c):
        scatter_copy(slot_ref[0, r], r, cur).start()
        return c
    lax.fori_loop(0, nv_ref[i], put_row, 0)

    @pl.when(i == n - 1)
    def _():
        drain_scatter(nv_ref[i], cur)

        @pl.when(i >= 1)
        def _():
            drain_scatter(nv_ref[jnp.maximum(i - 1, 0)], 1 - cur)


def _experts(blk_expert, blk_valid, row_tok, row_slot, h, wg, wu, wd, layer):
    n_blk = row_tok.shape[0]
    n_slots = h.shape[0] * TOP_K
    idx_blk = lambda f: pl.BlockSpec((None, 1, EXPERT_ROWS), f, memory_space=pltpu.SMEM)
    row_tok = row_tok.reshape(n_blk, 1, EXPERT_ROWS)
    row_slot = row_slot.reshape(n_blk, 1, EXPERT_ROWS)
    grid_spec = pltpu.PrefetchScalarGridSpec(
        num_scalar_prefetch=2,
        grid=(n_blk,),
        in_specs=[idx_blk(lambda i, be, nv: (i, 0, 0)),
                  idx_blk(lambda i, be, nv: (jnp.minimum(i + 1, n_blk - 1), 0, 0)),
                  idx_blk(lambda i, be, nv: (i, 0, 0)),
                  pl.BlockSpec(memory_space=pl.ANY),
                  pl.BlockSpec((None, 1, D_MODEL, D_EXPERT), lambda i, be, nv: (layer, be[i], 0, 0)),
                  pl.BlockSpec((None, 1, D_MODEL, D_EXPERT), lambda i, be, nv: (layer, be[i], 0, 0)),
                  pl.BlockSpec((None, 1, D_EXPERT, D_MODEL), lambda i, be, nv: (layer, be[i], 0, 0))],
        out_specs=pl.BlockSpec(memory_space=pl.ANY),
        scratch_shapes=[pltpu.VMEM((2, EXPERT_ROWS, D_MODEL), F32), pltpu.VMEM((2, EXPERT_ROWS, D_MODEL), F32),
                        pltpu.SemaphoreType.DMA((2,)), pltpu.SemaphoreType.DMA((2,)),
                        pltpu.VMEM((D_MODEL, D_EXPERT), BF16), pltpu.VMEM((D_MODEL, D_EXPERT), BF16),
                        pltpu.VMEM((D_EXPERT, D_MODEL), BF16)],
    )
    return pl.pallas_call(
        _experts_body,
        grid_spec=grid_spec,
        out_shape=jax.ShapeDtypeStruct((n_slots, D_MODEL), F32),
        compiler_params=_cparams(("arbitrary",)),
        name="experts",
    )(blk_expert, blk_valid, row_tok, row_tok, row_slot, h, wg, wu, wd)


def _ln2_body(xp_ref, xs_ref, y_ref, gt_ref, g_ref, b_ref, op_ref, os_ref, *, n_ptiles):
    i = pl.program_id(0)
    gt = gt_ref[...]
    f = gt[:, 0:1] * y_ref[:, :D_MODEL] + gt[:, 1:2] * y_ref[:, D_MODEL:]

    @pl.when(i < n_ptiles)
    def _():
        op_ref[...] = _layer_norm(DEEPNORM_ALPHA * xp_ref[...] + f, g_ref[...], b_ref[...])

    @pl.when(i >= n_ptiles)
    def _():
        os_ref[...] = _layer_norm(DEEPNORM_ALPHA * xs_ref[...] + f, g_ref[...], b_ref[...])


def _ln2(x_p, x_s, y, gate, g, b, *, tm):
    n_p, n_s = x_p.shape[0], x_s.shape[0]
    n_ptiles, n_stiles = n_p // tm, n_s // tm
    assert n_p % tm == 0 and n_s % tm == 0
    row = lambda n: pl.BlockSpec((tm, n), lambda i: (i, 0))
    p_row = pl.BlockSpec((tm, D_MODEL), lambda i: (jnp.minimum(i, n_ptiles - 1), 0))
    s_row = pl.BlockSpec((tm, D_MODEL), lambda i: (jnp.maximum(i - n_ptiles, 0), 0))
    vec = pl.BlockSpec((1, D_MODEL), lambda i: (0, 0))
    return pl.pallas_call(
        functools.partial(_ln2_body, n_ptiles=n_ptiles),
        grid=(n_ptiles + n_stiles,),
        in_specs=[p_row, s_row, row(TOP_K * D_MODEL), row(TOP_K), vec, vec],
        out_specs=[p_row, s_row],
        out_shape=[jax.ShapeDtypeStruct((n_p, D_MODEL), F32), jax.ShapeDtypeStruct((n_s, D_MODEL), F32)],
        compiler_params=_cparams(("arbitrary",)),
        name="ln2",
    )(x_p, x_s, y, gate, g, b)


def _moe(h, expert, wg, wu, wd, layer):
    n = h.shape[0]
    s = n * TOP_K
    e_flat = expert.reshape(-1)
    onehot = (e_flat[:, None] == jnp.arange(N_EXPERTS, dtype=jnp.int32)[None, :]).astype(jnp.int32)
    csum = jnp.cumsum(onehot, axis=0)
    counts = csum[-1]
    pos_in = jnp.take_along_axis(csum, e_flat[:, None], axis=1)[:, 0] - 1
    padded = (counts + EXPERT_ROWS - 1) // EXPERT_ROWS * EXPERT_ROWS
    pad_end = jnp.cumsum(padded)
    pad_start = pad_end - padded
    dest = pad_start[e_flat] + pos_in
    n_blk = (s + N_EXPERTS * (EXPERT_ROWS - 1) + EXPERT_ROWS - 1) // EXPERT_ROWS
    row_slot = jnp.zeros((n_blk * EXPERT_ROWS,), jnp.int32).at[dest].set(jnp.arange(s, dtype=jnp.int32))
    row_slot = row_slot.reshape(n_blk, EXPERT_ROWS)
    blk_start = jnp.arange(n_blk, dtype=jnp.int32) * EXPERT_ROWS
    blk_expert = jnp.minimum(jnp.sum((pad_end[None, :] <= blk_start[:, None]).astype(jnp.int32), axis=1),
                             N_EXPERTS - 1)
    blk_valid = jnp.clip(pad_start[blk_expert] + counts[blk_expert] - blk_start, 0, EXPERT_ROWS)
    ys = _experts(blk_expert, blk_valid, row_slot // TOP_K, row_slot, h, wg, wu, wd, layer)
    return ys.reshape(n, TOP_K * D_MODEL)


def kernel(x_prompt, x_sample, cache_k, cache_v, state_pool, page_table, rel_bias, w_router, b_router,
           w_in, pool_w, pool_scale, w_pool_out, w_attn_out, w_o, ln1_g, ln1_b, w_gate, w_up, w_down,
           ln2_g, ln2_b):
    bsz, seq, _ = x_prompt.shape
    db, ts, _ = x_sample.shape
    past_len = page_table.shape[1] * PAGE_SIZE
    nb = seq // MOBA_BLOCK
    nbs = past_len // MOBA_BLOCK
    assert seq % MOBA_BLOCK == 0 and past_len % MOBA_BLOCK == 0 and ts <= 8
    np_rows = bsz * seq
    ns_rows = db * ts

    far_s = _far_bucket(MOBA_BLOCK + 1, past_len + ts)
    c = np.arange(MOBA_BLOCK)
    t_idx = np.repeat(np.arange(ts), N_HEADS)
    h_idx = np.tile(np.arange(N_HEADS), ts)
    d_last = MOBA_BLOCK + t_idx[:, None] - c[None, :]
    b_last_s = rel_bias[_bucket_np(d_last), h_idx[:, None]]
    tn = np.arange(8)
    d_new = t_idx[:, None] - tn[None, :]
    b_own_s = jnp.where((d_new >= 0) & (tn[None, :] < ts),
                        rel_bias[_bucket_np(np.maximum(d_new, 0)), h_idx[:, None]], NEG_INF)
    cfar_s = rel_bias[far_s][h_idx][:, None]
    head_mask = jnp.asarray((np.arange(ATT_WIDTH)[None, :] // HEAD_DIM) == h_idx[:, None])
    cache_kt = jnp.transpose(cache_k, (0, 2, 3, 4, 1))
    cache_vt = jnp.transpose(cache_v, (0, 2, 3, 4, 1))

    wr_pad = jnp.zeros((D_MODEL, ROUTER_PAD), BF16).at[:, :N_EXPERTS].set(w_router.astype(BF16))
    br_pad = jnp.zeros((1, ROUTER_PAD), F32).at[0, :N_EXPERTS].set(b_router)

    xp = x_prompt.reshape(np_rows, D_MODEL)
    xs = x_sample.reshape(ns_rows, D_MODEL)
    kp, vp, pp, ksm, vsm, psm = [], [], [], [], [], []
    for l in range(DEPTH):
        w_in_bf = w_in[l].astype(BF16)
        pool_w_bf = pool_w[l].astype(BF16)
        scale = pool_scale[l][None, :]
        wpo = w_pool_out[l].astype(BF16)
        wao = w_attn_out[l].astype(BF16)
        wo = w_o[l].astype(BF16)
        g1, b1 = ln1_g[l][None, :], ln1_b[l][None, :]
        g2, b2 = ln2_g[l][None, :], ln2_b[l][None, :]

        p_p, q_p, k_p, v_p, gp_p, ga_p, kb_p, vt_p, km_p = _inproj(xp, w_in_bf, tm=MOBA_BLOCK, attn_layouts=True,
                                                                   seq=seq)
        prev0 = jnp.zeros((bsz, POOL_HALO, POOL_WIDTH), F32)
        yp_p = _pool(p_p.reshape(bsz, seq, POOL_WIDTH), prev0, pool_w_bf, scale, tq=MOBA_BLOCK, pos0=0)
        o_p = _moba_prompt(q_p.reshape(bsz, seq, ATT_WIDTH),
                           kb_p.reshape(bsz, nb, MOBA_BLOCK, ATT_WIDTH),
                           vt_p.reshape(bsz, nb, ATT_WIDTH, MOBA_BLOCK),
                           km_p.reshape(bsz, nb, ATT_WIDTH), rel_bias)
        x1_p, rt_p = _merge(xp, yp_p.reshape(np_rows, POOL_WIDTH), o_p.reshape(np_rows, ATT_WIDTH),
                            gp_p, ga_p, wpo, wao, wo, g1, b1, wr_pad, br_pad, tm=MOBA_BLOCK)

        p_s, q_s, k_s, v_s, gp_s, ga_s = _inproj(xs, w_in_bf, tm=ns_rows, attn_layouts=False)
        state = state_pool[:, l]
        prev_s = jnp.concatenate([jnp.zeros((db, 1, POOL_WIDTH), F32), state], axis=1)
        p_s3 = p_s.reshape(db, ts, POOL_WIDTH)
        p_s8 = jnp.pad(p_s3, ((0, 0), (0, 8 - ts), (0, 0)))
        yp_s = _pool(p_s8, prev_s, pool_w_bf, scale, tq=8, pos0=past_len)[:, :ts]
        qbd = jnp.where(head_mask[None], jnp.repeat(q_s.reshape(db, ts, ATT_WIDTH), N_HEADS, axis=1), 0.0)
        knew8 = jnp.pad(k_s.reshape(db, ts, ATT_WIDTH), ((0, 0), (0, 8 - ts), (0, 0)))
        vnew8 = jnp.pad(v_s.reshape(db, ts, ATT_WIDTH), ((0, 0), (0, 8 - ts), (0, 0)))
        o_s = _moba_sample(page_table, qbd, knew8, vnew8, cfar_s, b_last_s, b_own_s,
                           cache_kt, cache_vt, l)[:, :ts]
        x1_s, rt_s = _merge(xs, yp_s.reshape(ns_rows, POOL_WIDTH), o_s.reshape(ns_rows, ATT_WIDTH),
                            gp_s, ga_s, wpo, wao, wo, g1, b1, wr_pad, br_pad, tm=ns_rows)

        x1 = jnp.concatenate([x1_p, x1_s], axis=0)
        e_p, gt_p = _unpack_route(rt_p)
        e_s, gt_s = _unpack_route(rt_s)
        y = _moe(x1, jnp.concatenate([e_p, e_s], axis=0), w_gate, w_up, w_down, l)
        xp, xs = _ln2(x1_p, x1_s, y, jnp.concatenate([gt_p, gt_s], axis=0), g2, b2, tm=ROW_TILE)

        kp.append(k_p)
        vp.append(v_p)
        pp.append(p_p.reshape(bsz, seq, POOL_WIDTH)[:, seq - POOL_BUF:])
        ksm.append(k_s.reshape(db, ts, N_HEADS, HEAD_DIM))
        vsm.append(v_s.reshape(db, ts, N_HEADS, HEAD_DIM))
        psm.append(jnp.concatenate([state, p_s3], axis=1)[:, -POOL_BUF:])

    def rows_out(parts):
        a = jnp.stack(parts, axis=1).reshape(bsz, DEPTH, N_HEADS, HEAD_DIM, seq)
        return jnp.transpose(a, (0, 4, 1, 2, 3))

    return (xp.reshape(bsz, seq, D_MODEL), xs.reshape(db, ts, D_MODEL),
            rows_out(kp), rows_out(vp), jnp.stack(pp, axis=1),
            jnp.stack(ksm, axis=2), jnp.stack(vsm, axis=2), jnp.stack(psm, axis=1))
```

```python
import functools
import math

import numpy as np
import jax
import jax.numpy as jnp
from jax import lax
from jax.experimental import pallas as pl
from jax.experimental.pallas import tpu as pltpu

F32 = jnp.float32
BF16 = jnp.bfloat16
NEG_INF = float("-inf")

D_MODEL = 1024
DEPTH = 2
N_HEADS = 8
HEAD_DIM = 64
ATT_WIDTH = N_HEADS * HEAD_DIM
MOBA_BLOCK = 256
MOBA_TOPK = 3
N_BUCKETS = 32
REL_MAX_DIST = 128
POOL_WINDOWS = (2, 4, 8, 16)
POOL_WIDTH = 512
POOL_GW = 128
POOL_BUF = 15
POOL_HALO = 16
IN_WIDTH = POOL_WIDTH + 3 * ATT_WIDTH + 2 * D_MODEL
N_EXPERTS = 16
N_GROUPS = 4
EXPERTS_PER_GROUP = 4
TOP_K = 2
D_EXPERT = 512
EXPERT_ROWS = 512
ROW_TILE = 128
DEEPNORM_ALPHA = (2 * DEPTH) ** 0.25
LN_EPS = 1e-5
PAGE_SIZE = 128
ROUTER_PAD = 128
LANES = 128

VMEM_LIMIT = 52 * 1024 * 1024


def _cparams(sem):
    return pltpu.CompilerParams(dimension_semantics=sem, vmem_limit_bytes=VMEM_LIMIT)


def _nt_dot(a, b):
    return lax.dot_general(a, b, (((1,), (1,)), ((), ())), preferred_element_type=F32)


def _top_mask(scores, cand, k, n):
    idx = lax.broadcasted_iota(jnp.int32, scores.shape, 0)
    sel = jnp.zeros(scores.shape, jnp.bool_)
    for _ in range(k):
        c = cand & jnp.logical_not(sel)
        cur = jnp.where(c, scores, NEG_INF)
        mx = jnp.max(cur, axis=0, keepdims=True)
        first = jnp.min(jnp.where(c & (cur == mx), idx, n), axis=0, keepdims=True)
        sel = sel | (idx == first)
    return sel


def _inproj_body(x_ref, w_ref, p_ref, q_ref, k_ref, v_ref, gp_ref, ga_ref, *attn_refs, tm):
    xb = x_ref[...].astype(BF16)

    def seg(a, b):
        return jnp.dot(xb, w_ref[:, a:b], preferred_element_type=F32)

    c0 = POOL_WIDTH
    p_ref[...] = seg(0, c0)
    q_ref[...] = seg(c0, c0 + ATT_WIDTH) * (HEAD_DIM ** -0.5)
    k = seg(c0 + ATT_WIDTH, c0 + 2 * ATT_WIDTH)
    v = seg(c0 + 2 * ATT_WIDTH, c0 + 3 * ATT_WIDTH)
    g0 = c0 + 3 * ATT_WIDTH
    gp_ref[...] = jax.nn.sigmoid(seg(g0, g0 + D_MODEL))
    ga_ref[...] = jax.nn.sigmoid(seg(g0 + D_MODEL, g0 + 2 * D_MODEL))
    if attn_refs:
        kb_ref, vt_ref, km_ref = attn_refs
        k_ref[...] = k.T
        vt = v.T
        v_ref[...] = vt
        kb_ref[...] = k.astype(BF16)
        vt_ref[0] = vt.astype(BF16)
        km_ref[0] = jnp.sum(k, axis=0, keepdims=True) * (1.0 / tm)
    else:
        k_ref[...] = k
        v_ref[...] = v


def _inproj(x, w_bf, *, tm, attn_layouts, seq=None):
    m = x.shape[0]
    nt = m // tm
    row = lambda n: pl.BlockSpec((tm, n), lambda i: (i, 0))
    out_shape = [jax.ShapeDtypeStruct((m, POOL_WIDTH), F32),
                 jax.ShapeDtypeStruct((m, ATT_WIDTH), F32),
                 jax.ShapeDtypeStruct((m, ATT_WIDTH), F32),
                 jax.ShapeDtypeStruct((m, ATT_WIDTH), F32),
                 jax.ShapeDtypeStruct((m, D_MODEL), F32),
                 jax.ShapeDtypeStruct((m, D_MODEL), F32)]
    out_specs = [row(POOL_WIDTH), row(ATT_WIDTH), row(ATT_WIDTH), row(ATT_WIDTH),
                 row(D_MODEL), row(D_MODEL)]
    if attn_layouts:
        assert tm == MOBA_BLOCK
        tps = seq // tm
        for idx in (2, 3):
            out_shape[idx] = jax.ShapeDtypeStruct((m // seq, ATT_WIDTH, seq), F32)
            out_specs[idx] = pl.BlockSpec((None, ATT_WIDTH, tm), lambda i: (i // tps, 0, i % tps))
        out_shape += [jax.ShapeDtypeStruct((m, ATT_WIDTH), BF16),
                      jax.ShapeDtypeStruct((nt, ATT_WIDTH, tm), BF16),
                      jax.ShapeDtypeStruct((nt, 1, ATT_WIDTH), F32)]
        out_specs += [row(ATT_WIDTH),
                      pl.BlockSpec((1, ATT_WIDTH, tm), lambda i: (i, 0, 0)),
                      pl.BlockSpec((1, 1, ATT_WIDTH), lambda i: (i, 0, 0))]
    return pl.pallas_call(
        functools.partial(_inproj_body, tm=tm),
        grid=(nt,),
        in_specs=[row(D_MODEL), pl.BlockSpec((D_MODEL, IN_WIDTH), lambda i: (0, 0))],
        out_specs=out_specs,
        out_shape=out_shape,
        compiler_params=_cparams(("arbitrary",)),
        name="inproj",
    )(x, w_bf)


def _pool_body(prev_ref, halo_ref, p_ref, pw_ref, sc_ref, y_ref, *, tq, pos0):
    i = pl.program_id(1)
    p = p_ref[0]
    halo = jnp.where(i == 0, prev_ref[0], halo_ref[0])
    z = jnp.concatenate([halo, p], axis=0)
    lane = lax.broadcasted_iota(jnp.int32, z.shape, 1)
    x = z
    for s in (8, 4, 2, 1):
        thr = POOL_WIDTH - POOL_GW * {8: 1, 4: 2, 2: 3, 1: 4}[s]
        sh = pltpu.roll(x, s, 0)
        x = x + (jnp.where(lane >= thr, sh, 0.0) if thr > 0 else sh)
    wsum = x[POOL_HALO:, :]
    row = lax.broadcasted_iota(jnp.int32, (tq, POOL_WIDTH), 0)
    lane2 = lax.broadcasted_iota(jnp.int32, (tq, POOL_WIDTH), 1)
    wl = jnp.where(lane2 < POOL_GW, POOL_WINDOWS[0],
                   jnp.where(lane2 < 2 * POOL_GW, POOL_WINDOWS[1],
                             jnp.where(lane2 < 3 * POOL_GW, POOL_WINDOWS[2], POOL_WINDOWS[3])))
    pos = pos0 + i * tq + row
    cnt = jnp.minimum(pos + 1, wl).astype(F32)
    d = (wsum / cnt - p).astype(BF16)
    ys = [jnp.dot(d[:, g * POOL_GW:(g + 1) * POOL_GW], pw_ref[g], preferred_element_type=F32)
          for g in range(len(POOL_WINDOWS))]
    y_ref[0] = jnp.concatenate(ys, axis=1) * sc_ref[...]


def _pool(p, prev16, pool_w_bf, scale, *, tq, pos0):
    b, t, c = p.shape
    nq = t // tq
    hb = tq // POOL_HALO
    return pl.pallas_call(
        functools.partial(_pool_body, tq=tq, pos0=pos0),
        grid=(b, nq),
        in_specs=[pl.BlockSpec((1, POOL_HALO, c), lambda bi, i: (bi, 0, 0)),
                  pl.BlockSpec((1, POOL_HALO, c), lambda bi, i: (bi, jnp.maximum(i * hb - 1, 0), 0)),
                  pl.BlockSpec((1, tq, c), lambda bi, i: (bi, i, 0)),
                  pl.BlockSpec((len(POOL_WINDOWS), POOL_GW, POOL_GW), lambda bi, i: (0, 0, 0)),
                  pl.BlockSpec((1, c), lambda bi, i: (0, 0))],
        out_specs=pl.BlockSpec((1, tq, c), lambda bi, i: (bi, i, 0)),
        out_shape=jax.ShapeDtypeStruct((b, t, c), F32),
        compiler_params=_cparams(("arbitrary", "arbitrary")),
        name="pool",
    )(prev16, p, p, pool_w_bf, scale)


def _bucket_np(dist):
    n = np.maximum(dist, 0)
    max_exact = N_BUCKETS // 2
    nf = np.maximum(n, 1).astype(np.float32)
    large = max_exact + (np.log(nf / np.float32(max_exact)) / np.float32(math.log(REL_MAX_DIST / max_exact))
                         * np.float32(N_BUCKETS - max_exact)).astype(np.int32)
    large = np.minimum(large, N_BUCKETS - 1)
    return np.where(n < max_exact, n, large)


def _far_bucket(min_dist, max_dist):
    b = _bucket_np(np.arange(min_dist, max_dist + 1))
    assert (b == b[0]).all()
    return int(b[0])


def _moba_prompt_body(rb_ref, q_ref, kb_ref, vt_ref, km_ref, io_ref, ip_ref, o_ref,
                      sel_ref, qz_ref, m_ref, l_ref, acc_ref, bown_ref, bprev_ref, *, nb, far):
    i = pl.program_id(1)
    tq = MOBA_BLOCK

    @pl.when((pl.program_id(0) == 0) & (i == 0))
    def _():
        for h in range(N_HEADS):
            bown_ref[h] = jnp.full((tq, tq), NEG_INF, F32)
            bprev_ref[h] = jnp.zeros((tq, tq), F32)
        io = io_ref[...]
        ip = ip_ref[...]

        def fill(bkt, carry):
            mo = io == bkt
            mp = ip == bkt
            for h in range(N_HEADS):
                val = rb_ref[bkt, h] - rb_ref[far, h]
                bown_ref[h] = jnp.where(mo, val, bown_ref[h])
                bprev_ref[h] = jnp.where(mp, val, bprev_ref[h])
            return carry

        lax.fori_loop(0, N_BUCKETS, fill, 0)

    blk = lax.broadcasted_iota(jnp.int32, (nb, tq), 0)
    valid = blk < i
    lane = lax.broadcasted_iota(jnp.int32, (tq, LANES), 1)
    for hp in range(N_HEADS // 2):
        qf = q_ref[0, :, LANES * hp:LANES * (hp + 1)]
        km = km_ref[0, :, LANES * hp:LANES * (hp + 1)].astype(BF16)
        for hh in range(2):
            h = 2 * hp + hh
            qz = jnp.where((lane >= HEAD_DIM * hh) & (lane < HEAD_DIM * (hh + 1)), qf, 0.0).astype(BF16)
            qz_ref[h] = qz
            sel = _top_mask(_nt_dot(km, qz), valid, MOBA_TOPK, nb)
            selb = jnp.where(sel, 0.0, NEG_INF)
            for jj in range(nb):
                sel_ref[h, jj] = selb[jj:jj + 1, :]
            m_ref[h] = jnp.full((1, tq), -1e30, F32)
            l_ref[h] = jnp.zeros((1, tq), F32)
            acc_ref[h] = jnp.zeros((HEAD_DIM, tq), F32)

    def key_blocks(items):
        sts = [[_nt_dot(kb_ref[0, j, :, LANES * (h // 2):LANES * (h // 2 + 1)], qz_ref[h])
                for h in range(N_HEADS)] for j, _, _ in items]
        ps, alphas = [], []
        for h in range(N_HEADS):
            m = m_ref[h]
            m_new = m
            shifted = []
            for idx, (j, bias_ref, masked) in enumerate(items):
                st = sts[idx][h] if bias_ref is None else sts[idx][h] + bias_ref[h]
                cm = jnp.max(st, axis=0, keepdims=True)
                mb = sel_ref[h, j] if masked else None
                m_new = jnp.maximum(m_new, cm + mb if masked else cm)
                shifted.append((st, mb))
            alpha = jnp.exp(m - m_new)
            lsum = alpha * l_ref[h]
            ph = []
            for st, mb in shifted:
                p = jnp.exp(st - m_new) if mb is None else jnp.exp(st + (mb - m_new))
                lsum = lsum + jnp.sum(p, axis=0, keepdims=True)
                ph.append(p.astype(BF16))
            l_ref[h] = lsum
            m_ref[h] = m_new
            ps.append(ph)
            alphas.append(alpha)
        for h in range(N_HEADS):
            acc = alphas[h] * acc_ref[h]
            for idx, (j, _, _) in enumerate(items):
                vt = vt_ref[0, j, HEAD_DIM * h:HEAD_DIM * (h + 1), :]
                acc = acc + jnp.dot(vt, ps[h][idx], preferred_element_type=F32)
            acc_ref[h] = acc

    n_far = jnp.maximum(i - 1, 0)

    def far_body(jj, carry):
        key_blocks([(2 * jj, None, True), (2 * jj + 1, None, True)])
        return carry

    lax.fori_loop(0, n_far // 2, far_body, 0)

    @pl.when(n_far % 2 == 1)
    def _():
        key_blocks([(n_far - 1, None, True)])

    jp = jnp.maximum(i - 1, 0)
    key_blocks([(jp, bprev_ref, True), (i, bown_ref, False)])
    for hp in range(N_HEADS // 2):
        pair = jnp.concatenate([acc_ref[2 * hp] / l_ref[2 * hp], acc_ref[2 * hp + 1] / l_ref[2 * hp + 1]],
                               axis=0)
        o_ref[0, :, LANES * hp:LANES * (hp + 1)] = pair.T


def _moba_prompt(q, kb, vt, km, rel_bias):
    b, t, _ = q.shape
    nb = t // MOBA_BLOCK
    far = _far_bucket(MOBA_BLOCK + 1, t)
    c = np.arange(MOBA_BLOCK)
    d_own = c[None, :] - c[:, None]
    idx_own = jnp.asarray(np.where(d_own >= 0, _bucket_np(d_own), -1).astype(np.int32))
    idx_prev = jnp.asarray(_bucket_np(d_own + MOBA_BLOCK).astype(np.int32))
    once = pl.Buffered(1)
    tile = pl.BlockSpec((MOBA_BLOCK, MOBA_BLOCK), lambda bi, i: (0, 0), pipeline_mode=once)
    return pl.pallas_call(
        functools.partial(_moba_prompt_body, nb=nb, far=far),
        grid=(b, nb),
        in_specs=[pl.BlockSpec(memory_space=pltpu.SMEM),
                  pl.BlockSpec((1, MOBA_BLOCK, ATT_WIDTH), lambda bi, i: (bi, i, 0)),
                  pl.BlockSpec((1, nb, MOBA_BLOCK, ATT_WIDTH), lambda bi, i: (bi, 0, 0, 0), pipeline_mode=once),
                  pl.BlockSpec((1, nb, ATT_WIDTH, MOBA_BLOCK), lambda bi, i: (bi, 0, 0, 0), pipeline_mode=once),
                  pl.BlockSpec((1, nb, ATT_WIDTH), lambda bi, i: (bi, 0, 0)),
                  tile, tile],
        out_specs=pl.BlockSpec((1, MOBA_BLOCK, ATT_WIDTH), lambda bi, i: (bi, i, 0)),
        out_shape=jax.ShapeDtypeStruct((b, t, ATT_WIDTH), F32),
        scratch_shapes=[pltpu.VMEM((N_HEADS, nb, 1, MOBA_BLOCK), F32),
                        pltpu.VMEM((N_HEADS, MOBA_BLOCK, LANES), BF16),
                        pltpu.VMEM((N_HEADS, 1, MOBA_BLOCK), F32),
                        pltpu.VMEM((N_HEADS, 1, MOBA_BLOCK), F32),
                        pltpu.VMEM((N_HEADS, HEAD_DIM, MOBA_BLOCK), F32),
                        pltpu.VMEM((N_HEADS, MOBA_BLOCK, MOBA_BLOCK), F32),
                        pltpu.VMEM((N_HEADS, MOBA_BLOCK, MOBA_BLOCK), F32)],
        compiler_params=_cparams(("arbitrary", "arbitrary")),
        name="moba_prompt",
    )(rel_bias, q, kb, vt, km, idx_own, idx_prev)


SAMPLE_PAGES_PER_STEP = 16
PAGES_PER_BLOCK = MOBA_BLOCK // PAGE_SIZE


def _moba_sample_body(pt_ref, qbd_ref, knew_ref, vnew_ref, cfar_ref, blast_ref, bown_ref, *rest,
                      nbs, ts):
    npg = SAMPLE_PAGES_PER_STEP
    ppb = PAGES_PER_BLOCK
    k_refs = rest[:npg]
    v_refs = rest[npg:2 * npg]
    o_ref = rest[2 * npg]
    gs_sc, m_sc, l_sc, o_sc = rest[2 * npg + 1:]
    s = pl.program_id(1)
    nsteps = pl.num_programs(1)
    bps = npg // ppb
    nr = ts * N_HEADS
    qb = qbd_ref[0].astype(BF16)
    lane0 = lax.broadcasted_iota(jnp.int32, (ATT_WIDTH, LANES), 1) == 0
    sts = []
    for jj in range(bps):
        kts = [k_refs[ppb * jj + u][...].reshape(ATT_WIDTH, PAGE_SIZE) for u in range(ppb)]
        ksum = functools.reduce(lambda a, b: a + b, [jnp.sum(kt, axis=1, keepdims=True) for kt in kts])
        kmean = ksum * (1.0 / MOBA_BLOCK)
        kmcol = jnp.where(lane0, kmean, 0.0).astype(BF16)
        w = jnp.concatenate([kt.astype(BF16) for kt in kts] + [kmcol], axis=1)
        sts.append(jnp.dot(qb, w, preferred_element_type=F32))
    ps = []
    for jj in range(bps):
        j = s * bps + jj
        gs_sc[j] = sts[jj][:, MOBA_BLOCK:MOBA_BLOCK + 1]
        bias = jnp.where(j == nbs - 1, blast_ref[...], cfar_ref[...])
        st = sts[jj][:, :MOBA_BLOCK] + bias
        m = jnp.max(st, axis=1, keepdims=True)
        p = jnp.exp(st - m)
        m_sc[j] = m
        l_sc[j] = jnp.sum(p, axis=1, keepdims=True)
        ps.append(p.astype(BF16))
    for jj in range(bps):
        o = None
        for u in range(ppb):
            vt = v_refs[ppb * jj + u][...].reshape(ATT_WIDTH, PAGE_SIZE).astype(BF16)
            part = _nt_dot(ps[jj][:, PAGE_SIZE * u:PAGE_SIZE * (u + 1)], vt)
            o = part if o is None else o + part
        o_sc[s * bps + jj] = o

    @pl.when(s == nsteps - 1)
    def _():
        qf = qb.astype(F32)
        gs = gs_sc[...]
        sel = _top_mask(gs, jnp.ones(gs.shape, jnp.bool_), MOBA_TOPK, nbs)
        knew = knew_ref[0].astype(BF16).astype(F32)
        vnew = vnew_ref[0].astype(BF16).astype(F32)
        s_own = [jnp.sum(qf * knew[t:t + 1, :], axis=1, keepdims=True) + bown_ref[:, t:t + 1]
                 for t in range(ts)]
        m_all = m_sc[...]
        mtot = jnp.max(jnp.where(sel, m_all, NEG_INF), axis=0)
        for t in range(ts):
            mtot = jnp.maximum(mtot, s_own[t])
        w = jnp.where(sel, jnp.exp(m_all - mtot[None]), 0.0)
        ltot = jnp.sum(w * l_sc[...], axis=0)
        gs_sc[...] = w

        def merge(j, acc):
            return acc + gs_sc[j] * o_sc[j]

        otot = lax.fori_loop(0, nbs, merge, jnp.zeros((nr, ATT_WIDTH), F32))
        for t in range(ts):
            pt = jnp.exp(s_own[t] - mtot)
            ltot = ltot + pt
            otot = otot + pt.astype(BF16).astype(F32) * vnew[t:t + 1, :]
        out = otot / ltot
        row = lax.broadcasted_iota(jnp.int32, out.shape, 0)
        lane = lax.broadcasted_iota(jnp.int32, out.shape, 1)
        out = jnp.where((lane // HEAD_DIM) == (row % N_HEADS), out, 0.0)
        pieces = [jnp.sum(out[N_HEADS * t:N_HEADS * (t + 1), :], axis=0, keepdims=True) for t in range(ts)]
        pieces.append(jnp.zeros((8 - ts, ATT_WIDTH), F32))
        o_ref[0] = jnp.concatenate(pieces, axis=0)


def _moba_sample(page_table, qbd, knew8, vnew8, cfar_rows, b_last, b_own, cache_kt, cache_vt, layer):
    db, nr, _ = qbd.shape
    ts = nr // N_HEADS
    n_pages = page_table.shape[1]
    nbs = n_pages // PAGES_PER_BLOCK
    npg = SAMPLE_PAGES_PER_STEP
    nsteps = n_pages // npg

    def page_spec(u):
        return pl.BlockSpec((None, None, N_HEADS, HEAD_DIM, PAGE_SIZE),
                            lambda b, s, pt: (pt[b, s * npg + u], layer, 0, 0, 0))

    full2 = lambda shp: pl.BlockSpec(shp, lambda b, s, pt: (0, 0))
    per_b = lambda r: pl.BlockSpec((1, r, ATT_WIDTH), lambda b, s, pt: (b, 0, 0))
    grid_spec = pltpu.PrefetchScalarGridSpec(
        num_scalar_prefetch=1,
        grid=(db, nsteps),
        in_specs=[per_b(nr), per_b(8), per_b(8),
                  full2((nr, 1)), full2((nr, MOBA_BLOCK)), full2((nr, 8))]
                 + [page_spec(u) for u in range(npg)] + [page_spec(u) for u in range(npg)],
        out_specs=per_b(8),
        scratch_shapes=[pltpu.VMEM((nbs, nr, 1), F32), pltpu.VMEM((nbs, nr, 1), F32),
                        pltpu.VMEM((nbs, nr, 1), F32), pltpu.VMEM((nbs, nr, ATT_WIDTH), F32)],
    )
    return pl.pallas_call(
        functools.partial(_moba_sample_body, nbs=nbs, ts=ts),
        grid_spec=grid_spec,
        out_shape=jax.ShapeDtypeStruct((db, 8, ATT_WIDTH), F32),
        compiler_params=_cparams(("arbitrary", "arbitrary")),
        name="moba_sample",
    )(page_table, qbd, knew8, vnew8, cfar_rows, b_last, b_own,
      *([cache_kt] * npg), *([cache_vt] * npg))


def _layer_norm(h, g, b):
    mu = jnp.mean(h, axis=-1, keepdims=True)
    c = h - mu
    var = jnp.mean(c * c, axis=-1, keepdims=True)
    return c * lax.rsqrt(var + LN_EPS) * g + b


def _argmax_first(vals):
    best, idx = vals[0], jnp.zeros(vals[0].shape, jnp.int32)
    for k in range(1, len(vals)):
        upd = vals[k] > best
        idx = jnp.where(upd, k, idx)
        best = jnp.where(upd, vals[k], best)
    return best, idx


def _route_rows(logit_rows):
    mx = functools.reduce(jnp.maximum, logit_rows)
    ex = [jnp.exp(r - mx) for r in logit_rows]
    tot = functools.reduce(lambda a, b: a + b, ex)
    probs = [e / tot for e in ex]
    scores = []
    for g in range(N_GROUPS):
        a, b, c, d = probs[EXPERTS_PER_GROUP * g:EXPERTS_PER_GROUP * (g + 1)]
        s1, t1 = jnp.maximum(a, b), jnp.minimum(a, b)
        s2, t2 = jnp.maximum(c, d), jnp.minimum(c, d)
        scores.append(jnp.maximum(s1, s2) + jnp.maximum(jnp.minimum(s1, s2), jnp.maximum(t1, t2)))
    _, gi = _argmax_first(scores)
    ing = []
    for j in range(EXPERTS_PER_GROUP):
        v = probs[j]
        for g in range(1, N_GROUPS):
            v = jnp.where(gi == g, probs[EXPERTS_PER_GROUP * g + j], v)
        ing.append(v)
    w1, i1 = _argmax_first(ing)
    w2, i2 = _argmax_first([jnp.where(i1 == j, -1.0, ing[j]) for j in range(EXPERTS_PER_GROUP)])
    den = w1 + w2
    e1 = (gi * EXPERTS_PER_GROUP + i1).astype(F32)
    e2 = (gi * EXPERTS_PER_GROUP + i2).astype(F32)
    return e1, e2, w1 / den, w2 / den


def _merge_body(x_ref, yp_ref, o_ref, gp_ref, ga_ref, wpo_ref, wao_ref, wo_ref, g_ref, b_ref,
                wr_ref, br_ref, x1_ref, rt_ref, *, tm):
    a = jnp.dot(yp_ref[...].astype(BF16), wpo_ref[...], preferred_element_type=F32)
    bb = jnp.dot(o_ref[...].astype(BF16), wao_ref[...], preferred_element_type=F32)
    merged = gp_ref[...] * a + ga_ref[...] * bb
    mix = jnp.dot(merged.astype(BF16), wo_ref[...], preferred_element_type=F32)
    x1 = _layer_norm(DEEPNORM_ALPHA * x_ref[...] + mix, g_ref[...], b_ref[...])
    x1_ref[...] = x1
    lg = jnp.dot(x1.astype(BF16), wr_ref[...], preferred_element_type=F32) + br_ref[...]
    lgt = lg.T
    e1, e2, g1, g2 = _route_rows([lgt[e:e + 1, :] for e in range(N_EXPERTS)])
    r = lax.broadcasted_iota(jnp.int32, (8, tm), 0)
    rt_ref[0] = jnp.where(r == 0, e1, jnp.where(r == 1, e2, jnp.where(r == 2, g1, jnp.where(r == 3, g2, 0.0))))


def _merge(x, yp, o, gp, ga, wpo, wao, wo, g, b, wr, br, *, tm):
    m = x.shape[0]
    row = lambda n: pl.BlockSpec((tm, n), lambda i: (i, 0))
    full = lambda r, c: pl.BlockSpec((r, c), lambda i: (0, 0))
    return pl.pallas_call(
        functools.partial(_merge_body, tm=tm),
        grid=(m // tm,),
        in_specs=[row(D_MODEL), row(POOL_WIDTH), row(ATT_WIDTH), row(D_MODEL), row(D_MODEL),
                  full(POOL_WIDTH, D_MODEL), full(ATT_WIDTH, D_MODEL), full(D_MODEL, D_MODEL),
                  full(1, D_MODEL), full(1, D_MODEL), full(D_MODEL, ROUTER_PAD), full(1, ROUTER_PAD)],
        out_specs=[row(D_MODEL), pl.BlockSpec((1, 8, tm), lambda i: (i, 0, 0))],
        out_shape=[jax.ShapeDtypeStruct((m, D_MODEL), F32), jax.ShapeDtypeStruct((m // tm, 8, tm), F32)],
        compiler_params=_cparams(("arbitrary",)),
        name="merge",
    )(x, yp, o, gp, ga, wpo, wao, wo, g, b, wr, br)


def _unpack_route(rt):
    n = rt.shape[0] * rt.shape[2]
    cols = rt[:, :4, :].transpose(0, 2, 1).reshape(n, 4)
    return cols[:, :2].astype(jnp.int32), cols[:, 2:]


def _experts_body(be_ref, xs_ref, wg_ref, wu_ref, wd_ref, ys_ref, wg_sc, wu_sc, wd_sc):
    i = pl.program_id(0)

    @pl.when((i == 0) | (be_ref[i] != be_ref[jnp.maximum(i - 1, 0)]))
    def _():
        wg_sc[...] = wg_ref[0].astype(BF16)
        wu_sc[...] = wu_ref[0].astype(BF16)
        wd_sc[...] = wd_ref[0].astype(BF16)

    xb = xs_ref[...].astype(BF16)
    h1 = jnp.dot(xb, wg_sc[...], preferred_element_type=F32)
    h2 = jnp.dot(xb, wu_sc[...], preferred_element_type=F32)
    act = (h1 * jax.nn.sigmoid(h1) * h2).astype(BF16)
    ys_ref[...] = jnp.dot(act, wd_sc[...], preferred_element_type=F32)


def _experts(blk_expert, xs, wg, wu, wd, layer):
    n_rows = xs.shape[0]
    n_blk = n_rows // EXPERT_ROWS
    grid_spec = pltpu.PrefetchScalarGridSpec(
        num_scalar_prefetch=1,
        grid=(n_blk,),
        in_specs=[pl.BlockSpec((EXPERT_ROWS, D_MODEL), lambda i, be: (i, 0)),
                  pl.BlockSpec((None, 1, D_MODEL, D_EXPERT), lambda i, be: (layer, be[i], 0, 0)),
                  pl.BlockSpec((None, 1, D_MODEL, D_EXPERT), lambda i, be: (layer, be[i], 0, 0)),
                  pl.BlockSpec((None, 1, D_EXPERT, D_MODEL), lambda i, be: (layer, be[i], 0, 0))],
        out_specs=pl.BlockSpec((EXPERT_ROWS, D_MODEL), lambda i, be: (i, 0)),
        scratch_shapes=[pltpu.VMEM((D_MODEL, D_EXPERT), BF16), pltpu.VMEM((D_MODEL, D_EXPERT), BF16),
                        pltpu.VMEM((D_EXPERT, D_MODEL), BF16)],
    )
    return pl.pallas_call(
        _experts_body,
        grid_spec=grid_spec,
        out_shape=jax.ShapeDtypeStruct((n_rows, D_MODEL), F32),
        compiler_params=_cparams(("arbitrary",)),
        name="experts",
    )(blk_expert, xs, wg, wu, wd)


def _ln2_body(xp_ref, xs_ref, y0_ref, y1_ref, gt_ref, g_ref, b_ref, op_ref, os_ref, *, n_ptiles):
    i = pl.program_id(0)
    gt = gt_ref[...]
    f = gt[:, 0:1] * y0_ref[...] + gt[:, 1:2] * y1_ref[...]

    @pl.when(i < n_ptiles)
    def _():
        op_ref[...] = _layer_norm(DEEPNORM_ALPHA * xp_ref[...] + f, g_ref[...], b_ref[...])

    @pl.when(i >= n_ptiles)
    def _():
        os_ref[...] = _layer_norm(DEEPNORM_ALPHA * xs_ref[...] + f, g_ref[...], b_ref[...])


def _ln2(x_p, x_s, y0, y1, gate, g, b, *, tm):
    n_p, n_s = x_p.shape[0], x_s.shape[0]
    n_ptiles, n_stiles = n_p // tm, n_s // tm
    assert n_p % tm == 0 and n_s % tm == 0
    row = lambda n: pl.BlockSpec((tm, n), lambda i: (i, 0))
    p_row = pl.BlockSpec((tm, D_MODEL), lambda i: (jnp.minimum(i, n_ptiles - 1), 0))
    s_row = pl.BlockSpec((tm, D_MODEL), lambda i: (jnp.maximum(i - n_ptiles, 0), 0))
    vec = pl.BlockSpec((1, D_MODEL), lambda i: (0, 0))
    return pl.pallas_call(
        functools.partial(_ln2_body, n_ptiles=n_ptiles),
        grid=(n_ptiles + n_stiles,),
        in_specs=[p_row, s_row, row(D_MODEL), row(D_MODEL), row(TOP_K), vec, vec],
        out_specs=[p_row, s_row],
        out_shape=[jax.ShapeDtypeStruct((n_p, D_MODEL), F32), jax.ShapeDtypeStruct((n_s, D_MODEL), F32)],
        compiler_params=_cparams(("arbitrary",)),
        name="ln2",
    )(x_p, x_s, y0, y1, gate, g, b)


def _moe(h, expert, wg, wu, wd, layer):
    n = h.shape[0]
    s = n * TOP_K
    e_flat = expert.reshape(-1)
    onehot = (e_flat[:, None] == jnp.arange(N_EXPERTS, dtype=jnp.int32)[None, :]).astype(jnp.int32)
    csum = jnp.cumsum(onehot, axis=0)
    counts = csum[-1]
    pos_in = jnp.take_along_axis(csum, e_flat[:, None], axis=1)[:, 0] - 1
    padded = (counts + EXPERT_ROWS - 1) // EXPERT_ROWS * EXPERT_ROWS
    pad_end = jnp.cumsum(padded)
    pad_start = pad_end - padded
    dest = pad_start[e_flat] + pos_in
    n_blk = (s + N_EXPERTS * (EXPERT_ROWS - 1) + EXPERT_ROWS - 1) // EXPERT_ROWS
    n_rows = n_blk * EXPERT_ROWS
    tok = jnp.arange(s, dtype=jnp.int32) // TOP_K
    row_tok = (jnp.arange(n_rows, dtype=jnp.int32) % n).at[dest].set(tok)
    xs = h[row_tok]
    blk_start = jnp.arange(n_blk, dtype=jnp.int32) * EXPERT_ROWS
    blk_expert = jnp.minimum(jnp.sum((pad_end[None, :] <= blk_start[:, None]).astype(jnp.int32), axis=1),
                             N_EXPERTS - 1)
    ys = _experts(blk_expert, xs, wg, wu, wd, layer)
    dest2 = dest.reshape(n, TOP_K)
    return ys[dest2[:, 0]], ys[dest2[:, 1]]


def kernel(x_prompt, x_sample, cache_k, cache_v, state_pool, page_table, rel_bias, w_router, b_router,
           w_in, pool_w, pool_scale, w_pool_out, w_attn_out, w_o, ln1_g, ln1_b, w_gate, w_up, w_down,
           ln2_g, ln2_b):
    bsz, seq, _ = x_prompt.shape
    db, ts, _ = x_sample.shape
    past_len = page_table.shape[1] * PAGE_SIZE
    nb = seq // MOBA_BLOCK
    nbs = past_len // MOBA_BLOCK
    assert seq % MOBA_BLOCK == 0 and past_len % MOBA_BLOCK == 0 and ts <= 8
    np_rows = bsz * seq
    ns_rows = db * ts

    far_s = _far_bucket(MOBA_BLOCK + 1, past_len + ts)
    c = np.arange(MOBA_BLOCK)
    t_idx = np.repeat(np.arange(ts), N_HEADS)
    h_idx = np.tile(np.arange(N_HEADS), ts)
    d_last = MOBA_BLOCK + t_idx[:, None] - c[None, :]
    b_last_s = rel_bias[_bucket_np(d_last), h_idx[:, None]]
    tn = np.arange(8)
    d_new = t_idx[:, None] - tn[None, :]
    b_own_s = jnp.where((d_new >= 0) & (tn[None, :] < ts),
                        rel_bias[_bucket_np(np.maximum(d_new, 0)), h_idx[:, None]], NEG_INF)
    cfar_s = rel_bias[far_s][h_idx][:, None]
    head_mask = jnp.asarray((np.arange(ATT_WIDTH)[None, :] // HEAD_DIM) == h_idx[:, None])
    cache_kt = jnp.transpose(cache_k, (0, 2, 3, 4, 1))
    cache_vt = jnp.transpose(cache_v, (0, 2, 3, 4, 1))

    wr_pad = jnp.zeros((D_MODEL, ROUTER_PAD), BF16).at[:, :N_EXPERTS].set(w_router.astype(BF16))
    br_pad = jnp.zeros((1, ROUTER_PAD), F32).at[0, :N_EXPERTS].set(b_router)

    xp = x_prompt.reshape(np_rows, D_MODEL)
    xs = x_sample.reshape(ns_rows, D_MODEL)
    kp, vp, pp, ksm, vsm, psm = [], [], [], [], [], []
    for l in range(DEPTH):
        w_in_bf = w_in[l].astype(BF16)
        pool_w_bf = pool_w[l].astype(BF16)
        scale = pool_scale[l][None, :]
        wpo = w_pool_out[l].astype(BF16)
        wao = w_attn_out[l].astype(BF16)
        wo = w_o[l].astype(BF16)
        g1, b1 = ln1_g[l][None, :], ln1_b[l][None, :]
        g2, b2 = ln2_g[l][None, :], ln2_b[l][None, :]

        p_p, q_p, k_p, v_p, gp_p, ga_p, kb_p, vt_p, km_p = _inproj(xp, w_in_bf, tm=MOBA_BLOCK, attn_layouts=True,
                                                                   seq=seq)
        prev0 = jnp.zeros((bsz, POOL_HALO, POOL_WIDTH), F32)
        yp_p = _pool(p_p.reshape(bsz, seq, POOL_WIDTH), prev0, pool_w_bf, scale, tq=MOBA_BLOCK, pos0=0)
        o_p = _moba_prompt(q_p.reshape(bsz, seq, ATT_WIDTH),
                           kb_p.reshape(bsz, nb, MOBA_BLOCK, ATT_WIDTH),
                           vt_p.reshape(bsz, nb, ATT_WIDTH, MOBA_BLOCK),
                           km_p.reshape(bsz, nb, ATT_WIDTH), rel_bias)
        x1_p, rt_p = _merge(xp, yp_p.reshape(np_rows, POOL_WIDTH), o_p.reshape(np_rows, ATT_WIDTH),
                            gp_p, ga_p, wpo, wao, wo, g1, b1, wr_pad, br_pad, tm=MOBA_BLOCK)

        p_s, q_s, k_s, v_s, gp_s, ga_s = _inproj(xs, w_in_bf, tm=ns_rows, attn_layouts=False)
        state = state_pool[:, l]
        prev_s = jnp.concatenate([jnp.zeros((db, 1, POOL_WIDTH), F32), state], axis=1)
        p_s3 = p_s.reshape(db, ts, POOL_WIDTH)
        p_s8 = jnp.pad(p_s3, ((0, 0), (0, 8 - ts), (0, 0)))
        yp_s = _pool(p_s8, prev_s, pool_w_bf, scale, tq=8, pos0=past_len)[:, :ts]
        qbd = jnp.where(head_mask[None], jnp.repeat(q_s.reshape(db, ts, ATT_WIDTH), N_HEADS, axis=1), 0.0)
        knew8 = jnp.pad(k_s.reshape(db, ts, ATT_WIDTH), ((0, 0), (0, 8 - ts), (0, 0)))
        vnew8 = jnp.pad(v_s.reshape(db, ts, ATT_WIDTH), ((0, 0), (0, 8 - ts), (0, 0)))
        o_s = _moba_sample(page_table, qbd, knew8, vnew8, cfar_s, b_last_s, b_own_s,
                           cache_kt, cache_vt, l)[:, :ts]
        x1_s, rt_s = _merge(xs, yp_s.reshape(ns_rows, POOL_WIDTH), o_s.reshape(ns_rows, ATT_WIDTH),
                            gp_s, ga_s, wpo, wao, wo, g1, b1, wr_pad, br_pad, tm=ns_rows)

        x1 = jnp.concatenate([x1_p, x1_s], axis=0)
        e_p, gt_p = _unpack_route(rt_p)
        e_s, gt_s = _unpack_route(rt_s)
        y0, y1 = _moe(x1, jnp.concatenate([e_p, e_s], axis=0), w_gate, w_up, w_down, l)
        xp, xs = _ln2(x1_p, x1_s, y0, y1, jnp.concatenate([gt_p, gt_s], axis=0), g2, b2, tm=ROW_TILE)

        kp.append(k_p)
        vp.append(v_p)
        pp.append(p_p.reshape(bsz, seq, POOL_WIDTH)[:, seq - POOL_BUF:])
        ksm.append(k_s.reshape(db, ts, N_HEADS, HEAD_DIM))
        vsm.append(v_s.reshape(db, ts, N_HEADS, HEAD_DIM))
        psm.append(jnp.concatenate([state, p_s3], axis=1)[:, -POOL_BUF:])

    def rows_out(parts):
        a = jnp.stack(parts, axis=1).reshape(bsz, DEPTH, N_HEADS, HEAD_DIM, seq)
        return jnp.transpose(a, (0, 4, 1, 2, 3))

    return (xp.reshape(bsz, seq, D_MODEL), xs.reshape(db, ts, D_MODEL),
            rows_out(kp), rows_out(vp), jnp.stack(pp, axis=1),
            jnp.stack(ksm, axis=2), jnp.stack(vsm, axis=2), jnp.stack(psm, axis=1))
```

```python
import functools
import math

import numpy as np
import jax
import jax.numpy as jnp
from jax import lax
from jax.experimental import pallas as pl
from jax.experimental.pallas import tpu as pltpu

F32 = jnp.float32
BF16 = jnp.bfloat16
NEG_INF = float("-inf")

D_MODEL = 1024
DEPTH = 2
N_HEADS = 8
HEAD_DIM = 64
ATT_WIDTH = N_HEADS * HEAD_DIM
MOBA_BLOCK = 256
MOBA_TOPK = 3
N_BUCKETS = 32
REL_MAX_DIST = 128
POOL_WINDOWS = (2, 4, 8, 16)
POOL_WIDTH = 512
POOL_GW = 128
POOL_BUF = 15
POOL_HALO = 16
IN_WIDTH = POOL_WIDTH + 3 * ATT_WIDTH + 2 * D_MODEL
N_EXPERTS = 16
N_GROUPS = 4
EXPERTS_PER_GROUP = 4
TOP_K = 2
D_EXPERT = 512
EXPERT_ROWS = 512
ROW_TILE = 128
DEEPNORM_ALPHA = (2 * DEPTH) ** 0.25
LN_EPS = 1e-5
PAGE_SIZE = 128
ROUTER_PAD = 128
LANES = 128

VMEM_LIMIT = 52 * 1024 * 1024


def _cparams(sem):
    return pltpu.CompilerParams(dimension_semantics=sem, vmem_limit_bytes=VMEM_LIMIT)


def _nt_dot(a, b):
    return lax.dot_general(a, b, (((1,), (1,)), ((), ())), preferred_element_type=F32)


def _top_mask(scores, cand, k, n):
    idx = lax.broadcasted_iota(jnp.int32, scores.shape, 0)
    sel = jnp.zeros(scores.shape, jnp.bool_)
    for _ in range(k):
        c = cand & jnp.logical_not(sel)
        cur = jnp.where(c, scores, NEG_INF)
        mx = jnp.max(cur, axis=0, keepdims=True)
        first = jnp.min(jnp.where(c & (cur == mx), idx, n), axis=0, keepdims=True)
        sel = sel | (idx == first)
    return sel


def _inproj_body(x_ref, w_ref, p_ref, q_ref, k_ref, v_ref, gp_ref, ga_ref, *attn_refs, tm):
    xb = x_ref[...].astype(BF16)

    def seg(a, b):
        return jnp.dot(xb, w_ref[:, a:b], preferred_element_type=F32)

    c0 = POOL_WIDTH
    p_ref[...] = seg(0, c0)
    q_ref[...] = seg(c0, c0 + ATT_WIDTH) * (HEAD_DIM ** -0.5)
    k = seg(c0 + ATT_WIDTH, c0 + 2 * ATT_WIDTH)
    v = seg(c0 + 2 * ATT_WIDTH, c0 + 3 * ATT_WIDTH)
    g0 = c0 + 3 * ATT_WIDTH
    gp_ref[...] = jax.nn.sigmoid(seg(g0, g0 + D_MODEL))
    ga_ref[...] = jax.nn.sigmoid(seg(g0 + D_MODEL, g0 + 2 * D_MODEL))
    if attn_refs:
        kb_ref, vt_ref, km_ref = attn_refs
        k_ref[...] = k.T
        vt = v.T
        v_ref[...] = vt
        kb_ref[...] = k.astype(BF16)
        vt_ref[0] = vt.astype(BF16)
        km_ref[0] = jnp.sum(k, axis=0, keepdims=True) * (1.0 / tm)
    else:
        k_ref[...] = k
        v_ref[...] = v


def _inproj(x, w_bf, *, tm, attn_layouts, seq=None):
    m = x.shape[0]
    nt = m // tm
    row = lambda n: pl.BlockSpec((tm, n), lambda i: (i, 0))
    out_shape = [jax.ShapeDtypeStruct((m, POOL_WIDTH), F32),
                 jax.ShapeDtypeStruct((m, ATT_WIDTH), F32),
                 jax.ShapeDtypeStruct((m, ATT_WIDTH), F32),
                 jax.ShapeDtypeStruct((m, ATT_WIDTH), F32),
                 jax.ShapeDtypeStruct((m, D_MODEL), F32),
                 jax.ShapeDtypeStruct((m, D_MODEL), F32)]
    out_specs = [row(POOL_WIDTH), row(ATT_WIDTH), row(ATT_WIDTH), row(ATT_WIDTH),
                 row(D_MODEL), row(D_MODEL)]
    if attn_layouts:
        assert tm == MOBA_BLOCK
        tps = seq // tm
        for idx in (2, 3):
            out_shape[idx] = jax.ShapeDtypeStruct((m // seq, ATT_WIDTH, seq), F32)
            out_specs[idx] = pl.BlockSpec((None, ATT_WIDTH, tm), lambda i: (i // tps, 0, i % tps))
        out_shape += [jax.ShapeDtypeStruct((m, ATT_WIDTH), BF16),
                      jax.ShapeDtypeStruct((nt, ATT_WIDTH, tm), BF16),
                      jax.ShapeDtypeStruct((nt, 1, ATT_WIDTH), F32)]
        out_specs += [row(ATT_WIDTH),
                      pl.BlockSpec((1, ATT_WIDTH, tm), lambda i: (i, 0, 0)),
                      pl.BlockSpec((1, 1, ATT_WIDTH), lambda i: (i, 0, 0))]
    return pl.pallas_call(
        functools.partial(_inproj_body, tm=tm),
        grid=(nt,),
        in_specs=[row(D_MODEL), pl.BlockSpec((D_MODEL, IN_WIDTH), lambda i: (0, 0))],
        out_specs=out_specs,
        out_shape=out_shape,
        compiler_params=_cparams(("arbitrary",)),
        name="inproj",
    )(x, w_bf)


def _pool_body(prev_ref, halo_ref, p_ref, pw_ref, sc_ref, y_ref, *, tq, pos0):
    i = pl.program_id(1)
    p = p_ref[0]
    halo = jnp.where(i == 0, prev_ref[0], halo_ref[0])
    z = jnp.concatenate([halo, p], axis=0)
    lane = lax.broadcasted_iota(jnp.int32, z.shape, 1)
    x = z
    for s in (8, 4, 2, 1):
        thr = POOL_WIDTH - POOL_GW * {8: 1, 4: 2, 2: 3, 1: 4}[s]
        sh = pltpu.roll(x, s, 0)
        x = x + (jnp.where(lane >= thr, sh, 0.0) if thr > 0 else sh)
    wsum = x[POOL_HALO:, :]
    row = lax.broadcasted_iota(jnp.int32, (tq, POOL_WIDTH), 0)
    lane2 = lax.broadcasted_iota(jnp.int32, (tq, POOL_WIDTH), 1)
    wl = jnp.where(lane2 < POOL_GW, POOL_WINDOWS[0],
                   jnp.where(lane2 < 2 * POOL_GW, POOL_WINDOWS[1],
                             jnp.where(lane2 < 3 * POOL_GW, POOL_WINDOWS[2], POOL_WINDOWS[3])))
    pos = pos0 + i * tq + row
    cnt = jnp.minimum(pos + 1, wl).astype(F32)
    d = (wsum / cnt - p).astype(BF16)
    ys = [jnp.dot(d[:, g * POOL_GW:(g + 1) * POOL_GW], pw_ref[g], preferred_element_type=F32)
          for g in range(len(POOL_WINDOWS))]
    y_ref[0] = jnp.concatenate(ys, axis=1) * sc_ref[...]


def _pool(p, prev16, pool_w_bf, scale, *, tq, pos0):
    b, t, c = p.shape
    nq = t // tq
    hb = tq // POOL_HALO
    return pl.pallas_call(
        functools.partial(_pool_body, tq=tq, pos0=pos0),
        grid=(b, nq),
        in_specs=[pl.BlockSpec((1, POOL_HALO, c), lambda bi, i: (bi, 0, 0)),
                  pl.BlockSpec((1, POOL_HALO, c), lambda bi, i: (bi, jnp.maximum(i * hb - 1, 0), 0)),
                  pl.BlockSpec((1, tq, c), lambda bi, i: (bi, i, 0)),
                  pl.BlockSpec((len(POOL_WINDOWS), POOL_GW, POOL_GW), lambda bi, i: (0, 0, 0)),
                  pl.BlockSpec((1, c), lambda bi, i: (0, 0))],
        out_specs=pl.BlockSpec((1, tq, c), lambda bi, i: (bi, i, 0)),
        out_shape=jax.ShapeDtypeStruct((b, t, c), F32),
        compiler_params=_cparams(("arbitrary", "arbitrary")),
        name="pool",
    )(prev16, p, p, pool_w_bf, scale)


def _bucket_np(dist):
    n = np.maximum(dist, 0)
    max_exact = N_BUCKETS // 2
    nf = np.maximum(n, 1).astype(np.float32)
    large = max_exact + (np.log(nf / np.float32(max_exact)) / np.float32(math.log(REL_MAX_DIST / max_exact))
                         * np.float32(N_BUCKETS - max_exact)).astype(np.int32)
    large = np.minimum(large, N_BUCKETS - 1)
    return np.where(n < max_exact, n, large)


def _far_bucket(min_dist, max_dist):
    b = _bucket_np(np.arange(min_dist, max_dist + 1))
    assert (b == b[0]).all()
    return int(b[0])


def _moba_prompt_body(rb_ref, q_ref, kb_ref, vt_ref, km_ref, io_ref, ip_ref, o_ref,
                      sel_ref, qz_ref, m_ref, l_ref, acc_ref, bown_ref, bprev_ref, *, nb, far):
    i = pl.program_id(1)
    tq = MOBA_BLOCK

    @pl.when((pl.program_id(0) == 0) & (i == 0))
    def _():
        for h in range(N_HEADS):
            bown_ref[h] = jnp.full((tq, tq), NEG_INF, F32)
            bprev_ref[h] = jnp.zeros((tq, tq), F32)
        io = io_ref[...]
        ip = ip_ref[...]

        def fill(bkt, carry):
            mo = io == bkt
            mp = ip == bkt
            for h in range(N_HEADS):
                val = rb_ref[bkt, h] - rb_ref[far, h]
                bown_ref[h] = jnp.where(mo, val, bown_ref[h])
                bprev_ref[h] = jnp.where(mp, val, bprev_ref[h])
            return carry

        lax.fori_loop(0, N_BUCKETS, fill, 0)

    blk = lax.broadcasted_iota(jnp.int32, (nb, tq), 0)
    valid = blk < i
    lane = lax.broadcasted_iota(jnp.int32, (tq, LANES), 1)
    for hp in range(N_HEADS // 2):
        qf = q_ref[0, :, LANES * hp:LANES * (hp + 1)]
        km = km_ref[0, :, LANES * hp:LANES * (hp + 1)].astype(BF16)
        for hh in range(2):
            h = 2 * hp + hh
            qz = jnp.where((lane >= HEAD_DIM * hh) & (lane < HEAD_DIM * (hh + 1)), qf, 0.0).astype(BF16)
            qz_ref[h] = qz
            sel = _top_mask(_nt_dot(km, qz), valid, MOBA_TOPK, nb)
            selb = jnp.where(sel, 0.0, NEG_INF)
            for jj in range(nb):
                sel_ref[h, jj] = selb[jj:jj + 1, :]
            m_ref[h] = jnp.full((1, tq), -1e30, F32)
            l_ref[h] = jnp.zeros((1, tq), F32)
            acc_ref[h] = jnp.zeros((HEAD_DIM, tq), F32)

    def key_blocks(items):
        sts = [[_nt_dot(kb_ref[0, j, :, LANES * (h // 2):LANES * (h // 2 + 1)], qz_ref[h])
                for h in range(N_HEADS)] for j, _, _ in items]
        ps, alphas = [], []
        for h in range(N_HEADS):
            m = m_ref[h]
            m_new = m
            shifted = []
            for idx, (j, bias_ref, masked) in enumerate(items):
                st = sts[idx][h] if bias_ref is None else sts[idx][h] + bias_ref[h]
                cm = jnp.max(st, axis=0, keepdims=True)
                mb = sel_ref[h, j] if masked else None
                m_new = jnp.maximum(m_new, cm + mb if masked else cm)
                shifted.append((st, mb))
            alpha = jnp.exp(m - m_new)
            lsum = alpha * l_ref[h]
            ph = []
            for st, mb in shifted:
                p = jnp.exp(st - m_new) if mb is None else jnp.exp(st + (mb - m_new))
                lsum = lsum + jnp.sum(p, axis=0, keepdims=True)
                ph.append(p.astype(BF16))
            l_ref[h] = lsum
            m_ref[h] = m_new
            ps.append(ph)
            alphas.append(alpha)
        for h in range(N_HEADS):
            acc = alphas[h] * acc_ref[h]
            for idx, (j, _, _) in enumerate(items):
                vt = vt_ref[0, j, HEAD_DIM * h:HEAD_DIM * (h + 1), :]
                acc = acc + jnp.dot(vt, ps[h][idx], preferred_element_type=F32)
            acc_ref[h] = acc

    n_far = jnp.maximum(i - 1, 0)

    def far_body(jj, carry):
        key_blocks([(2 * jj, None, True), (2 * jj + 1, None, True)])
        return carry

    lax.fori_loop(0, n_far // 2, far_body, 0)

    @pl.when(n_far % 2 == 1)
    def _():
        key_blocks([(n_far - 1, None, True)])

    jp = jnp.maximum(i - 1, 0)
    key_blocks([(jp, bprev_ref, True), (i, bown_ref, False)])
    for hp in range(N_HEADS // 2):
        pair = jnp.concatenate([acc_ref[2 * hp] / l_ref[2 * hp], acc_ref[2 * hp + 1] / l_ref[2 * hp + 1]],
                               axis=0)
        o_ref[0, :, LANES * hp:LANES * (hp + 1)] = pair.T


def _moba_prompt(q, kb, vt, km, rel_bias):
    b, t, _ = q.shape
    nb = t // MOBA_BLOCK
    far = _far_bucket(MOBA_BLOCK + 1, t)
    c = np.arange(MOBA_BLOCK)
    d_own = c[None, :] - c[:, None]
    idx_own = jnp.asarray(np.where(d_own >= 0, _bucket_np(d_own), -1).astype(np.int32))
    idx_prev = jnp.asarray(_bucket_np(d_own + MOBA_BLOCK).astype(np.int32))
    once = pl.Buffered(1)
    tile = pl.BlockSpec((MOBA_BLOCK, MOBA_BLOCK), lambda bi, i: (0, 0), pipeline_mode=once)
    return pl.pallas_call(
        functools.partial(_moba_prompt_body, nb=nb, far=far),
        grid=(b, nb),
        in_specs=[pl.BlockSpec(memory_space=pltpu.SMEM),
                  pl.BlockSpec((1, MOBA_BLOCK, ATT_WIDTH), lambda bi, i: (bi, i, 0)),
                  pl.BlockSpec((1, nb, MOBA_BLOCK, ATT_WIDTH), lambda bi, i: (bi, 0, 0, 0), pipeline_mode=once),
                  pl.BlockSpec((1, nb, ATT_WIDTH, MOBA_BLOCK), lambda bi, i: (bi, 0, 0, 0), pipeline_mode=once),
                  pl.BlockSpec((1, nb, ATT_WIDTH), lambda bi, i: (bi, 0, 0)),
                  tile, tile],
        out_specs=pl.BlockSpec((1, MOBA_BLOCK, ATT_WIDTH), lambda bi, i: (bi, i, 0)),
        out_shape=jax.ShapeDtypeStruct((b, t, ATT_WIDTH), F32),
        scratch_shapes=[pltpu.VMEM((N_HEADS, nb, 1, MOBA_BLOCK), F32),
                        pltpu.VMEM((N_HEADS, MOBA_BLOCK, LANES), BF16),
                        pltpu.VMEM((N_HEADS, 1, MOBA_BLOCK), F32),
                        pltpu.VMEM((N_HEADS, 1, MOBA_BLOCK), F32),
                        pltpu.VMEM((N_HEADS, HEAD_DIM, MOBA_BLOCK), F32),
                        pltpu.VMEM((N_HEADS, MOBA_BLOCK, MOBA_BLOCK), F32),
                        pltpu.VMEM((N_HEADS, MOBA_BLOCK, MOBA_BLOCK), F32)],
        compiler_params=_cparams(("arbitrary", "arbitrary")),
        name="moba_prompt",
    )(rel_bias, q, kb, vt, km, idx_own, idx_prev)


SAMPLE_PAGES_PER_STEP = 16
PAGES_PER_BLOCK = MOBA_BLOCK // PAGE_SIZE


def _moba_sample_body(pt_ref, qbd_ref, knew_ref, vnew_ref, cfar_ref, blast_ref, bown_ref, ck_hbm, cv_hbm,
                      o_ref, gs_sc, m_sc, l_sc, o_sc, kbuf, vbuf, sem_k, sem_v, *, nbs, ts, layer):
    npg = SAMPLE_PAGES_PER_STEP
    ppb = PAGES_PER_BLOCK
    b = pl.program_id(0)
    s = pl.program_id(1)
    nsteps = pl.num_programs(1)
    g = b * nsteps + s
    cur = g % 2
    bps = npg // ppb
    nr = ts * N_HEADS

    def page_copies(bb, ss, slot):
        out = []
        for u in range(npg):
            page = pt_ref[bb, ss * npg + u]
            out.append(pltpu.make_async_copy(ck_hbm.at[page, layer], kbuf.at[slot, u], sem_k.at[slot]))
            out.append(pltpu.make_async_copy(cv_hbm.at[page, layer], vbuf.at[slot, u], sem_v.at[slot]))
        return out

    @pl.when(g == 0)
    def _():
        for c in page_copies(b, s, 0):
            c.start()

    @pl.when(g + 1 < pl.num_programs(0) * nsteps)
    def _():
        wrap = s + 1 == nsteps
        for c in page_copies(jnp.where(wrap, b + 1, b), jnp.where(wrap, 0, s + 1), 1 - cur):
            c.start()

    for c in page_copies(b, s, cur):
        c.wait()
    k_refs = [kbuf.at[cur, u] for u in range(npg)]
    v_refs = [vbuf.at[cur, u] for u in range(npg)]
    qb = qbd_ref[0].astype(BF16)
    lane0 = lax.broadcasted_iota(jnp.int32, (ATT_WIDTH, LANES), 1) == 0
    sts = []
    for jj in range(bps):
        kts = [k_refs[ppb * jj + u][...].reshape(ATT_WIDTH, PAGE_SIZE) for u in range(ppb)]
        ksum = functools.reduce(lambda a, b: a + b, [jnp.sum(kt, axis=1, keepdims=True) for kt in kts])
        kmean = ksum * (1.0 / MOBA_BLOCK)
        kmcol = jnp.where(lane0, kmean, 0.0).astype(BF16)
        w = jnp.concatenate([kt.astype(BF16) for kt in kts] + [kmcol], axis=1)
        sts.append(jnp.dot(qb, w, preferred_element_type=F32))
    ps = []
    for jj in range(bps):
        j = s * bps + jj
        gs_sc[j] = sts[jj][:, MOBA_BLOCK:MOBA_BLOCK + 1]
        bias = jnp.where(j == nbs - 1, blast_ref[...], cfar_ref[...])
        st = sts[jj][:, :MOBA_BLOCK] + bias
        m = jnp.max(st, axis=1, keepdims=True)
        p = jnp.exp(st - m)
        m_sc[j] = m
        l_sc[j] = jnp.sum(p, axis=1, keepdims=True)
        ps.append(p.astype(BF16))
    for jj in range(bps):
        o = None
        for u in range(ppb):
            vt = v_refs[ppb * jj + u][...].reshape(ATT_WIDTH, PAGE_SIZE).astype(BF16)
            part = _nt_dot(ps[jj][:, PAGE_SIZE * u:PAGE_SIZE * (u + 1)], vt)
            o = part if o is None else o + part
        o_sc[s * bps + jj] = o

    @pl.when(s == nsteps - 1)
    def _():
        qf = qb.astype(F32)
        gs = gs_sc[...]
        sel = _top_mask(gs, jnp.ones(gs.shape, jnp.bool_), MOBA_TOPK, nbs)
        knew = knew_ref[0].astype(BF16).astype(F32)
        vnew = vnew_ref[0].astype(BF16).astype(F32)
        s_own = [jnp.sum(qf * knew[t:t + 1, :], axis=1, keepdims=True) + bown_ref[:, t:t + 1]
                 for t in range(ts)]
        m_all = m_sc[...]
        mtot = jnp.max(jnp.where(sel, m_all, NEG_INF), axis=0)
        for t in range(ts):
            mtot = jnp.maximum(mtot, s_own[t])
        w = jnp.where(sel, jnp.exp(m_all - mtot[None]), 0.0)
        ltot = jnp.sum(w * l_sc[...], axis=0)
        gs_sc[...] = w

        def merge(j, acc):
            return acc + gs_sc[j] * o_sc[j]

        otot = lax.fori_loop(0, nbs, merge, jnp.zeros((nr, ATT_WIDTH), F32))
        for t in range(ts):
            pt = jnp.exp(s_own[t] - mtot)
            ltot = ltot + pt
            otot = otot + pt.astype(BF16).astype(F32) * vnew[t:t + 1, :]
        out = otot / ltot
        row = lax.broadcasted_iota(jnp.int32, out.shape, 0)
        lane = lax.broadcasted_iota(jnp.int32, out.shape, 1)
        out = jnp.where((lane // HEAD_DIM) == (row % N_HEADS), out, 0.0)
        pieces = [jnp.sum(out[N_HEADS * t:N_HEADS * (t + 1), :], axis=0, keepdims=True) for t in range(ts)]
        pieces.append(jnp.zeros((8 - ts, ATT_WIDTH), F32))
        o_ref[0] = jnp.concatenate(pieces, axis=0)


def _moba_sample(page_table, qbd, knew8, vnew8, cfar_rows, b_last, b_own, cache_kt, cache_vt, layer):
    db, nr, _ = qbd.shape
    ts = nr // N_HEADS
    n_pages = page_table.shape[1]
    nbs = n_pages // PAGES_PER_BLOCK
    npg = SAMPLE_PAGES_PER_STEP
    nsteps = n_pages // npg

    full2 = lambda shp: pl.BlockSpec(shp, lambda b, s, pt: (0, 0))
    per_b = lambda r: pl.BlockSpec((1, r, ATT_WIDTH), lambda b, s, pt: (b, 0, 0))
    page_buf = pltpu.VMEM((2, npg, N_HEADS, HEAD_DIM, PAGE_SIZE), F32)
    grid_spec = pltpu.PrefetchScalarGridSpec(
        num_scalar_prefetch=1,
        grid=(db, nsteps),
        in_specs=[per_b(nr), per_b(8), per_b(8),
                  full2((nr, 1)), full2((nr, MOBA_BLOCK)), full2((nr, 8)),
                  pl.BlockSpec(memory_space=pl.ANY), pl.BlockSpec(memory_space=pl.ANY)],
        out_specs=per_b(8),
        scratch_shapes=[pltpu.VMEM((nbs, nr, 1), F32), pltpu.VMEM((nbs, nr, 1), F32),
                        pltpu.VMEM((nbs, nr, 1), F32), pltpu.VMEM((nbs, nr, ATT_WIDTH), F32),
                        page_buf, page_buf, pltpu.SemaphoreType.DMA((2,)), pltpu.SemaphoreType.DMA((2,))],
    )
    return pl.pallas_call(
        functools.partial(_moba_sample_body, nbs=nbs, ts=ts, layer=layer),
        grid_spec=grid_spec,
        out_shape=jax.ShapeDtypeStruct((db, 8, ATT_WIDTH), F32),
        compiler_params=_cparams(("arbitrary", "arbitrary")),
        name="moba_sample",
    )(page_table, qbd, knew8, vnew8, cfar_rows, b_last, b_own, cache_kt, cache_vt)


def _layer_norm(h, g, b):
    mu = jnp.mean(h, axis=-1, keepdims=True)
    c = h - mu
    var = jnp.mean(c * c, axis=-1, keepdims=True)
    return c * lax.rsqrt(var + LN_EPS) * g + b


def _argmax_first(vals):
    best, idx = vals[0], jnp.zeros(vals[0].shape, jnp.int32)
    for k in range(1, len(vals)):
        upd = vals[k] > best
        idx = jnp.where(upd, k, idx)
        best = jnp.where(upd, vals[k], best)
    return best, idx


def _route_rows(logit_rows):
    mx = functools.reduce(jnp.maximum, logit_rows)
    ex = [jnp.exp(r - mx) for r in logit_rows]
    tot = functools.reduce(lambda a, b: a + b, ex)
    probs = [e / tot for e in ex]
    scores = []
    for g in range(N_GROUPS):
        a, b, c, d = probs[EXPERTS_PER_GROUP * g:EXPERTS_PER_GROUP * (g + 1)]
        s1, t1 = jnp.maximum(a, b), jnp.minimum(a, b)
        s2, t2 = jnp.maximum(c, d), jnp.minimum(c, d)
        scores.append(jnp.maximum(s1, s2) + jnp.maximum(jnp.minimum(s1, s2), jnp.maximum(t1, t2)))
    _, gi = _argmax_first(scores)
    ing = []
    for j in range(EXPERTS_PER_GROUP):
        v = probs[j]
        for g in range(1, N_GROUPS):
            v = jnp.where(gi == g, probs[EXPERTS_PER_GROUP * g + j], v)
        ing.append(v)
    w1, i1 = _argmax_first(ing)
    w2, i2 = _argmax_first([jnp.where(i1 == j, -1.0, ing[j]) for j in range(EXPERTS_PER_GROUP)])
    den = w1 + w2
    e1 = (gi * EXPERTS_PER_GROUP + i1).astype(F32)
    e2 = (gi * EXPERTS_PER_GROUP + i2).astype(F32)
    return e1, e2, w1 / den, w2 / den


def _merge_body(x_ref, yp_ref, o_ref, gp_ref, ga_ref, wpo_ref, wao_ref, wo_ref, g_ref, b_ref,
                wr_ref, br_ref, x1_ref, rt_ref, *, tm):
    a = jnp.dot(yp_ref[...].astype(BF16), wpo_ref[...], preferred_element_type=F32)
    bb = jnp.dot(o_ref[...].astype(BF16), wao_ref[...], preferred_element_type=F32)
    merged = gp_ref[...] * a + ga_ref[...] * bb
    mix = jnp.dot(merged.astype(BF16), wo_ref[...], preferred_element_type=F32)
    x1 = _layer_norm(DEEPNORM_ALPHA * x_ref[...] + mix, g_ref[...], b_ref[...])
    x1_ref[...] = x1
    lg = jnp.dot(x1.astype(BF16), wr_ref[...], preferred_element_type=F32) + br_ref[...]
    lgt = lg.T
    e1, e2, g1, g2 = _route_rows([lgt[e:e + 1, :] for e in range(N_EXPERTS)])
    r = lax.broadcasted_iota(jnp.int32, (8, tm), 0)
    rt_ref[0] = jnp.where(r == 0, e1, jnp.where(r == 1, e2, jnp.where(r == 2, g1, jnp.where(r == 3, g2, 0.0))))


def _merge(x, yp, o, gp, ga, wpo, wao, wo, g, b, wr, br, *, tm):
    m = x.shape[0]
    row = lambda n: pl.BlockSpec((tm, n), lambda i: (i, 0))
    full = lambda r, c: pl.BlockSpec((r, c), lambda i: (0, 0))
    return pl.pallas_call(
        functools.partial(_merge_body, tm=tm),
        grid=(m // tm,),
        in_specs=[row(D_MODEL), row(POOL_WIDTH), row(ATT_WIDTH), row(D_MODEL), row(D_MODEL),
                  full(POOL_WIDTH, D_MODEL), full(ATT_WIDTH, D_MODEL), full(D_MODEL, D_MODEL),
                  full(1, D_MODEL), full(1, D_MODEL), full(D_MODEL, ROUTER_PAD), full(1, ROUTER_PAD)],
        out_specs=[row(D_MODEL), pl.BlockSpec((1, 8, tm), lambda i: (i, 0, 0))],
        out_shape=[jax.ShapeDtypeStruct((m, D_MODEL), F32), jax.ShapeDtypeStruct((m // tm, 8, tm), F32)],
        compiler_params=_cparams(("arbitrary",)),
        name="merge",
    )(x, yp, o, gp, ga, wpo, wao, wo, g, b, wr, br)


def _unpack_route(rt):
    n = rt.shape[0] * rt.shape[2]
    cols = rt[:, :4, :].transpose(0, 2, 1).reshape(n, 4)
    return cols[:, :2].astype(jnp.int32), cols[:, 2:]


def _experts_body(be_ref, xs_ref, wg_ref, wu_ref, wd_ref, ys_ref, wg_sc, wu_sc, wd_sc):
    i = pl.program_id(0)

    @pl.when((i == 0) | (be_ref[i] != be_ref[jnp.maximum(i - 1, 0)]))
    def _():
        wg_sc[...] = wg_ref[0].astype(BF16)
        wu_sc[...] = wu_ref[0].astype(BF16)
        wd_sc[...] = wd_ref[0].astype(BF16)

    xb = xs_ref[...].astype(BF16)
    h1 = jnp.dot(xb, wg_sc[...], preferred_element_type=F32)
    h2 = jnp.dot(xb, wu_sc[...], preferred_element_type=F32)
    act = (h1 * jax.nn.sigmoid(h1) * h2).astype(BF16)
    ys_ref[...] = jnp.dot(act, wd_sc[...], preferred_element_type=F32)


def _experts(blk_expert, xs, wg, wu, wd, layer):
    n_rows = xs.shape[0]
    n_blk = n_rows // EXPERT_ROWS
    grid_spec = pltpu.PrefetchScalarGridSpec(
        num_scalar_prefetch=1,
        grid=(n_blk,),
        in_specs=[pl.BlockSpec((EXPERT_ROWS, D_MODEL), lambda i, be: (i, 0)),
                  pl.BlockSpec((None, 1, D_MODEL, D_EXPERT), lambda i, be: (layer, be[i], 0, 0)),
                  pl.BlockSpec((None, 1, D_MODEL, D_EXPERT), lambda i, be: (layer, be[i], 0, 0)),
                  pl.BlockSpec((None, 1, D_EXPERT, D_MODEL), lambda i, be: (layer, be[i], 0, 0))],
        out_specs=pl.BlockSpec((EXPERT_ROWS, D_MODEL), lambda i, be: (i, 0)),
        scratch_shapes=[pltpu.VMEM((D_MODEL, D_EXPERT), BF16), pltpu.VMEM((D_MODEL, D_EXPERT), BF16),
                        pltpu.VMEM((D_EXPERT, D_MODEL), BF16)],
    )
    return pl.pallas_call(
        _experts_body,
        grid_spec=grid_spec,
        out_shape=jax.ShapeDtypeStruct((n_rows, D_MODEL), F32),
        compiler_params=_cparams(("arbitrary",)),
        name="experts",
    )(blk_expert, xs, wg, wu, wd)


def _ln2_body(xp_ref, xs_ref, y0_ref, y1_ref, gt_ref, g_ref, b_ref, op_ref, os_ref, *, n_ptiles):
    i = pl.program_id(0)
    gt = gt_ref[...]
    f = gt[:, 0:1] * y0_ref[...] + gt[:, 1:2] * y1_ref[...]

    @pl.when(i < n_ptiles)
    def _():
        op_ref[...] = _layer_norm(DEEPNORM_ALPHA * xp_ref[...] + f, g_ref[...], b_ref[...])

    @pl.when(i >= n_ptiles)
    def _():
        os_ref[...] = _layer_norm(DEEPNORM_ALPHA * xs_ref[...] + f, g_ref[...], b_ref[...])


def _ln2(x_p, x_s, y0, y1, gate, g, b, *, tm):
    n_p, n_s = x_p.shape[0], x_s.shape[0]
    n_ptiles, n_stiles = n_p // tm, n_s // tm
    assert n_p % tm == 0 and n_s % tm == 0
    row = lambda n: pl.BlockSpec((tm, n), lambda i: (i, 0))
    p_row = pl.BlockSpec((tm, D_MODEL), lambda i: (jnp.minimum(i, n_ptiles - 1), 0))
    s_row = pl.BlockSpec((tm, D_MODEL), lambda i: (jnp.maximum(i - n_ptiles, 0), 0))
    vec = pl.BlockSpec((1, D_MODEL), lambda i: (0, 0))
    return pl.pallas_call(
        functools.partial(_ln2_body, n_ptiles=n_ptiles),
        grid=(n_ptiles + n_stiles,),
        in_specs=[p_row, s_row, row(D_MODEL), row(D_MODEL), row(TOP_K), vec, vec],
        out_specs=[p_row, s_row],
        out_shape=[jax.ShapeDtypeStruct((n_p, D_MODEL), F32), jax.ShapeDtypeStruct((n_s, D_MODEL), F32)],
        compiler_params=_cparams(("arbitrary",)),
        name="ln2",
    )(x_p, x_s, y0, y1, gate, g, b)


def _moe(h, expert, wg, wu, wd, layer):
    n = h.shape[0]
    s = n * TOP_K
    e_flat = expert.reshape(-1)
    onehot = (e_flat[:, None] == jnp.arange(N_EXPERTS, dtype=jnp.int32)[None, :]).astype(jnp.int32)
    csum = jnp.cumsum(onehot, axis=0)
    counts = csum[-1]
    pos_in = jnp.take_along_axis(csum, e_flat[:, None], axis=1)[:, 0] - 1
    padded = (counts + EXPERT_ROWS - 1) // EXPERT_ROWS * EXPERT_ROWS
    pad_end = jnp.cumsum(padded)
    pad_start = pad_end - padded
    dest = pad_start[e_flat] + pos_in
    n_blk = (s + N_EXPERTS * (EXPERT_ROWS - 1) + EXPERT_ROWS - 1) // EXPERT_ROWS
    n_rows = n_blk * EXPERT_ROWS
    tok = jnp.arange(s, dtype=jnp.int32) // TOP_K
    row_tok = (jnp.arange(n_rows, dtype=jnp.int32) % n).at[dest].set(tok)
    xs = h[row_tok]
    blk_start = jnp.arange(n_blk, dtype=jnp.int32) * EXPERT_ROWS
    blk_expert = jnp.minimum(jnp.sum((pad_end[None, :] <= blk_start[:, None]).astype(jnp.int32), axis=1),
                             N_EXPERTS - 1)
    ys = _experts(blk_expert, xs, wg, wu, wd, layer)
    dest2 = dest.reshape(n, TOP_K)
    return ys[dest2[:, 0]], ys[dest2[:, 1]]


def kernel(x_prompt, x_sample, cache_k, cache_v, state_pool, page_table, rel_bias, w_router, b_router,
           w_in, pool_w, pool_scale, w_pool_out, w_attn_out, w_o, ln1_g, ln1_b, w_gate, w_up, w_down,
           ln2_g, ln2_b):
    bsz, seq, _ = x_prompt.shape
    db, ts, _ = x_sample.shape
    past_len = page_table.shape[1] * PAGE_SIZE
    nb = seq // MOBA_BLOCK
    nbs = past_len // MOBA_BLOCK
    assert seq % MOBA_BLOCK == 0 and past_len % MOBA_BLOCK == 0 and ts <= 8
    np_rows = bsz * seq
    ns_rows = db * ts

    far_s = _far_bucket(MOBA_BLOCK + 1, past_len + ts)
    c = np.arange(MOBA_BLOCK)
    t_idx = np.repeat(np.arange(ts), N_HEADS)
    h_idx = np.tile(np.arange(N_HEADS), ts)
    d_last = MOBA_BLOCK + t_idx[:, None] - c[None, :]
    b_last_s = rel_bias[_bucket_np(d_last), h_idx[:, None]]
    tn = np.arange(8)
    d_new = t_idx[:, None] - tn[None, :]
    b_own_s = jnp.where((d_new >= 0) & (tn[None, :] < ts),
                        rel_bias[_bucket_np(np.maximum(d_new, 0)), h_idx[:, None]], NEG_INF)
    cfar_s = rel_bias[far_s][h_idx][:, None]
    head_mask = jnp.asarray((np.arange(ATT_WIDTH)[None, :] // HEAD_DIM) == h_idx[:, None])
    cache_kt = jnp.transpose(cache_k, (0, 2, 3, 4, 1))
    cache_vt = jnp.transpose(cache_v, (0, 2, 3, 4, 1))

    wr_pad = jnp.zeros((D_MODEL, ROUTER_PAD), BF16).at[:, :N_EXPERTS].set(w_router.astype(BF16))
    br_pad = jnp.zeros((1, ROUTER_PAD), F32).at[0, :N_EXPERTS].set(b_router)

    xp = x_prompt.reshape(np_rows, D_MODEL)
    xs = x_sample.reshape(ns_rows, D_MODEL)
    kp, vp, pp, ksm, vsm, psm = [], [], [], [], [], []
    for l in range(DEPTH):
        w_in_bf = w_in[l].astype(BF16)
        pool_w_bf = pool_w[l].astype(BF16)
        scale = pool_scale[l][None, :]
        wpo = w_pool_out[l].astype(BF16)
        wao = w_attn_out[l].astype(BF16)
        wo = w_o[l].astype(BF16)
        g1, b1 = ln1_g[l][None, :], ln1_b[l][None, :]
        g2, b2 = ln2_g[l][None, :], ln2_b[l][None, :]

        p_p, q_p, k_p, v_p, gp_p, ga_p, kb_p, vt_p, km_p = _inproj(xp, w_in_bf, tm=MOBA_BLOCK, attn_layouts=True,
                                                                   seq=seq)
        prev0 = jnp.zeros((bsz, POOL_HALO, POOL_WIDTH), F32)
        yp_p = _pool(p_p.reshape(bsz, seq, POOL_WIDTH), prev0, pool_w_bf, scale, tq=MOBA_BLOCK, pos0=0)
        o_p = _moba_prompt(q_p.reshape(bsz, seq, ATT_WIDTH),
                           kb_p.reshape(bsz, nb, MOBA_BLOCK, ATT_WIDTH),
                           vt_p.reshape(bsz, nb, ATT_WIDTH, MOBA_BLOCK),
                           km_p.reshape(bsz, nb, ATT_WIDTH), rel_bias)
        x1_p, rt_p = _merge(xp, yp_p.reshape(np_rows, POOL_WIDTH), o_p.reshape(np_rows, ATT_WIDTH),
                            gp_p, ga_p, wpo, wao, wo, g1, b1, wr_pad, br_pad, tm=MOBA_BLOCK)

        p_s, q_s, k_s, v_s, gp_s, ga_s = _inproj(xs, w_in_bf, tm=ns_rows, attn_layouts=False)
        state = state_pool[:, l]
        prev_s = jnp.concatenate([jnp.zeros((db, 1, POOL_WIDTH), F32), state], axis=1)
        p_s3 = p_s.reshape(db, ts, POOL_WIDTH)
        p_s8 = jnp.pad(p_s3, ((0, 0), (0, 8 - ts), (0, 0)))
        yp_s = _pool(p_s8, prev_s, pool_w_bf, scale, tq=8, pos0=past_len)[:, :ts]
        qbd = jnp.where(head_mask[None], jnp.repeat(q_s.reshape(db, ts, ATT_WIDTH), N_HEADS, axis=1), 0.0)
        knew8 = jnp.pad(k_s.reshape(db, ts, ATT_WIDTH), ((0, 0), (0, 8 - ts), (0, 0)))
        vnew8 = jnp.pad(v_s.reshape(db, ts, ATT_WIDTH), ((0, 0), (0, 8 - ts), (0, 0)))
        o_s = _moba_sample(page_table, qbd, knew8, vnew8, cfar_s, b_last_s, b_own_s,
                           cache_kt, cache_vt, l)[:, :ts]
        x1_s, rt_s = _merge(xs, yp_s.reshape(ns_rows, POOL_WIDTH), o_s.reshape(ns_rows, ATT_WIDTH),
                            gp_s, ga_s, wpo, wao, wo, g1, b1, wr_pad, br_pad, tm=ns_rows)

        x1 = jnp.concatenate([x1_p, x1_s], axis=0)
        e_p, gt_p = _unpack_route(rt_p)
        e_s, gt_s = _unpack_route(rt_s)
        y0, y1 = _moe(x1, jnp.concatenate([e_p, e_s], axis=0), w_gate, w_up, w_down, l)
        xp, xs = _ln2(x1_p, x1_s, y0, y1, jnp.concatenate([gt_p, gt_s], axis=0), g2, b2, tm=ROW_TILE)

        kp.append(k_p)
        vp.append(v_p)
        pp.append(p_p.reshape(bsz, seq, POOL_WIDTH)[:, seq - POOL_BUF:])
        ksm.append(k_s.reshape(db, ts, N_HEADS, HEAD_DIM))
        vsm.append(v_s.reshape(db, ts, N_HEADS, HEAD_DIM))
        psm.append(jnp.concatenate([state, p_s3], axis=1)[:, -POOL_BUF:])

    def rows_out(parts):
        a = jnp.stack(parts, axis=1).reshape(bsz, DEPTH, N_HEADS, HEAD_DIM, seq)
        return jnp.transpose(a, (0, 4, 1, 2, 3))

    return (xp.reshape(bsz, seq, D_MODEL), xs.reshape(db, ts, D_MODEL),
            rows_out(kp), rows_out(vp), jnp.stack(pp, axis=1),
            jnp.stack(ksm, axis=2), jnp.stack(vsm, axis=2), jnp.stack(psm, axis=1))
```

```python
import functools
import math

import numpy as np
import jax
import jax.numpy as jnp
from jax import lax
from jax.experimental import pallas as pl
from jax.experimental.pallas import tpu as pltpu

F32 = jnp.float32
BF16 = jnp.bfloat16
NEG_INF = float("-inf")

D_MODEL = 1024
DEPTH = 2
N_HEADS = 8
HEAD_DIM = 64
ATT_WIDTH = N_HEADS * HEAD_DIM
MOBA_BLOCK = 256
MOBA_TOPK = 3
N_BUCKETS = 32
REL_MAX_DIST = 128
POOL_WINDOWS = (2, 4, 8, 16)
POOL_WIDTH = 512
POOL_GW = 128
POOL_BUF = 15
POOL_HALO = 16
IN_WIDTH = POOL_WIDTH + 3 * ATT_WIDTH + 2 * D_MODEL
N_EXPERTS = 16
N_GROUPS = 4
EXPERTS_PER_GROUP = 4
TOP_K = 2
D_EXPERT = 512
EXPERT_ROWS = 512
ROW_TILE = 128
PROMPT_ROW_TILE = 512
DEEPNORM_ALPHA = (2 * DEPTH) ** 0.25
LN_EPS = 1e-5
PAGE_SIZE = 128
ROUTER_PAD = 128
LANES = 128

VMEM_LIMIT = 52 * 1024 * 1024


def _cparams(sem):
    return pltpu.CompilerParams(dimension_semantics=sem, vmem_limit_bytes=VMEM_LIMIT)


def _nt_dot(a, b):
    return lax.dot_general(a, b, (((1,), (1,)), ((), ())), preferred_element_type=F32)


def _top_mask(scores, cand, k, n):
    idx = lax.broadcasted_iota(jnp.int32, scores.shape, 0)
    sel = jnp.zeros(scores.shape, jnp.bool_)
    for _ in range(k):
        c = cand & jnp.logical_not(sel)
        cur = jnp.where(c, scores, NEG_INF)
        mx = jnp.max(cur, axis=0, keepdims=True)
        first = jnp.min(jnp.where(c & (cur == mx), idx, n), axis=0, keepdims=True)
        sel = sel | (idx == first)
    return sel


def _inproj_body(x_ref, w_ref, p_ref, q_ref, k_ref, v_ref, gp_ref, ga_ref, *attn_refs, tm):
    xb = x_ref[...].astype(BF16)

    def seg(a, b):
        return jnp.dot(xb, w_ref[:, a:b], preferred_element_type=F32)

    c0 = POOL_WIDTH
    p_ref[...] = seg(0, c0)
    q_ref[...] = seg(c0, c0 + ATT_WIDTH) * (HEAD_DIM ** -0.5)
    k = seg(c0 + ATT_WIDTH, c0 + 2 * ATT_WIDTH)
    v = seg(c0 + 2 * ATT_WIDTH, c0 + 3 * ATT_WIDTH)
    g0 = c0 + 3 * ATT_WIDTH
    gp_ref[...] = jax.nn.sigmoid(seg(g0, g0 + D_MODEL))
    ga_ref[...] = jax.nn.sigmoid(seg(g0 + D_MODEL, g0 + 2 * D_MODEL))
    if attn_refs:
        kb_ref, vt_ref, km_ref = attn_refs
        k_ref[...] = k.T
        vt = v.T
        v_ref[...] = vt
        kb_ref[...] = k.astype(BF16)
        for blk in range(tm // MOBA_BLOCK):
            rows = slice(blk * MOBA_BLOCK, (blk + 1) * MOBA_BLOCK)
            vt_ref[blk] = vt[:, rows].astype(BF16)
            km_ref[blk] = jnp.sum(k[rows, :], axis=0, keepdims=True) * (1.0 / MOBA_BLOCK)
    else:
        k_ref[...] = k
        v_ref[...] = v


def _inproj(x, w_bf, *, tm, attn_layouts, seq=None):
    m = x.shape[0]
    nt = m // tm
    row = lambda n: pl.BlockSpec((tm, n), lambda i: (i, 0))
    out_shape = [jax.ShapeDtypeStruct((m, POOL_WIDTH), F32),
                 jax.ShapeDtypeStruct((m, ATT_WIDTH), F32),
                 jax.ShapeDtypeStruct((m, ATT_WIDTH), F32),
                 jax.ShapeDtypeStruct((m, ATT_WIDTH), F32),
                 jax.ShapeDtypeStruct((m, D_MODEL), F32),
                 jax.ShapeDtypeStruct((m, D_MODEL), F32)]
    out_specs = [row(POOL_WIDTH), row(ATT_WIDTH), row(ATT_WIDTH), row(ATT_WIDTH),
                 row(D_MODEL), row(D_MODEL)]
    if attn_layouts:
        assert tm % MOBA_BLOCK == 0 and seq % tm == 0
        tps = seq // tm
        bpt = tm // MOBA_BLOCK
        for idx in (2, 3):
            out_shape[idx] = jax.ShapeDtypeStruct((m // seq, ATT_WIDTH, seq), F32)
            out_specs[idx] = pl.BlockSpec((None, ATT_WIDTH, tm), lambda i: (i // tps, 0, i % tps))
        out_shape += [jax.ShapeDtypeStruct((m, ATT_WIDTH), BF16),
                      jax.ShapeDtypeStruct((nt * bpt, ATT_WIDTH, MOBA_BLOCK), BF16),
                      jax.ShapeDtypeStruct((nt * bpt, 1, ATT_WIDTH), F32)]
        out_specs += [row(ATT_WIDTH),
                      pl.BlockSpec((bpt, ATT_WIDTH, MOBA_BLOCK), lambda i: (i, 0, 0)),
                      pl.BlockSpec((bpt, 1, ATT_WIDTH), lambda i: (i, 0, 0))]
    return pl.pallas_call(
        functools.partial(_inproj_body, tm=tm),
        grid=(nt,),
        in_specs=[row(D_MODEL), pl.BlockSpec((D_MODEL, IN_WIDTH), lambda i: (0, 0))],
        out_specs=out_specs,
        out_shape=out_shape,
        compiler_params=_cparams(("arbitrary",)),
        name="inproj",
    )(x, w_bf)


def _pool_body(prev_ref, halo_ref, p_ref, pw_ref, sc_ref, y_ref, *, tq, pos0):
    i = pl.program_id(1)
    p = p_ref[0]
    halo = jnp.where(i == 0, prev_ref[0], halo_ref[0])
    z = jnp.concatenate([halo, p], axis=0)
    lane = lax.broadcasted_iota(jnp.int32, z.shape, 1)
    x = z
    for s in (8, 4, 2, 1):
        thr = POOL_WIDTH - POOL_GW * {8: 1, 4: 2, 2: 3, 1: 4}[s]
        sh = pltpu.roll(x, s, 0)
        x = x + (jnp.where(lane >= thr, sh, 0.0) if thr > 0 else sh)
    wsum = x[POOL_HALO:, :]
    row = lax.broadcasted_iota(jnp.int32, (tq, POOL_WIDTH), 0)
    lane2 = lax.broadcasted_iota(jnp.int32, (tq, POOL_WIDTH), 1)
    wl = jnp.where(lane2 < POOL_GW, POOL_WINDOWS[0],
                   jnp.where(lane2 < 2 * POOL_GW, POOL_WINDOWS[1],
                             jnp.where(lane2 < 3 * POOL_GW, POOL_WINDOWS[2], POOL_WINDOWS[3])))
    pos = pos0 + i * tq + row
    cnt = jnp.minimum(pos + 1, wl).astype(F32)
    d = (wsum / cnt - p).astype(BF16)
    ys = [jnp.dot(d[:, g * POOL_GW:(g + 1) * POOL_GW], pw_ref[g], preferred_element_type=F32)
          for g in range(len(POOL_WINDOWS))]
    y_ref[0] = jnp.concatenate(ys, axis=1) * sc_ref[...]


def _pool(p, prev16, pool_w_bf, scale, *, tq, pos0):
    b, t, c = p.shape
    nq = t // tq
    hb = tq // POOL_HALO
    return pl.pallas_call(
        functools.partial(_pool_body, tq=tq, pos0=pos0),
        grid=(b, nq),
        in_specs=[pl.BlockSpec((1, POOL_HALO, c), lambda bi, i: (bi, 0, 0)),
                  pl.BlockSpec((1, POOL_HALO, c), lambda bi, i: (bi, jnp.maximum(i * hb - 1, 0), 0)),
                  pl.BlockSpec((1, tq, c), lambda bi, i: (bi, i, 0)),
                  pl.BlockSpec((len(POOL_WINDOWS), POOL_GW, POOL_GW), lambda bi, i: (0, 0, 0)),
                  pl.BlockSpec((1, c), lambda bi, i: (0, 0))],
        out_specs=pl.BlockSpec((1, tq, c), lambda bi, i: (bi, i, 0)),
        out_shape=jax.ShapeDtypeStruct((b, t, c), F32),
        compiler_params=_cparams(("arbitrary", "arbitrary")),
        name="pool",
    )(prev16, p, p, pool_w_bf, scale)


def _bucket_np(dist):
    n = np.maximum(dist, 0)
    max_exact = N_BUCKETS // 2
    nf = np.maximum(n, 1).astype(np.float32)
    large = max_exact + (np.log(nf / np.float32(max_exact)) / np.float32(math.log(REL_MAX_DIST / max_exact))
                         * np.float32(N_BUCKETS - max_exact)).astype(np.int32)
    large = np.minimum(large, N_BUCKETS - 1)
    return np.where(n < max_exact, n, large)


def _far_bucket(min_dist, max_dist):
    b = _bucket_np(np.arange(min_dist, max_dist + 1))
    assert (b == b[0]).all()
    return int(b[0])


def _moba_prompt_body(rb_ref, q_ref, kb_ref, vt_ref, km_ref, io_ref, ip_ref, o_ref,
                      sel_ref, qz_ref, m_ref, l_ref, acc_ref, bown_ref, bprev_ref, *, nb, far):
    i = pl.program_id(1)
    tq = MOBA_BLOCK

    @pl.when((pl.program_id(0) == 0) & (i == 0))
    def _():
        for h in range(N_HEADS):
            bown_ref[h] = jnp.full((tq, tq), NEG_INF, F32)
            bprev_ref[h] = jnp.zeros((tq, tq), F32)
        io = io_ref[...]
        ip = ip_ref[...]

        def fill(bkt, carry):
            mo = io == bkt
            mp = ip == bkt
            for h in range(N_HEADS):
                val = rb_ref[bkt, h] - rb_ref[far, h]
                bown_ref[h] = jnp.where(mo, val, bown_ref[h])
                bprev_ref[h] = jnp.where(mp, val, bprev_ref[h])
            return carry

        lax.fori_loop(0, N_BUCKETS, fill, 0)

    blk = lax.broadcasted_iota(jnp.int32, (nb, tq), 0)
    valid = blk < i
    lane = lax.broadcasted_iota(jnp.int32, (tq, LANES), 1)
    for hp in range(N_HEADS // 2):
        qf = q_ref[0, :, LANES * hp:LANES * (hp + 1)]
        km = km_ref[0, :, LANES * hp:LANES * (hp + 1)].astype(BF16)
        for hh in range(2):
            h = 2 * hp + hh
            qz = jnp.where((lane >= HEAD_DIM * hh) & (lane < HEAD_DIM * (hh + 1)), qf, 0.0).astype(BF16)
            qz_ref[h] = qz
            sel = _top_mask(_nt_dot(km, qz), valid, MOBA_TOPK, nb)
            selb = jnp.where(sel, 0.0, NEG_INF)
            for jj in range(nb):
                sel_ref[h, jj] = selb[jj:jj + 1, :]
            m_ref[h] = jnp.full((1, tq), -1e30, F32)
            l_ref[h] = jnp.zeros((1, tq), F32)
            acc_ref[h] = jnp.zeros((HEAD_DIM, tq), F32)

    def key_blocks(items):
        sts = [[_nt_dot(kb_ref[0, j, :, LANES * (h // 2):LANES * (h // 2 + 1)], qz_ref[h])
                for h in range(N_HEADS)] for j, _, _ in items]
        ps, alphas = [], []
        for h in range(N_HEADS):
            m = m_ref[h]
            m_new = m
            shifted = []
            for idx, (j, bias_ref, masked) in enumerate(items):
                st = sts[idx][h] if bias_ref is None else sts[idx][h] + bias_ref[h]
                cm = jnp.max(st, axis=0, keepdims=True)
                mb = sel_ref[h, j] if masked else None
                m_new = jnp.maximum(m_new, cm + mb if masked else cm)
                shifted.append((st, mb))
            alpha = jnp.exp(m - m_new)
            lsum = alpha * l_ref[h]
            ph = []
            for st, mb in shifted:
                p = jnp.exp(st - m_new) if mb is None else jnp.exp(st + (mb - m_new))
                lsum = lsum + jnp.sum(p, axis=0, keepdims=True)
                ph.append(p.astype(BF16))
            l_ref[h] = lsum
            m_ref[h] = m_new
            ps.append(ph)
            alphas.append(alpha)
        for h in range(N_HEADS):
            acc = alphas[h] * acc_ref[h]
            for idx, (j, _, _) in enumerate(items):
                vt = vt_ref[0, j, HEAD_DIM * h:HEAD_DIM * (h + 1), :]
                acc = acc + jnp.dot(vt, ps[h][idx], preferred_element_type=F32)
            acc_ref[h] = acc

    n_far = jnp.maximum(i - 1, 0)

    def far_body(jj, carry):
        key_blocks([(2 * jj, None, True), (2 * jj + 1, None, True)])
        return carry

    lax.fori_loop(0, n_far // 2, far_body, 0)

    @pl.when(n_far % 2 == 1)
    def _():
        key_blocks([(n_far - 1, None, True)])

    jp = jnp.maximum(i - 1, 0)
    key_blocks([(jp, bprev_ref, True), (i, bown_ref, False)])
    for hp in range(N_HEADS // 2):
        pair = jnp.concatenate([acc_ref[2 * hp] / l_ref[2 * hp], acc_ref[2 * hp + 1] / l_ref[2 * hp + 1]],
                               axis=0)
        o_ref[0, :, LANES * hp:LANES * (hp + 1)] = pair.T


def _moba_prompt(q, kb, vt, km, rel_bias):
    b, t, _ = q.shape
    nb = t // MOBA_BLOCK
    far = _far_bucket(MOBA_BLOCK + 1, t)
    c = np.arange(MOBA_BLOCK)
    d_own = c[None, :] - c[:, None]
    idx_own = jnp.asarray(np.where(d_own >= 0, _bucket_np(d_own), -1).astype(np.int32))
    idx_prev = jnp.asarray(_bucket_np(d_own + MOBA_BLOCK).astype(np.int32))
    once = pl.Buffered(1)
    tile = pl.BlockSpec((MOBA_BLOCK, MOBA_BLOCK), lambda bi, i: (0, 0), pipeline_mode=once)
    return pl.pallas_call(
        functools.partial(_moba_prompt_body, nb=nb, far=far),
        grid=(b, nb),
        in_specs=[pl.BlockSpec(memory_space=pltpu.SMEM),
                  pl.BlockSpec((1, MOBA_BLOCK, ATT_WIDTH), lambda bi, i: (bi, i, 0)),
                  pl.BlockSpec((1, nb, MOBA_BLOCK, ATT_WIDTH), lambda bi, i: (bi, 0, 0, 0), pipeline_mode=once),
                  pl.BlockSpec((1, nb, ATT_WIDTH, MOBA_BLOCK), lambda bi, i: (bi, 0, 0, 0), pipeline_mode=once),
                  pl.BlockSpec((1, nb, ATT_WIDTH), lambda bi, i: (bi, 0, 0)),
                  tile, tile],
        out_specs=pl.BlockSpec((1, MOBA_BLOCK, ATT_WIDTH), lambda bi, i: (bi, i, 0)),
        out_shape=jax.ShapeDtypeStruct((b, t, ATT_WIDTH), F32),
        scratch_shapes=[pltpu.VMEM((N_HEADS, nb, 1, MOBA_BLOCK), F32),
                        pltpu.VMEM((N_HEADS, MOBA_BLOCK, LANES), BF16),
                        pltpu.VMEM((N_HEADS, 1, MOBA_BLOCK), F32),
                        pltpu.VMEM((N_HEADS, 1, MOBA_BLOCK), F32),
                        pltpu.VMEM((N_HEADS, HEAD_DIM, MOBA_BLOCK), F32),
                        pltpu.VMEM((N_HEADS, MOBA_BLOCK, MOBA_BLOCK), F32),
                        pltpu.VMEM((N_HEADS, MOBA_BLOCK, MOBA_BLOCK), F32)],
        compiler_params=_cparams(("arbitrary", "arbitrary")),
        name="moba_prompt",
    )(rel_bias, q, kb, vt, km, idx_own, idx_prev)


SAMPLE_PAGES_PER_STEP = 16
PAGES_PER_BLOCK = MOBA_BLOCK // PAGE_SIZE


def _moba_sample_body(pt_ref, qbd_ref, knew_ref, vnew_ref, cfar_ref, blast_ref, bown_ref, ck_hbm, cv_hbm,
                      o_ref, gs_sc, m_sc, l_sc, o_sc, kbuf, vbuf, sem_k, sem_v, *, nbs, ts, layer):
    npg = SAMPLE_PAGES_PER_STEP
    ppb = PAGES_PER_BLOCK
    b = pl.program_id(0)
    s = pl.program_id(1)
    nsteps = pl.num_programs(1)
    g = b * nsteps + s
    cur = g % 2
    bps = npg // ppb
    nr = ts * N_HEADS

    def page_copies(bb, ss, slot):
        out = []
        for u in range(npg):
            page = pt_ref[bb, ss * npg + u]
            out.append(pltpu.make_async_copy(ck_hbm.at[page, layer], kbuf.at[slot, u], sem_k.at[slot]))
            out.append(pltpu.make_async_copy(cv_hbm.at[page, layer], vbuf.at[slot, u], sem_v.at[slot]))
        return out

    @pl.when(g == 0)
    def _():
        for c in page_copies(b, s, 0):
            c.start()

    @pl.when(g + 1 < pl.num_programs(0) * nsteps)
    def _():
        wrap = s + 1 == nsteps
        for c in page_copies(jnp.where(wrap, b + 1, b), jnp.where(wrap, 0, s + 1), 1 - cur):
            c.start()

    for c in page_copies(b, s, cur):
        c.wait()
    k_refs = [kbuf.at[cur, u] for u in range(npg)]
    v_refs = [vbuf.at[cur, u] for u in range(npg)]
    qb = qbd_ref[0].astype(BF16)
    lane0 = lax.broadcasted_iota(jnp.int32, (ATT_WIDTH, LANES), 1) == 0
    sts = []
    for jj in range(bps):
        kts = [k_refs[ppb * jj + u][...].reshape(ATT_WIDTH, PAGE_SIZE) for u in range(ppb)]
        ksum = functools.reduce(lambda a, b: a + b, [jnp.sum(kt, axis=1, keepdims=True) for kt in kts])
        kmean = ksum * (1.0 / MOBA_BLOCK)
        kmcol = jnp.where(lane0, kmean, 0.0).astype(BF16)
        w = jnp.concatenate([kt.astype(BF16) for kt in kts] + [kmcol], axis=1)
        sts.append(jnp.dot(qb, w, preferred_element_type=F32))
    ps = []
    for jj in range(bps):
        j = s * bps + jj
        gs_sc[j] = sts[jj][:, MOBA_BLOCK:MOBA_BLOCK + 1]
        bias = jnp.where(j == nbs - 1, blast_ref[...], cfar_ref[...])
        st = sts[jj][:, :MOBA_BLOCK] + bias
        m = jnp.max(st, axis=1, keepdims=True)
        p = jnp.exp(st - m)
        m_sc[j] = m
        l_sc[j] = jnp.sum(p, axis=1, keepdims=True)
        ps.append(p.astype(BF16))
    for jj in range(bps):
        o = None
        for u in range(ppb):
            vt = v_refs[ppb * jj + u][...].reshape(ATT_WIDTH, PAGE_SIZE).astype(BF16)
            part = _nt_dot(ps[jj][:, PAGE_SIZE * u:PAGE_SIZE * (u + 1)], vt)
            o = part if o is None else o + part
        o_sc[s * bps + jj] = o

    @pl.when(s == nsteps - 1)
    def _():
        qf = qb.astype(F32)
        gs = gs_sc[...]
        sel = _top_mask(gs, jnp.ones(gs.shape, jnp.bool_), MOBA_TOPK, nbs)
        knew = knew_ref[0].astype(BF16).astype(F32)
        vnew = vnew_ref[0].astype(BF16).astype(F32)
        s_own = [jnp.sum(qf * knew[t:t + 1, :], axis=1, keepdims=True) + bown_ref[:, t:t + 1]
                 for t in range(ts)]
        m_all = m_sc[...]
        mtot = jnp.max(jnp.where(sel, m_all, NEG_INF), axis=0)
        for t in range(ts):
            mtot = jnp.maximum(mtot, s_own[t])
        w = jnp.where(sel, jnp.exp(m_all - mtot[None]), 0.0)
        ltot = jnp.sum(w * l_sc[...], axis=0)
        gs_sc[...] = w

        def merge(j, acc):
            return acc + gs_sc[j] * o_sc[j]

        otot = lax.fori_loop(0, nbs, merge, jnp.zeros((nr, ATT_WIDTH), F32))
        for t in range(ts):
            pt = jnp.exp(s_own[t] - mtot)
            ltot = ltot + pt
            otot = otot + pt.astype(BF16).astype(F32) * vnew[t:t + 1, :]
        out = otot / ltot
        row = lax.broadcasted_iota(jnp.int32, out.shape, 0)
        lane = lax.broadcasted_iota(jnp.int32, out.shape, 1)
        out = jnp.where((lane // HEAD_DIM) == (row % N_HEADS), out, 0.0)
        pieces = [jnp.sum(out[N_HEADS * t:N_HEADS * (t + 1), :], axis=0, keepdims=True) for t in range(ts)]
        pieces.append(jnp.zeros((8 - ts, ATT_WIDTH), F32))
        o_ref[0] = jnp.concatenate(pieces, axis=0)


def _moba_sample(page_table, qbd, knew8, vnew8, cfar_rows, b_last, b_own, cache_kt, cache_vt, layer):
    db, nr, _ = qbd.shape
    ts = nr // N_HEADS
    n_pages = page_table.shape[1]
    nbs = n_pages // PAGES_PER_BLOCK
    npg = SAMPLE_PAGES_PER_STEP
    nsteps = n_pages // npg

    full2 = lambda shp: pl.BlockSpec(shp, lambda b, s, pt: (0, 0))
    per_b = lambda r: pl.BlockSpec((1, r, ATT_WIDTH), lambda b, s, pt: (b, 0, 0))
    page_buf = pltpu.VMEM((2, npg, N_HEADS, HEAD_DIM, PAGE_SIZE), F32)
    grid_spec = pltpu.PrefetchScalarGridSpec(
        num_scalar_prefetch=1,
        grid=(db, nsteps),
        in_specs=[per_b(nr), per_b(8), per_b(8),
                  full2((nr, 1)), full2((nr, MOBA_BLOCK)), full2((nr, 8)),
                  pl.BlockSpec(memory_space=pl.ANY), pl.BlockSpec(memory_space=pl.ANY)],
        out_specs=per_b(8),
        scratch_shapes=[pltpu.VMEM((nbs, nr, 1), F32), pltpu.VMEM((nbs, nr, 1), F32),
                        pltpu.VMEM((nbs, nr, 1), F32), pltpu.VMEM((nbs, nr, ATT_WIDTH), F32),
                        page_buf, page_buf, pltpu.SemaphoreType.DMA((2,)), pltpu.SemaphoreType.DMA((2,))],
    )
    return pl.pallas_call(
        functools.partial(_moba_sample_body, nbs=nbs, ts=ts, layer=layer),
        grid_spec=grid_spec,
        out_shape=jax.ShapeDtypeStruct((db, 8, ATT_WIDTH), F32),
        compiler_params=_cparams(("arbitrary", "arbitrary")),
        name="moba_sample",
    )(page_table, qbd, knew8, vnew8, cfar_rows, b_last, b_own, cache_kt, cache_vt)


def _layer_norm(h, g, b):
    mu = jnp.mean(h, axis=-1, keepdims=True)
    c = h - mu
    var = jnp.mean(c * c, axis=-1, keepdims=True)
    return c * lax.rsqrt(var + LN_EPS) * g + b


def _argmax_first(vals):
    best, idx = vals[0], jnp.zeros(vals[0].shape, jnp.int32)
    for k in range(1, len(vals)):
        upd = vals[k] > best
        idx = jnp.where(upd, k, idx)
        best = jnp.where(upd, vals[k], best)
    return best, idx


def _route_rows(logit_rows):
    mx = functools.reduce(jnp.maximum, logit_rows)
    ex = [jnp.exp(r - mx) for r in logit_rows]
    tot = functools.reduce(lambda a, b: a + b, ex)
    probs = [e / tot for e in ex]
    scores = []
    for g in range(N_GROUPS):
        a, b, c, d = probs[EXPERTS_PER_GROUP * g:EXPERTS_PER_GROUP * (g + 1)]
        s1, t1 = jnp.maximum(a, b), jnp.minimum(a, b)
        s2, t2 = jnp.maximum(c, d), jnp.minimum(c, d)
        scores.append(jnp.maximum(s1, s2) + jnp.maximum(jnp.minimum(s1, s2), jnp.maximum(t1, t2)))
    _, gi = _argmax_first(scores)
    ing = []
    for j in range(EXPERTS_PER_GROUP):
        v = probs[j]
        for g in range(1, N_GROUPS):
            v = jnp.where(gi == g, probs[EXPERTS_PER_GROUP * g + j], v)
        ing.append(v)
    w1, i1 = _argmax_first(ing)
    w2, i2 = _argmax_first([jnp.where(i1 == j, -1.0, ing[j]) for j in range(EXPERTS_PER_GROUP)])
    den = w1 + w2
    e1 = (gi * EXPERTS_PER_GROUP + i1).astype(F32)
    e2 = (gi * EXPERTS_PER_GROUP + i2).astype(F32)
    return e1, e2, w1 / den, w2 / den


def _merge_body(x_ref, yp_ref, o_ref, gp_ref, ga_ref, wpo_ref, wao_ref, wo_ref, g_ref, b_ref,
                wr_ref, br_ref, x1_ref, rt_ref, *, tm):
    a = jnp.dot(yp_ref[...].astype(BF16), wpo_ref[...], preferred_element_type=F32)
    bb = jnp.dot(o_ref[...].astype(BF16), wao_ref[...], preferred_element_type=F32)
    merged = gp_ref[...] * a + ga_ref[...] * bb
    mix = jnp.dot(merged.astype(BF16), wo_ref[...], preferred_element_type=F32)
    x1 = _layer_norm(DEEPNORM_ALPHA * x_ref[...] + mix, g_ref[...], b_ref[...])
    x1_ref[...] = x1
    lg = jnp.dot(x1.astype(BF16), wr_ref[...], preferred_element_type=F32) + br_ref[...]
    lgt = lg.T
    e1, e2, g1, g2 = _route_rows([lgt[e:e + 1, :] for e in range(N_EXPERTS)])
    r = lax.broadcasted_iota(jnp.int32, (8, tm), 0)
    rt_ref[0] = jnp.where(r == 0, e1, jnp.where(r == 1, e2, jnp.where(r == 2, g1, jnp.where(r == 3, g2, 0.0))))


def _merge(x, yp, o, gp, ga, wpo, wao, wo, g, b, wr, br, *, tm):
    m = x.shape[0]
    row = lambda n: pl.BlockSpec((tm, n), lambda i: (i, 0))
    full = lambda r, c: pl.BlockSpec((r, c), lambda i: (0, 0))
    return pl.pallas_call(
        functools.partial(_merge_body, tm=tm),
        grid=(m // tm,),
        in_specs=[row(D_MODEL), row(POOL_WIDTH), row(ATT_WIDTH), row(D_MODEL), row(D_MODEL),
                  full(POOL_WIDTH, D_MODEL), full(ATT_WIDTH, D_MODEL), full(D_MODEL, D_MODEL),
                  full(1, D_MODEL), full(1, D_MODEL), full(D_MODEL, ROUTER_PAD), full(1, ROUTER_PAD)],
        out_specs=[row(D_MODEL), pl.BlockSpec((1, 8, tm), lambda i: (i, 0, 0))],
        out_shape=[jax.ShapeDtypeStruct((m, D_MODEL), F32), jax.ShapeDtypeStruct((m // tm, 8, tm), F32)],
        compiler_params=_cparams(("arbitrary",)),
        name="merge",
    )(x, yp, o, gp, ga, wpo, wao, wo, g, b, wr, br)


def _unpack_route(rt):
    n = rt.shape[0] * rt.shape[2]
    cols = rt[:, :4, :].transpose(0, 2, 1).reshape(n, 4)
    return cols[:, :2].astype(jnp.int32), cols[:, 2:]


def _experts_body(be_ref, xs_ref, wg_ref, wu_ref, wd_ref, ys_ref, wg_sc, wu_sc, wd_sc):
    i = pl.program_id(0)

    @pl.when((i == 0) | (be_ref[i] != be_ref[jnp.maximum(i - 1, 0)]))
    def _():
        wg_sc[...] = wg_ref[0].astype(BF16)
        wu_sc[...] = wu_ref[0].astype(BF16)
        wd_sc[...] = wd_ref[0].astype(BF16)

    xb = xs_ref[...].astype(BF16)
    h1 = jnp.dot(xb, wg_sc[...], preferred_element_type=F32)
    h2 = jnp.dot(xb, wu_sc[...], preferred_element_type=F32)
    act = (h1 * jax.nn.sigmoid(h1) * h2).astype(BF16)
    ys_ref[...] = jnp.dot(act, wd_sc[...], preferred_element_type=F32)


def _experts(blk_expert, xs, wg, wu, wd, layer):
    n_rows = xs.shape[0]
    n_blk = n_rows // EXPERT_ROWS
    grid_spec = pltpu.PrefetchScalarGridSpec(
        num_scalar_prefetch=1,
        grid=(n_blk,),
        in_specs=[pl.BlockSpec((EXPERT_ROWS, D_MODEL), lambda i, be: (i, 0)),
                  pl.BlockSpec((None, 1, D_MODEL, D_EXPERT), lambda i, be: (layer, be[i], 0, 0)),
                  pl.BlockSpec((None, 1, D_MODEL, D_EXPERT), lambda i, be: (layer, be[i], 0, 0)),
                  pl.BlockSpec((None, 1, D_EXPERT, D_MODEL), lambda i, be: (layer, be[i], 0, 0))],
        out_specs=pl.BlockSpec((EXPERT_ROWS, D_MODEL), lambda i, be: (i, 0)),
        scratch_shapes=[pltpu.VMEM((D_MODEL, D_EXPERT), BF16), pltpu.VMEM((D_MODEL, D_EXPERT), BF16),
                        pltpu.VMEM((D_EXPERT, D_MODEL), BF16)],
    )
    return pl.pallas_call(
        _experts_body,
        grid_spec=grid_spec,
        out_shape=jax.ShapeDtypeStruct((n_rows, D_MODEL), F32),
        compiler_params=_cparams(("arbitrary",)),
        name="experts",
    )(blk_expert, xs, wg, wu, wd)


def _ln2_body(xp_ref, xs_ref, y0_ref, y1_ref, gt_ref, g_ref, b_ref, op_ref, os_ref, *, n_ptiles):
    i = pl.program_id(0)
    gt = gt_ref[...]
    f = gt[:, 0:1] * y0_ref[...] + gt[:, 1:2] * y1_ref[...]

    @pl.when(i < n_ptiles)
    def _():
        op_ref[...] = _layer_norm(DEEPNORM_ALPHA * xp_ref[...] + f, g_ref[...], b_ref[...])

    @pl.when(i >= n_ptiles)
    def _():
        os_ref[...] = _layer_norm(DEEPNORM_ALPHA * xs_ref[...] + f, g_ref[...], b_ref[...])


def _ln2(x_p, x_s, y0, y1, gate, g, b, *, tm):
    n_p, n_s = x_p.shape[0], x_s.shape[0]
    n_ptiles, n_stiles = n_p // tm, n_s // tm
    assert n_p % tm == 0 and n_s % tm == 0
    row = lambda n: pl.BlockSpec((tm, n), lambda i: (i, 0))
    p_row = pl.BlockSpec((tm, D_MODEL), lambda i: (jnp.minimum(i, n_ptiles - 1), 0))
    s_row = pl.BlockSpec((tm, D_MODEL), lambda i: (jnp.maximum(i - n_ptiles, 0), 0))
    vec = pl.BlockSpec((1, D_MODEL), lambda i: (0, 0))
    return pl.pallas_call(
        functools.partial(_ln2_body, n_ptiles=n_ptiles),
        grid=(n_ptiles + n_stiles,),
        in_specs=[p_row, s_row, row(D_MODEL), row(D_MODEL), row(TOP_K), vec, vec],
        out_specs=[p_row, s_row],
        out_shape=[jax.ShapeDtypeStruct((n_p, D_MODEL), F32), jax.ShapeDtypeStruct((n_s, D_MODEL), F32)],
        compiler_params=_cparams(("arbitrary",)),
        name="ln2",
    )(x_p, x_s, y0, y1, gate, g, b)


def _moe(h, expert, wg, wu, wd, layer):
    n = h.shape[0]
    s = n * TOP_K
    e_flat = expert.reshape(-1)
    onehot = (e_flat[:, None] == jnp.arange(N_EXPERTS, dtype=jnp.int32)[None, :]).astype(jnp.int32)
    csum = jnp.cumsum(onehot, axis=0)
    counts = csum[-1]
    pos_in = jnp.take_along_axis(csum, e_flat[:, None], axis=1)[:, 0] - 1
    padded = (counts + EXPERT_ROWS - 1) // EXPERT_ROWS * EXPERT_ROWS
    pad_end = jnp.cumsum(padded)
    pad_start = pad_end - padded
    dest = pad_start[e_flat] + pos_in
    n_blk = (s + N_EXPERTS * (EXPERT_ROWS - 1) + EXPERT_ROWS - 1) // EXPERT_ROWS
    n_rows = n_blk * EXPERT_ROWS
    tok = jnp.arange(s, dtype=jnp.int32) // TOP_K
    row_tok = (jnp.arange(n_rows, dtype=jnp.int32) % n).at[dest].set(tok, unique_indices=True)
    xs = h[row_tok]
    blk_start = jnp.arange(n_blk, dtype=jnp.int32) * EXPERT_ROWS
    blk_expert = jnp.minimum(jnp.sum((pad_end[None, :] <= blk_start[:, None]).astype(jnp.int32), axis=1),
                             N_EXPERTS - 1)
    ys = _experts(blk_expert, xs, wg, wu, wd, layer)
    dest2 = dest.reshape(n, TOP_K)
    return ys[dest2[:, 0]], ys[dest2[:, 1]]


def kernel(x_prompt, x_sample, cache_k, cache_v, state_pool, page_table, rel_bias, w_router, b_router,
           w_in, pool_w, pool_scale, w_pool_out, w_attn_out, w_o, ln1_g, ln1_b, w_gate, w_up, w_down,
           ln2_g, ln2_b):
    bsz, seq, _ = x_prompt.shape
    db, ts, _ = x_sample.shape
    past_len = page_table.shape[1] * PAGE_SIZE
    nb = seq // MOBA_BLOCK
    nbs = past_len // MOBA_BLOCK
    assert seq % MOBA_BLOCK == 0 and past_len % MOBA_BLOCK == 0 and ts <= 8
    np_rows = bsz * seq
    ns_rows = db * ts

    far_s = _far_bucket(MOBA_BLOCK + 1, past_len + ts)
    c = np.arange(MOBA_BLOCK)
    t_idx = np.repeat(np.arange(ts), N_HEADS)
    h_idx = np.tile(np.arange(N_HEADS), ts)
    d_last = MOBA_BLOCK + t_idx[:, None] - c[None, :]
    b_last_s = rel_bias[_bucket_np(d_last), h_idx[:, None]]
    tn = np.arange(8)
    d_new = t_idx[:, None] - tn[None, :]
    b_own_s = jnp.where((d_new >= 0) & (tn[None, :] < ts),
                        rel_bias[_bucket_np(np.maximum(d_new, 0)), h_idx[:, None]], NEG_INF)
    cfar_s = rel_bias[far_s][h_idx][:, None]
    head_mask = jnp.asarray((np.arange(ATT_WIDTH)[None, :] // HEAD_DIM) == h_idx[:, None])
    cache_kt = jnp.transpose(cache_k, (0, 2, 3, 4, 1))
    cache_vt = jnp.transpose(cache_v, (0, 2, 3, 4, 1))

    wr_pad = jnp.zeros((D_MODEL, ROUTER_PAD), BF16).at[:, :N_EXPERTS].set(w_router.astype(BF16))
    br_pad = jnp.zeros((1, ROUTER_PAD), F32).at[0, :N_EXPERTS].set(b_router)

    xp = x_prompt.reshape(np_rows, D_MODEL)
    xs = x_sample.reshape(ns_rows, D_MODEL)
    kp, vp, pp, ksm, vsm, psm = [], [], [], [], [], []
    for l in range(DEPTH):
        w_in_bf = w_in[l].astype(BF16)
        pool_w_bf = pool_w[l].astype(BF16)
        scale = pool_scale[l][None, :]
        wpo = w_pool_out[l].astype(BF16)
        wao = w_attn_out[l].astype(BF16)
        wo = w_o[l].astype(BF16)
        g1, b1 = ln1_g[l][None, :], ln1_b[l][None, :]
        g2, b2 = ln2_g[l][None, :], ln2_b[l][None, :]

        p_p, q_p, k_p, v_p, gp_p, ga_p, kb_p, vt_p, km_p = _inproj(xp, w_in_bf, tm=PROMPT_ROW_TILE,
                                                                   attn_layouts=True, seq=seq)
        prev0 = jnp.zeros((bsz, POOL_HALO, POOL_WIDTH), F32)
        yp_p = _pool(p_p.reshape(bsz, seq, POOL_WIDTH), prev0, pool_w_bf, scale, tq=MOBA_BLOCK, pos0=0)
        o_p = _moba_prompt(q_p.reshape(bsz, seq, ATT_WIDTH),
                           kb_p.reshape(bsz, nb, MOBA_BLOCK, ATT_WIDTH),
                           vt_p.reshape(bsz, nb, ATT_WIDTH, MOBA_BLOCK),
                           km_p.reshape(bsz, nb, ATT_WIDTH), rel_bias)
        x1_p, rt_p = _merge(xp, yp_p.reshape(np_rows, POOL_WIDTH), o_p.reshape(np_rows, ATT_WIDTH),
                            gp_p, ga_p, wpo, wao, wo, g1, b1, wr_pad, br_pad, tm=PROMPT_ROW_TILE)

        p_s, q_s, k_s, v_s, gp_s, ga_s = _inproj(xs, w_in_bf, tm=ns_rows, attn_layouts=False)
        state = state_pool[:, l]
        prev_s = jnp.concatenate([jnp.zeros((db, 1, POOL_WIDTH), F32), state], axis=1)
        p_s3 = p_s.reshape(db, ts, POOL_WIDTH)
        p_s8 = jnp.pad(p_s3, ((0, 0), (0, 8 - ts), (0, 0)))
        yp_s = _pool(p_s8, prev_s, pool_w_bf, scale, tq=8, pos0=past_len)[:, :ts]
        qbd = jnp.where(head_mask[None], jnp.repeat(q_s.reshape(db, ts, ATT_WIDTH), N_HEADS, axis=1), 0.0)
        knew8 = jnp.pad(k_s.reshape(db, ts, ATT_WIDTH), ((0, 0), (0, 8 - ts), (0, 0)))
        vnew8 = jnp.pad(v_s.reshape(db, ts, ATT_WIDTH), ((0, 0), (0, 8 - ts), (0, 0)))
        o_s = _moba_sample(page_table, qbd, knew8, vnew8, cfar_s, b_last_s, b_own_s,
                           cache_kt, cache_vt, l)[:, :ts]
        x1_s, rt_s = _merge(xs, yp_s.reshape(ns_rows, POOL_WIDTH), o_s.reshape(ns_rows, ATT_WIDTH),
                            gp_s, ga_s, wpo, wao, wo, g1, b1, wr_pad, br_pad, tm=ns_rows)

        x1 = jnp.concatenate([x1_p, x1_s], axis=0)
        e_p, gt_p = _unpack_route(rt_p)
        e_s, gt_s = _unpack_route(rt_s)
        y0, y1 = _moe(x1, jnp.concatenate([e_p, e_s], axis=0), w_gate, w_up, w_down, l)
        xp, xs = _ln2(x1_p, x1_s, y0, y1, jnp.concatenate([gt_p, gt_s], axis=0), g2, b2, tm=ROW_TILE)

        kp.append(k_p)
        vp.append(v_p)
        pp.append(p_p.reshape(bsz, seq, POOL_WIDTH)[:, seq - POOL_BUF:])
        ksm.append(k_s.reshape(db, ts, N_HEADS, HEAD_DIM))
        vsm.append(v_s.reshape(db, ts, N_HEADS, HEAD_DIM))
        psm.append(jnp.concatenate([state, p_s3], axis=1)[:, -POOL_BUF:])

    def rows_out(parts):
        a = jnp.stack(parts, axis=1).reshape(bsz, DEPTH, N_HEADS, HEAD_DIM, seq)
        return jnp.transpose(a, (0, 4, 1, 2, 3))

    return (xp.reshape(bsz, seq, D_MODEL), xs.reshape(db, ts, D_MODEL),
            rows_out(kp), rows_out(vp), jnp.stack(pp, axis=1),
            jnp.stack(ksm, axis=2), jnp.stack(vsm, axis=2), jnp.stack(psm, axis=1))
```

```python
import functools
import math

import numpy as np
import jax
import jax.numpy as jnp
from jax import lax
from jax.experimental import pallas as pl
from jax.experimental.pallas import tpu as pltpu

F32 = jnp.float32
BF16 = jnp.bfloat16
NEG_INF = float("-inf")

D_MODEL = 1024
DEPTH = 2
N_HEADS = 8
HEAD_DIM = 64
ATT_WIDTH = N_HEADS * HEAD_DIM
MOBA_BLOCK = 256
MOBA_TOPK = 3
N_BUCKETS = 32
REL_MAX_DIST = 128
POOL_WINDOWS = (2, 4, 8, 16)
POOL_WIDTH = 512
POOL_GW = 128
POOL_BUF = 15
POOL_HALO = 16
IN_WIDTH = POOL_WIDTH + 3 * ATT_WIDTH + 2 * D_MODEL
N_EXPERTS = 16
N_GROUPS = 4
EXPERTS_PER_GROUP = 4
TOP_K = 2
D_EXPERT = 512
EXPERT_ROWS = 512
ROW_TILE = 128
PROMPT_ROW_TILE = 512
DEEPNORM_ALPHA = (2 * DEPTH) ** 0.25
LN_EPS = 1e-5
PAGE_SIZE = 128
ROUTER_PAD = 128
LANES = 128

VMEM_LIMIT = 52 * 1024 * 1024


def _cparams(sem):
    return pltpu.CompilerParams(dimension_semantics=sem, vmem_limit_bytes=VMEM_LIMIT)


def _nt_dot(a, b):
    return lax.dot_general(a, b, (((1,), (1,)), ((), ())), preferred_element_type=F32)


def _top_mask(scores, cand, k, n):
    idx = lax.broadcasted_iota(jnp.int32, scores.shape, 0)
    sel = jnp.zeros(scores.shape, jnp.bool_)
    for _ in range(k):
        c = cand & jnp.logical_not(sel)
        cur = jnp.where(c, scores, NEG_INF)
        mx = jnp.max(cur, axis=0, keepdims=True)
        first = jnp.min(jnp.where(c & (cur == mx), idx, n), axis=0, keepdims=True)
        sel = sel | (idx == first)
    return sel


def _inproj_body(x_ref, w_ref, *refs, tm, n_alias):
    p_ref, q_ref, k_ref, v_ref, gp_ref, ga_ref, *attn_refs = refs[n_alias:]
    xb = x_ref[...].astype(BF16)

    def seg(a, b):
        return jnp.dot(xb, w_ref[:, a:b], preferred_element_type=F32)

    c0 = POOL_WIDTH
    p_ref[...] = seg(0, c0)
    q_ref[...] = seg(c0, c0 + ATT_WIDTH) * (HEAD_DIM ** -0.5)
    k = seg(c0 + ATT_WIDTH, c0 + 2 * ATT_WIDTH)
    v = seg(c0 + 2 * ATT_WIDTH, c0 + 3 * ATT_WIDTH)
    g0 = c0 + 3 * ATT_WIDTH
    gp_ref[...] = jax.nn.sigmoid(seg(g0, g0 + D_MODEL))
    ga_ref[...] = jax.nn.sigmoid(seg(g0 + D_MODEL, g0 + 2 * D_MODEL))
    if attn_refs:
        kb_ref, vt_ref, km_ref = attn_refs
        vt = v.T
        if n_alias:
            k_ref[...] = k.T
            v_ref[...] = vt
        else:
            k_ref[0] = k.T
            v_ref[0] = vt
            for other in range(1, DEPTH):
                k_ref[other] = jnp.zeros((ATT_WIDTH, tm), F32)
                v_ref[other] = jnp.zeros((ATT_WIDTH, tm), F32)
        kb_ref[...] = k.astype(BF16)
        for blk in range(tm // MOBA_BLOCK):
            rows = slice(blk * MOBA_BLOCK, (blk + 1) * MOBA_BLOCK)
            vt_ref[blk] = vt[:, rows].astype(BF16)
            km_ref[blk] = jnp.sum(k[rows, :], axis=0, keepdims=True) * (1.0 / MOBA_BLOCK)
    else:
        k_ref[...] = k
        v_ref[...] = v


def _inproj(x, w_bf, *, tm, attn_layouts, seq=None, layer=0, kv_all=None):
    m = x.shape[0]
    nt = m // tm
    row = lambda n: pl.BlockSpec((tm, n), lambda i: (i, 0))
    out_shape = [jax.ShapeDtypeStruct((m, POOL_WIDTH), F32),
                 jax.ShapeDtypeStruct((m, ATT_WIDTH), F32),
                 jax.ShapeDtypeStruct((m, ATT_WIDTH), F32),
                 jax.ShapeDtypeStruct((m, ATT_WIDTH), F32),
                 jax.ShapeDtypeStruct((m, D_MODEL), F32),
                 jax.ShapeDtypeStruct((m, D_MODEL), F32)]
    out_specs = [row(POOL_WIDTH), row(ATT_WIDTH), row(ATT_WIDTH), row(ATT_WIDTH),
                 row(D_MODEL), row(D_MODEL)]
    if attn_layouts:
        assert tm % MOBA_BLOCK == 0 and seq % tm == 0
        tps = seq // tm
        bpt = tm // MOBA_BLOCK
        assert (layer == 0) == (kv_all is None)
        for idx in (2, 3):
            out_shape[idx] = jax.ShapeDtypeStruct((m // seq, DEPTH, ATT_WIDTH, seq), F32)
            if layer == 0:
                out_specs[idx] = pl.BlockSpec((None, DEPTH, ATT_WIDTH, tm), lambda i: (i // tps, 0, 0, i % tps))
            else:
                out_specs[idx] = pl.BlockSpec((None, None, ATT_WIDTH, tm),
                                              lambda i: (i // tps, layer, 0, i % tps))
        out_shape += [jax.ShapeDtypeStruct((m, ATT_WIDTH), BF16),
                      jax.ShapeDtypeStruct((nt * bpt, ATT_WIDTH, MOBA_BLOCK), BF16),
                      jax.ShapeDtypeStruct((nt * bpt, 1, ATT_WIDTH), F32)]
        out_specs += [row(ATT_WIDTH),
                      pl.BlockSpec((bpt, ATT_WIDTH, MOBA_BLOCK), lambda i: (i, 0, 0)),
                      pl.BlockSpec((bpt, 1, ATT_WIDTH), lambda i: (i, 0, 0))]
    aliased = list(kv_all) if kv_all is not None else []
    return pl.pallas_call(
        functools.partial(_inproj_body, tm=tm, n_alias=len(aliased)),
        grid=(nt,),
        in_specs=[row(D_MODEL), pl.BlockSpec((D_MODEL, IN_WIDTH), lambda i: (0, 0))]
                 + [pl.BlockSpec(memory_space=pl.ANY) for _ in aliased],
        out_specs=out_specs,
        out_shape=out_shape,
        input_output_aliases={2 + a: 2 + a for a in range(len(aliased))},
        compiler_params=_cparams(("arbitrary",)),
        name="inproj",
    )(x, w_bf, *aliased)


def _pool_body(prev_ref, halo_ref, p_ref, pw_ref, sc_ref, y_ref, *, tq, pos0):
    i = pl.program_id(1)
    p = p_ref[0]
    halo = jnp.where(i == 0, prev_ref[0], halo_ref[0])
    z = jnp.concatenate([halo, p], axis=0)
    lane = lax.broadcasted_iota(jnp.int32, z.shape, 1)
    x = z
    for s in (8, 4, 2, 1):
        thr = POOL_WIDTH - POOL_GW * {8: 1, 4: 2, 2: 3, 1: 4}[s]
        sh = pltpu.roll(x, s, 0)
        x = x + (jnp.where(lane >= thr, sh, 0.0) if thr > 0 else sh)
    wsum = x[POOL_HALO:, :]
    row = lax.broadcasted_iota(jnp.int32, (tq, POOL_WIDTH), 0)
    lane2 = lax.broadcasted_iota(jnp.int32, (tq, POOL_WIDTH), 1)
    wl = jnp.where(lane2 < POOL_GW, POOL_WINDOWS[0],
                   jnp.where(lane2 < 2 * POOL_GW, POOL_WINDOWS[1],
                             jnp.where(lane2 < 3 * POOL_GW, POOL_WINDOWS[2], POOL_WINDOWS[3])))
    pos = pos0 + i * tq + row
    cnt = jnp.minimum(pos + 1, wl).astype(F32)
    d = (wsum / cnt - p).astype(BF16)
    ys = [jnp.dot(d[:, g * POOL_GW:(g + 1) * POOL_GW], pw_ref[g], preferred_element_type=F32)
          for g in range(len(POOL_WINDOWS))]
    y_ref[0] = jnp.concatenate(ys, axis=1) * sc_ref[...]


def _pool(p, prev16, pool_w_bf, scale, *, tq, pos0):
    b, t, c = p.shape
    nq = t // tq
    hb = tq // POOL_HALO
    return pl.pallas_call(
        functools.partial(_pool_body, tq=tq, pos0=pos0),
        grid=(b, nq),
        in_specs=[pl.BlockSpec((1, POOL_HALO, c), lambda bi, i: (bi, 0, 0)),
                  pl.BlockSpec((1, POOL_HALO, c), lambda bi, i: (bi, jnp.maximum(i * hb - 1, 0), 0)),
                  pl.BlockSpec((1, tq, c), lambda bi, i: (bi, i, 0)),
                  pl.BlockSpec((len(POOL_WINDOWS), POOL_GW, POOL_GW), lambda bi, i: (0, 0, 0)),
                  pl.BlockSpec((1, c), lambda bi, i: (0, 0))],
        out_specs=pl.BlockSpec((1, tq, c), lambda bi, i: (bi, i, 0)),
        out_shape=jax.ShapeDtypeStruct((b, t, c), F32),
        compiler_params=_cparams(("arbitrary", "arbitrary")),
        name="pool",
    )(prev16, p, p, pool_w_bf, scale)


def _bucket_np(dist):
    n = np.maximum(dist, 0)
    max_exact = N_BUCKETS // 2
    nf = np.maximum(n, 1).astype(np.float32)
    large = max_exact + (np.log(nf / np.float32(max_exact)) / np.float32(math.log(REL_MAX_DIST / max_exact))
                         * np.float32(N_BUCKETS - max_exact)).astype(np.int32)
    large = np.minimum(large, N_BUCKETS - 1)
    return np.where(n < max_exact, n, large)


def _far_bucket(min_dist, max_dist):
    b = _bucket_np(np.arange(min_dist, max_dist + 1))
    assert (b == b[0]).all()
    return int(b[0])


def _moba_prompt_body(rb_ref, q_ref, kb_ref, vt_ref, km_ref, io_ref, ip_ref, o_ref,
                      sel_ref, qz_ref, m_ref, l_ref, acc_ref, bown_ref, bprev_ref, *, nb, far):
    i = pl.program_id(1)
    tq = MOBA_BLOCK

    @pl.when((pl.program_id(0) == 0) & (i == 0))
    def _():
        for h in range(N_HEADS):
            bown_ref[h] = jnp.full((tq, tq), NEG_INF, F32)
            bprev_ref[h] = jnp.zeros((tq, tq), F32)
        io = io_ref[...]
        ip = ip_ref[...]

        def fill(bkt, carry):
            mo = io == bkt
            mp = ip == bkt
            for h in range(N_HEADS):
                val = rb_ref[bkt, h] - rb_ref[far, h]
                bown_ref[h] = jnp.where(mo, val, bown_ref[h])
                bprev_ref[h] = jnp.where(mp, val, bprev_ref[h])
            return carry

        lax.fori_loop(0, N_BUCKETS, fill, 0)

    blk = lax.broadcasted_iota(jnp.int32, (nb, tq), 0)
    valid = blk < i
    lane = lax.broadcasted_iota(jnp.int32, (tq, LANES), 1)
    for hp in range(N_HEADS // 2):
        qf = q_ref[0, :, LANES * hp:LANES * (hp + 1)]
        km = km_ref[0, :, LANES * hp:LANES * (hp + 1)].astype(BF16)
        for hh in range(2):
            h = 2 * hp + hh
            qz = jnp.where((lane >= HEAD_DIM * hh) & (lane < HEAD_DIM * (hh + 1)), qf, 0.0).astype(BF16)
            qz_ref[h] = qz
            sel = _top_mask(_nt_dot(km, qz), valid, MOBA_TOPK, nb)
            selb = jnp.where(sel, 0.0, NEG_INF)
            for jj in range(nb):
                sel_ref[h, jj] = selb[jj:jj + 1, :]
            m_ref[h] = jnp.full((1, tq), -1e30, F32)
            l_ref[h] = jnp.zeros((1, tq), F32)
            acc_ref[h] = jnp.zeros((HEAD_DIM, tq), F32)

    def key_blocks(items):
        sts = [[_nt_dot(kb_ref[0, j, :, LANES * (h // 2):LANES * (h // 2 + 1)], qz_ref[h])
                for h in range(N_HEADS)] for j, _, _ in items]
        ps, alphas = [], []
        for h in range(N_HEADS):
            m = m_ref[h]
            m_new = m
            shifted = []
            for idx, (j, bias_ref, masked) in enumerate(items):
                st = sts[idx][h] if bias_ref is None else sts[idx][h] + bias_ref[h]
                cm = jnp.max(st, axis=0, keepdims=True)
                mb = sel_ref[h, j] if masked else None
                m_new = jnp.maximum(m_new, cm + mb if masked else cm)
                shifted.append((st, mb))
            alpha = jnp.exp(m - m_new)
            lsum = alpha * l_ref[h]
            ph = []
            for st, mb in shifted:
                p = jnp.exp(st - m_new) if mb is None else jnp.exp(st + (mb - m_new))
                lsum = lsum + jnp.sum(p, axis=0, keepdims=True)
                ph.append(p.astype(BF16))
            l_ref[h] = lsum
            m_ref[h] = m_new
            ps.append(ph)
            alphas.append(alpha)
        for h in range(N_HEADS):
            acc = alphas[h] * acc_ref[h]
            for idx, (j, _, _) in enumerate(items):
                vt = vt_ref[0, j, HEAD_DIM * h:HEAD_DIM * (h + 1), :]
                acc = acc + jnp.dot(vt, ps[h][idx], preferred_element_type=F32)
            acc_ref[h] = acc

    n_far = jnp.maximum(i - 1, 0)

    def far_body(jj, carry):
        key_blocks([(2 * jj, None, True), (2 * jj + 1, None, True)])
        return carry

    lax.fori_loop(0, n_far // 2, far_body, 0)

    @pl.when(n_far % 2 == 1)
    def _():
        key_blocks([(n_far - 1, None, True)])

    jp = jnp.maximum(i - 1, 0)
    key_blocks([(jp, bprev_ref, True), (i, bown_ref, False)])
    for hp in range(N_HEADS // 2):
        pair = jnp.concatenate([acc_ref[2 * hp] / l_ref[2 * hp], acc_ref[2 * hp + 1] / l_ref[2 * hp + 1]],
                               axis=0)
        o_ref[0, :, LANES * hp:LANES * (hp + 1)] = pair.T


def _moba_prompt(q, kb, vt, km, rel_bias):
    b, t, _ = q.shape
    nb = t // MOBA_BLOCK
    far = _far_bucket(MOBA_BLOCK + 1, t)
    c = np.arange(MOBA_BLOCK)
    d_own = c[None, :] - c[:, None]
    idx_own = jnp.asarray(np.where(d_own >= 0, _bucket_np(d_own), -1).astype(np.int32))
    idx_prev = jnp.asarray(_bucket_np(d_own + MOBA_BLOCK).astype(np.int32))
    once = pl.Buffered(1)
    tile = pl.BlockSpec((MOBA_BLOCK, MOBA_BLOCK), lambda bi, i: (0, 0), pipeline_mode=once)
    return pl.pallas_call(
        functools.partial(_moba_prompt_body, nb=nb, far=far),
        grid=(b, nb),
        in_specs=[pl.BlockSpec(memory_space=pltpu.SMEM),
                  pl.BlockSpec((1, MOBA_BLOCK, ATT_WIDTH), lambda bi, i: (bi, i, 0)),
                  pl.BlockSpec((1, nb, MOBA_BLOCK, ATT_WIDTH), lambda bi, i: (bi, 0, 0, 0), pipeline_mode=once),
                  pl.BlockSpec((1, nb, ATT_WIDTH, MOBA_BLOCK), lambda bi, i: (bi, 0, 0, 0), pipeline_mode=once),
                  pl.BlockSpec((1, nb, ATT_WIDTH), lambda bi, i: (bi, 0, 0)),
                  tile, tile],
        out_specs=pl.BlockSpec((1, MOBA_BLOCK, ATT_WIDTH), lambda bi, i: (bi, i, 0)),
        out_shape=jax.ShapeDtypeStruct((b, t, ATT_WIDTH), F32),
        scratch_shapes=[pltpu.VMEM((N_HEADS, nb, 1, MOBA_BLOCK), F32),
                        pltpu.VMEM((N_HEADS, MOBA_BLOCK, LANES), BF16),
                        pltpu.VMEM((N_HEADS, 1, MOBA_BLOCK), F32),
                        pltpu.VMEM((N_HEADS, 1, MOBA_BLOCK), F32),
                        pltpu.VMEM((N_HEADS, HEAD_DIM, MOBA_BLOCK), F32),
                        pltpu.VMEM((N_HEADS, MOBA_BLOCK, MOBA_BLOCK), F32),
                        pltpu.VMEM((N_HEADS, MOBA_BLOCK, MOBA_BLOCK), F32)],
        compiler_params=_cparams(("arbitrary", "arbitrary")),
        name="moba_prompt",
    )(rel_bias, q, kb, vt, km, idx_own, idx_prev)


SAMPLE_PAGES_PER_STEP = 16
PAGES_PER_BLOCK = MOBA_BLOCK // PAGE_SIZE


def _moba_sample_body(pt_ref, qbd_ref, knew_ref, vnew_ref, cfar_ref, blast_ref, bown_ref, ck_hbm, cv_hbm,
                      o_ref, gs_sc, m_sc, l_sc, o_sc, kbuf, vbuf, sem_k, sem_v, *, nbs, ts, layer):
    npg = SAMPLE_PAGES_PER_STEP
    ppb = PAGES_PER_BLOCK
    b = pl.program_id(0)
    s = pl.program_id(1)
    nsteps = pl.num_programs(1)
    g = b * nsteps + s
    cur = g % 2
    bps = npg // ppb
    nr = ts * N_HEADS

    def page_copies(bb, ss, slot):
        out = []
        for u in range(npg):
            page = pt_ref[bb, ss * npg + u]
            out.append(pltpu.make_async_copy(ck_hbm.at[page, layer], kbuf.at[slot, u], sem_k.at[slot]))
            out.append(pltpu.make_async_copy(cv_hbm.at[page, layer], vbuf.at[slot, u], sem_v.at[slot]))
        return out

    @pl.when(g == 0)
    def _():
        for c in page_copies(b, s, 0):
            c.start()

    @pl.when(g + 1 < pl.num_programs(0) * nsteps)
    def _():
        wrap = s + 1 == nsteps
        for c in page_copies(jnp.where(wrap, b + 1, b), jnp.where(wrap, 0, s + 1), 1 - cur):
            c.start()

    for c in page_copies(b, s, cur):
        c.wait()
    k_refs = [kbuf.at[cur, u] for u in range(npg)]
    v_refs = [vbuf.at[cur, u] for u in range(npg)]
    qb = qbd_ref[0].astype(BF16)
    lane0 = lax.broadcasted_iota(jnp.int32, (ATT_WIDTH, LANES), 1) == 0
    sts = []
    for jj in range(bps):
        kts = [k_refs[ppb * jj + u][...].reshape(ATT_WIDTH, PAGE_SIZE) for u in range(ppb)]
        ksum = functools.reduce(lambda a, b: a + b, [jnp.sum(kt, axis=1, keepdims=True) for kt in kts])
        kmean = ksum * (1.0 / MOBA_BLOCK)
        kmcol = jnp.where(lane0, kmean, 0.0).astype(BF16)
        w = jnp.concatenate([kt.astype(BF16) for kt in kts] + [kmcol], axis=1)
        sts.append(jnp.dot(qb, w, preferred_element_type=F32))
    ps = []
    for jj in range(bps):
        j = s * bps + jj
        gs_sc[j] = sts[jj][:, MOBA_BLOCK:MOBA_BLOCK + 1]
        bias = jnp.where(j == nbs - 1, blast_ref[...], cfar_ref[...])
        st = sts[jj][:, :MOBA_BLOCK] + bias
        m = jnp.max(st, axis=1, keepdims=True)
        p = jnp.exp(st - m)
        m_sc[j] = m
        l_sc[j] = jnp.sum(p, axis=1, keepdims=True)
        ps.append(p.astype(BF16))
    for jj in range(bps):
        o = None
        for u in range(ppb):
            vt = v_refs[ppb * jj + u][...].reshape(ATT_WIDTH, PAGE_SIZE).astype(BF16)
            part = _nt_dot(ps[jj][:, PAGE_SIZE * u:PAGE_SIZE * (u + 1)], vt)
            o = part if o is None else o + part
        o_sc[s * bps + jj] = o

    @pl.when(s == nsteps - 1)
    def _():
        qf = qb.astype(F32)
        gs = gs_sc[...]
        sel = _top_mask(gs, jnp.ones(gs.shape, jnp.bool_), MOBA_TOPK, nbs)
        knew = knew_ref[0].astype(BF16).astype(F32)
        vnew = vnew_ref[0].astype(BF16).astype(F32)
        s_own = [jnp.sum(qf * knew[t:t + 1, :], axis=1, keepdims=True) + bown_ref[:, t:t + 1]
                 for t in range(ts)]
        m_all = m_sc[...]
        mtot = jnp.max(jnp.where(sel, m_all, NEG_INF), axis=0)
        for t in range(ts):
            mtot = jnp.maximum(mtot, s_own[t])
        w = jnp.where(sel, jnp.exp(m_all - mtot[None]), 0.0)
        ltot = jnp.sum(w * l_sc[...], axis=0)
        gs_sc[...] = w

        def merge(j, acc):
            return acc + gs_sc[j] * o_sc[j]

        otot = lax.fori_loop(0, nbs, merge, jnp.zeros((nr, ATT_WIDTH), F32))
        for t in range(ts):
            pt = jnp.exp(s_own[t] - mtot)
            ltot = ltot + pt
            otot = otot + pt.astype(BF16).astype(F32) * vnew[t:t + 1, :]
        out = otot / ltot
        row = lax.broadcasted_iota(jnp.int32, out.shape, 0)
        lane = lax.broadcasted_iota(jnp.int32, out.shape, 1)
        out = jnp.where((lane // HEAD_DIM) == (row % N_HEADS), out, 0.0)
        pieces = [jnp.sum(out[N_HEADS * t:N_HEADS * (t + 1), :], axis=0, keepdims=True) for t in range(ts)]
        pieces.append(jnp.zeros((8 - ts, ATT_WIDTH), F32))
        o_ref[0] = jnp.concatenate(pieces, axis=0)


def _moba_sample(page_table, qbd, knew8, vnew8, cfar_rows, b_last, b_own, cache_kt, cache_vt, layer):
    db, nr, _ = qbd.shape
    ts = nr // N_HEADS
    n_pages = page_table.shape[1]
    nbs = n_pages // PAGES_PER_BLOCK
    npg = SAMPLE_PAGES_PER_STEP
    nsteps = n_pages // npg

    full2 = lambda shp: pl.BlockSpec(shp, lambda b, s, pt: (0, 0))
    per_b = lambda r: pl.BlockSpec((1, r, ATT_WIDTH), lambda b, s, pt: (b, 0, 0))
    page_buf = pltpu.VMEM((2, npg, N_HEADS, HEAD_DIM, PAGE_SIZE), F32)
    grid_spec = pltpu.PrefetchScalarGridSpec(
        num_scalar_prefetch=1,
        grid=(db, nsteps),
        in_specs=[per_b(nr), per_b(8), per_b(8),
                  full2((nr, 1)), full2((nr, MOBA_BLOCK)), full2((nr, 8)),
                  pl.BlockSpec(memory_space=pl.ANY), pl.BlockSpec(memory_space=pl.ANY)],
        out_specs=per_b(8),
        scratch_shapes=[pltpu.VMEM((nbs, nr, 1), F32), pltpu.VMEM((nbs, nr, 1), F32),
                        pltpu.VMEM((nbs, nr, 1), F32), pltpu.VMEM((nbs, nr, ATT_WIDTH), F32),
                        page_buf, page_buf, pltpu.SemaphoreType.DMA((2,)), pltpu.SemaphoreType.DMA((2,))],
    )
    return pl.pallas_call(
        functools.partial(_moba_sample_body, nbs=nbs, ts=ts, layer=layer),
        grid_spec=grid_spec,
        out_shape=jax.ShapeDtypeStruct((db, 8, ATT_WIDTH), F32),
        compiler_params=_cparams(("arbitrary", "arbitrary")),
        name="moba_sample",
    )(page_table, qbd, knew8, vnew8, cfar_rows, b_last, b_own, cache_kt, cache_vt)


def _layer_norm(h, g, b):
    mu = jnp.mean(h, axis=-1, keepdims=True)
    c = h - mu
    var = jnp.mean(c * c, axis=-1, keepdims=True)
    return c * lax.rsqrt(var + LN_EPS) * g + b


def _argmax_first(vals):
    best, idx = vals[0], jnp.zeros(vals[0].shape, jnp.int32)
    for k in range(1, len(vals)):
        upd = vals[k] > best
        idx = jnp.where(upd, k, idx)
        best = jnp.where(upd, vals[k], best)
    return best, idx


def _route_rows(logit_rows):
    mx = functools.reduce(jnp.maximum, logit_rows)
    ex = [jnp.exp(r - mx) for r in logit_rows]
    tot = functools.reduce(lambda a, b: a + b, ex)
    probs = [e / tot for e in ex]
    scores = []
    for g in range(N_GROUPS):
        a, b, c, d = probs[EXPERTS_PER_GROUP * g:EXPERTS_PER_GROUP * (g + 1)]
        s1, t1 = jnp.maximum(a, b), jnp.minimum(a, b)
        s2, t2 = jnp.maximum(c, d), jnp.minimum(c, d)
        scores.append(jnp.maximum(s1, s2) + jnp.maximum(jnp.minimum(s1, s2), jnp.maximum(t1, t2)))
    _, gi = _argmax_first(scores)
    ing = []
    for j in range(EXPERTS_PER_GROUP):
        v = probs[j]
        for g in range(1, N_GROUPS):
            v = jnp.where(gi == g, probs[EXPERTS_PER_GROUP * g + j], v)
        ing.append(v)
    w1, i1 = _argmax_first(ing)
    w2, i2 = _argmax_first([jnp.where(i1 == j, -1.0, ing[j]) for j in range(EXPERTS_PER_GROUP)])
    den = w1 + w2
    e1 = (gi * EXPERTS_PER_GROUP + i1).astype(F32)
    e2 = (gi * EXPERTS_PER_GROUP + i2).astype(F32)
    return e1, e2, w1 / den, w2 / den


def _merge_body(x_ref, yp_ref, o_ref, gp_ref, ga_ref, wpo_ref, wao_ref, wo_ref, g_ref, b_ref,
                wr_ref, br_ref, x1_ref, x1b_ref, rt_ref, *, tm):
    a = jnp.dot(yp_ref[...].astype(BF16), wpo_ref[...], preferred_element_type=F32)
    bb = jnp.dot(o_ref[...].astype(BF16), wao_ref[...], preferred_element_type=F32)
    merged = gp_ref[...] * a + ga_ref[...] * bb
    mix = jnp.dot(merged.astype(BF16), wo_ref[...], preferred_element_type=F32)
    x1 = _layer_norm(DEEPNORM_ALPHA * x_ref[...] + mix, g_ref[...], b_ref[...])
    x1_ref[...] = x1
    x1b = x1.astype(BF16)
    x1b_ref[...] = x1b
    lg = jnp.dot(x1b, wr_ref[...], preferred_element_type=F32) + br_ref[...]
    lgt = lg.T
    e1, e2, g1, g2 = _route_rows([lgt[e:e + 1, :] for e in range(N_EXPERTS)])
    r = lax.broadcasted_iota(jnp.int32, (8, tm), 0)
    rt_ref[0] = jnp.where(r == 0, e1, jnp.where(r == 1, e2, jnp.where(r == 2, g1, jnp.where(r == 3, g2, 0.0))))


def _merge(x, yp, o, gp, ga, wpo, wao, wo, g, b, wr, br, *, tm):
    m = x.shape[0]
    row = lambda n: pl.BlockSpec((tm, n), lambda i: (i, 0))
    full = lambda r, c: pl.BlockSpec((r, c), lambda i: (0, 0))
    return pl.pallas_call(
        functools.partial(_merge_body, tm=tm),
        grid=(m // tm,),
        in_specs=[row(D_MODEL), row(POOL_WIDTH), row(ATT_WIDTH), row(D_MODEL), row(D_MODEL),
                  full(POOL_WIDTH, D_MODEL), full(ATT_WIDTH, D_MODEL), full(D_MODEL, D_MODEL),
                  full(1, D_MODEL), full(1, D_MODEL), full(D_MODEL, ROUTER_PAD), full(1, ROUTER_PAD)],
        out_specs=[row(D_MODEL), row(D_MODEL), pl.BlockSpec((1, 8, tm), lambda i: (i, 0, 0))],
        out_shape=[jax.ShapeDtypeStruct((m, D_MODEL), F32), jax.ShapeDtypeStruct((m, D_MODEL), BF16),
                   jax.ShapeDtypeStruct((m // tm, 8, tm), F32)],
        compiler_params=_cparams(("arbitrary",)),
        name="merge",
    )(x, yp, o, gp, ga, wpo, wao, wo, g, b, wr, br)


def _unpack_route(rt):
    n = rt.shape[0] * rt.shape[2]
    cols = rt[:, :4, :].transpose(0, 2, 1).reshape(n, 4)
    return cols[:, :2].astype(jnp.int32), cols[:, 2:]


def _experts_body(be_ref, xs_ref, wg_ref, wu_ref, wd_ref, ys_ref, wg_sc, wu_sc, wd_sc):
    i = pl.program_id(0)

    @pl.when((i == 0) | (be_ref[i] != be_ref[jnp.maximum(i - 1, 0)]))
    def _():
        wg_sc[...] = wg_ref[0].astype(BF16)
        wu_sc[...] = wu_ref[0].astype(BF16)
        wd_sc[...] = wd_ref[0].astype(BF16)

    xb = xs_ref[...]
    h1 = jnp.dot(xb, wg_sc[...], preferred_element_type=F32)
    h2 = jnp.dot(xb, wu_sc[...], preferred_element_type=F32)
    act = (h1 * jax.nn.sigmoid(h1) * h2).astype(BF16)
    ys_ref[...] = jnp.dot(act, wd_sc[...], preferred_element_type=F32)


def _experts(blk_expert, xs, wg, wu, wd, layer):
    n_rows = xs.shape[0]
    n_blk = n_rows // EXPERT_ROWS
    grid_spec = pltpu.PrefetchScalarGridSpec(
        num_scalar_prefetch=1,
        grid=(n_blk,),
        in_specs=[pl.BlockSpec((EXPERT_ROWS, D_MODEL), lambda i, be: (i, 0)),
                  pl.BlockSpec((None, 1, D_MODEL, D_EXPERT), lambda i, be: (layer, be[i], 0, 0)),
                  pl.BlockSpec((None, 1, D_MODEL, D_EXPERT), lambda i, be: (layer, be[i], 0, 0)),
                  pl.BlockSpec((None, 1, D_EXPERT, D_MODEL), lambda i, be: (layer, be[i], 0, 0))],
        out_specs=pl.BlockSpec((EXPERT_ROWS, D_MODEL), lambda i, be: (i, 0)),
        scratch_shapes=[pltpu.VMEM((D_MODEL, D_EXPERT), BF16), pltpu.VMEM((D_MODEL, D_EXPERT), BF16),
                        pltpu.VMEM((D_EXPERT, D_MODEL), BF16)],
    )
    return pl.pallas_call(
        _experts_body,
        grid_spec=grid_spec,
        out_shape=jax.ShapeDtypeStruct((n_rows, D_MODEL), F32),
        compiler_params=_cparams(("arbitrary",)),
        name="experts",
    )(blk_expert, xs, wg, wu, wd)


def _ln2_body(xp_ref, xs_ref, y0_ref, y1_ref, gt_ref, g_ref, b_ref, op_ref, os_ref, *, n_ptiles):
    i = pl.program_id(0)
    gt = gt_ref[...]
    f = gt[:, 0:1] * y0_ref[...] + gt[:, 1:2] * y1_ref[...]

    @pl.when(i < n_ptiles)
    def _():
        op_ref[...] = _layer_norm(DEEPNORM_ALPHA * xp_ref[...] + f, g_ref[...], b_ref[...])

    @pl.when(i >= n_ptiles)
    def _():
        os_ref[...] = _layer_norm(DEEPNORM_ALPHA * xs_ref[...] + f, g_ref[...], b_ref[...])


def _ln2(x_p, x_s, y0, y1, gate, g, b, *, tm):
    n_p, n_s = x_p.shape[0], x_s.shape[0]
    n_ptiles, n_stiles = n_p // tm, n_s // tm
    assert n_p % tm == 0 and n_s % tm == 0
    row = lambda n: pl.BlockSpec((tm, n), lambda i: (i, 0))
    p_row = pl.BlockSpec((tm, D_MODEL), lambda i: (jnp.minimum(i, n_ptiles - 1), 0))
    s_row = pl.BlockSpec((tm, D_MODEL), lambda i: (jnp.maximum(i - n_ptiles, 0), 0))
    vec = pl.BlockSpec((1, D_MODEL), lambda i: (0, 0))
    return pl.pallas_call(
        functools.partial(_ln2_body, n_ptiles=n_ptiles),
        grid=(n_ptiles + n_stiles,),
        in_specs=[p_row, s_row, row(D_MODEL), row(D_MODEL), row(TOP_K), vec, vec],
        out_specs=[p_row, s_row],
        out_shape=[jax.ShapeDtypeStruct((n_p, D_MODEL), F32), jax.ShapeDtypeStruct((n_s, D_MODEL), F32)],
        compiler_params=_cparams(("arbitrary",)),
        name="ln2",
    )(x_p, x_s, y0, y1, gate, g, b)


def _moe(h, expert, wg, wu, wd, layer):
    n = h.shape[0]
    s = n * TOP_K
    e_flat = expert.reshape(-1)
    onehot = (e_flat[:, None] == jnp.arange(N_EXPERTS, dtype=jnp.int32)[None, :]).astype(jnp.int32)
    csum = jnp.cumsum(onehot, axis=0)
    counts = csum[-1]
    pos_in = jnp.take_along_axis(csum, e_flat[:, None], axis=1)[:, 0] - 1
    padded = (counts + EXPERT_ROWS - 1) // EXPERT_ROWS * EXPERT_ROWS
    pad_end = jnp.cumsum(padded)
    pad_start = pad_end - padded
    dest = pad_start[e_flat] + pos_in
    n_blk = (s + N_EXPERTS * (EXPERT_ROWS - 1) + EXPERT_ROWS - 1) // EXPERT_ROWS
    n_rows = n_blk * EXPERT_ROWS
    tok = jnp.arange(s, dtype=jnp.int32) // TOP_K
    row_tok = (jnp.arange(n_rows, dtype=jnp.int32) % n).at[dest].set(tok, unique_indices=True)
    xs = h[row_tok]
    blk_start = jnp.arange(n_blk, dtype=jnp.int32) * EXPERT_ROWS
    blk_expert = jnp.minimum(jnp.sum((pad_end[None, :] <= blk_start[:, None]).astype(jnp.int32), axis=1),
                             N_EXPERTS - 1)
    ys = _experts(blk_expert, xs, wg, wu, wd, layer)
    dest2 = dest.reshape(n, TOP_K)
    return ys[dest2[:, 0]], ys[dest2[:, 1]]


def kernel(x_prompt, x_sample, cache_k, cache_v, state_pool, page_table, rel_bias, w_router, b_router,
           w_in, pool_w, pool_scale, w_pool_out, w_attn_out, w_o, ln1_g, ln1_b, w_gate, w_up, w_down,
           ln2_g, ln2_b):
    bsz, seq, _ = x_prompt.shape
    db, ts, _ = x_sample.shape
    past_len = page_table.shape[1] * PAGE_SIZE
    nb = seq // MOBA_BLOCK
    nbs = past_len // MOBA_BLOCK
    assert seq % MOBA_BLOCK == 0 and past_len % MOBA_BLOCK == 0 and ts <= 8
    np_rows = bsz * seq
    ns_rows = db * ts

    far_s = _far_bucket(MOBA_BLOCK + 1, past_len + ts)
    c = np.arange(MOBA_BLOCK)
    t_idx = np.repeat(np.arange(ts), N_HEADS)
    h_idx = np.tile(np.arange(N_HEADS), ts)
    d_last = MOBA_BLOCK + t_idx[:, None] - c[None, :]
    b_last_s = rel_bias[_bucket_np(d_last), h_idx[:, None]]
    tn = np.arange(8)
    d_new = t_idx[:, None] - tn[None, :]
    b_own_s = jnp.where((d_new >= 0) & (tn[None, :] < ts),
                        rel_bias[_bucket_np(np.maximum(d_new, 0)), h_idx[:, None]], NEG_INF)
    cfar_s = rel_bias[far_s][h_idx][:, None]
    head_mask = jnp.asarray((np.arange(ATT_WIDTH)[None, :] // HEAD_DIM) == h_idx[:, None])
    cache_kt = jnp.transpose(cache_k, (0, 2, 3, 4, 1))
    cache_vt = jnp.transpose(cache_v, (0, 2, 3, 4, 1))

    wr_pad = jnp.zeros((D_MODEL, ROUTER_PAD), BF16).at[:, :N_EXPERTS].set(w_router.astype(BF16))
    br_pad = jnp.zeros((1, ROUTER_PAD), F32).at[0, :N_EXPERTS].set(b_router)

    xp = x_prompt.reshape(np_rows, D_MODEL)
    xs = x_sample.reshape(ns_rows, D_MODEL)
    pp, ksm, vsm, psm = [], [], [], []
    k_all = v_all = None
    for l in range(DEPTH):
        w_in_bf = w_in[l].astype(BF16)
        pool_w_bf = pool_w[l].astype(BF16)
        scale = pool_scale[l][None, :]
        wpo = w_pool_out[l].astype(BF16)
        wao = w_attn_out[l].astype(BF16)
        wo = w_o[l].astype(BF16)
        g1, b1 = ln1_g[l][None, :], ln1_b[l][None, :]
        g2, b2 = ln2_g[l][None, :], ln2_b[l][None, :]

        p_p, q_p, k_all, v_all, gp_p, ga_p, kb_p, vt_p, km_p = _inproj(
            xp, w_in_bf, tm=PROMPT_ROW_TILE, attn_layouts=True, seq=seq, layer=l,
            kv_all=None if l == 0 else (k_all, v_all))
        prev0 = jnp.zeros((bsz, POOL_HALO, POOL_WIDTH), F32)
        yp_p = _pool(p_p.reshape(bsz, seq, POOL_WIDTH), prev0, pool_w_bf, scale, tq=MOBA_BLOCK, pos0=0)
        o_p = _moba_prompt(q_p.reshape(bsz, seq, ATT_WIDTH),
                           kb_p.reshape(bsz, nb, MOBA_BLOCK, ATT_WIDTH),
                           vt_p.reshape(bsz, nb, ATT_WIDTH, MOBA_BLOCK),
                           km_p.reshape(bsz, nb, ATT_WIDTH), rel_bias)
        x1_p, x1b_p, rt_p = _merge(xp, yp_p.reshape(np_rows, POOL_WIDTH), o_p.reshape(np_rows, ATT_WIDTH),
                            gp_p, ga_p, wpo, wao, wo, g1, b1, wr_pad, br_pad, tm=PROMPT_ROW_TILE)

        p_s, q_s, k_s, v_s, gp_s, ga_s = _inproj(xs, w_in_bf, tm=ns_rows, attn_layouts=False)
        state = state_pool[:, l]
        prev_s = jnp.concatenate([jnp.zeros((db, 1, POOL_WIDTH), F32), state], axis=1)
        p_s3 = p_s.reshape(db, ts, POOL_WIDTH)
        p_s8 = jnp.pad(p_s3, ((0, 0), (0, 8 - ts), (0, 0)))
        yp_s = _pool(p_s8, prev_s, pool_w_bf, scale, tq=8, pos0=past_len)[:, :ts]
        qbd = jnp.where(head_mask[None], jnp.repeat(q_s.reshape(db, ts, ATT_WIDTH), N_HEADS, axis=1), 0.0)
        knew8 = jnp.pad(k_s.reshape(db, ts, ATT_WIDTH), ((0, 0), (0, 8 - ts), (0, 0)))
        vnew8 = jnp.pad(v_s.reshape(db, ts, ATT_WIDTH), ((0, 0), (0, 8 - ts), (0, 0)))
        o_s = _moba_sample(page_table, qbd, knew8, vnew8, cfar_s, b_last_s, b_own_s,
                           cache_kt, cache_vt, l)[:, :ts]
        x1_s, x1b_s, rt_s = _merge(xs, yp_s.reshape(ns_rows, POOL_WIDTH), o_s.reshape(ns_rows, ATT_WIDTH),
                            gp_s, ga_s, wpo, wao, wo, g1, b1, wr_pad, br_pad, tm=ns_rows)

        x1b = jnp.concatenate([x1b_p, x1b_s], axis=0)
        e_p, gt_p = _unpack_route(rt_p)
        e_s, gt_s = _unpack_route(rt_s)
        y0, y1 = _moe(x1b, jnp.concatenate([e_p, e_s], axis=0), w_gate, w_up, w_down, l)
        xp, xs = _ln2(x1_p, x1_s, y0, y1, jnp.concatenate([gt_p, gt_s], axis=0), g2, b2, tm=ROW_TILE)

        pp.append(p_p.reshape(bsz, seq, POOL_WIDTH)[:, seq - POOL_BUF:])
        ksm.append(k_s.reshape(db, ts, N_HEADS, HEAD_DIM))
        vsm.append(v_s.reshape(db, ts, N_HEADS, HEAD_DIM))
        psm.append(jnp.concatenate([state, p_s3], axis=1)[:, -POOL_BUF:])

    def rows_out(a):
        return jnp.transpose(a.reshape(bsz, DEPTH, N_HEADS, HEAD_DIM, seq), (0, 4, 1, 2, 3))

    return (xp.reshape(bsz, seq, D_MODEL), xs.reshape(db, ts, D_MODEL),
            rows_out(k_all), rows_out(v_all), jnp.stack(pp, axis=1),
            jnp.stack(ksm, axis=2), jnp.stack(vsm, axis=2), jnp.stack(psm, axis=1))
```

```python
import functools
import math

import numpy as np
import jax
import jax.numpy as jnp
from jax import lax
from jax.experimental import pallas as pl
from jax.experimental.pallas import tpu as pltpu

F32 = jnp.float32
BF16 = jnp.bfloat16
NEG_INF = float("-inf")

D_MODEL = 1024
DEPTH = 2
N_HEADS = 8
HEAD_DIM = 64
ATT_WIDTH = N_HEADS * HEAD_DIM
MOBA_BLOCK = 256
MOBA_TOPK = 3
N_BUCKETS = 32
REL_MAX_DIST = 128
POOL_WINDOWS = (2, 4, 8, 16)
POOL_WIDTH = 512
POOL_GW = 128
POOL_BUF = 15
POOL_HALO = 16
IN_WIDTH = POOL_WIDTH + 3 * ATT_WIDTH + 2 * D_MODEL
N_EXPERTS = 16
N_GROUPS = 4
EXPERTS_PER_GROUP = 4
TOP_K = 2
D_EXPERT = 512
EXPERT_ROWS = 512
ROW_TILE = 128
PROMPT_ROW_TILE = 512
DEEPNORM_ALPHA = (2 * DEPTH) ** 0.25
LN_EPS = 1e-5
PAGE_SIZE = 128
ROUTER_PAD = 128
LANES = 128

VMEM_LIMIT = 52 * 1024 * 1024


def _cparams(sem):
    return pltpu.CompilerParams(dimension_semantics=sem, vmem_limit_bytes=VMEM_LIMIT)


def _nt_dot(a, b):
    return lax.dot_general(a, b, (((1,), (1,)), ((), ())), preferred_element_type=F32)


def _top_mask(scores, cand, k, n):
    idx = lax.broadcasted_iota(jnp.int32, scores.shape, 0)
    sel = jnp.zeros(scores.shape, jnp.bool_)
    for _ in range(k):
        c = cand & jnp.logical_not(sel)
        cur = jnp.where(c, scores, NEG_INF)
        mx = jnp.max(cur, axis=0, keepdims=True)
        first = jnp.min(jnp.where(c & (cur == mx), idx, n), axis=0, keepdims=True)
        sel = sel | (idx == first)
    return sel


def _inproj_body(x_ref, w_ref, *refs, tm, n_alias):
    p_ref, q_ref, k_ref, v_ref, gp_ref, ga_ref, *attn_refs = refs[n_alias:]
    xb = x_ref[...].astype(BF16)

    def seg(a, b):
        return jnp.dot(xb, w_ref[:, a:b], preferred_element_type=F32)

    c0 = POOL_WIDTH
    p_ref[...] = seg(0, c0)
    q_ref[...] = seg(c0, c0 + ATT_WIDTH) * (HEAD_DIM ** -0.5)
    k = seg(c0 + ATT_WIDTH, c0 + 2 * ATT_WIDTH)
    v = seg(c0 + 2 * ATT_WIDTH, c0 + 3 * ATT_WIDTH)
    g0 = c0 + 3 * ATT_WIDTH
    gp_ref[...] = jax.nn.sigmoid(seg(g0, g0 + D_MODEL))
    ga_ref[...] = jax.nn.sigmoid(seg(g0 + D_MODEL, g0 + 2 * D_MODEL))
    if attn_refs:
        kb_ref, vt_ref, km_ref = attn_refs
        vt = v.T
        if n_alias:
            k_ref[...] = k.T
            v_ref[...] = vt
        else:
            k_ref[0] = k.T
            v_ref[0] = vt
            for other in range(1, DEPTH):
                k_ref[other] = jnp.zeros((ATT_WIDTH, tm), F32)
                v_ref[other] = jnp.zeros((ATT_WIDTH, tm), F32)
        kb_ref[...] = k.astype(BF16)
        for blk in range(tm // MOBA_BLOCK):
            rows = slice(blk * MOBA_BLOCK, (blk + 1) * MOBA_BLOCK)
            vt_ref[blk] = vt[:, rows].astype(BF16)
            km_ref[blk] = jnp.sum(k[rows, :], axis=0, keepdims=True) * (1.0 / MOBA_BLOCK)
    else:
        k_ref[...] = k
        v_ref[...] = v


def _inproj(x, w_bf, *, tm, attn_layouts, seq=None, layer=0, kv_all=None):
    m = x.shape[0]
    nt = m // tm
    row = lambda n: pl.BlockSpec((tm, n), lambda i: (i, 0))
    out_shape = [jax.ShapeDtypeStruct((m, POOL_WIDTH), F32),
                 jax.ShapeDtypeStruct((m, ATT_WIDTH), F32),
                 jax.ShapeDtypeStruct((m, ATT_WIDTH), F32),
                 jax.ShapeDtypeStruct((m, ATT_WIDTH), F32),
                 jax.ShapeDtypeStruct((m, D_MODEL), F32),
                 jax.ShapeDtypeStruct((m, D_MODEL), F32)]
    out_specs = [row(POOL_WIDTH), row(ATT_WIDTH), row(ATT_WIDTH), row(ATT_WIDTH),
                 row(D_MODEL), row(D_MODEL)]
    if attn_layouts:
        assert tm % MOBA_BLOCK == 0 and seq % tm == 0
        tps = seq // tm
        bpt = tm // MOBA_BLOCK
        assert (layer == 0) == (kv_all is None)
        for idx in (2, 3):
            out_shape[idx] = jax.ShapeDtypeStruct((m // seq, DEPTH, ATT_WIDTH, seq), F32)
            if layer == 0:
                out_specs[idx] = pl.BlockSpec((None, DEPTH, ATT_WIDTH, tm), lambda i: (i // tps, 0, 0, i % tps))
            else:
                out_specs[idx] = pl.BlockSpec((None, None, ATT_WIDTH, tm),
                                              lambda i: (i // tps, layer, 0, i % tps))
        out_shape += [jax.ShapeDtypeStruct((m, ATT_WIDTH), BF16),
                      jax.ShapeDtypeStruct((nt * bpt, ATT_WIDTH, MOBA_BLOCK), BF16),
                      jax.ShapeDtypeStruct((nt * bpt, 1, ATT_WIDTH), F32)]
        out_specs += [row(ATT_WIDTH),
                      pl.BlockSpec((bpt, ATT_WIDTH, MOBA_BLOCK), lambda i: (i, 0, 0)),
                      pl.BlockSpec((bpt, 1, ATT_WIDTH), lambda i: (i, 0, 0))]
    aliased = list(kv_all) if kv_all is not None else []
    return pl.pallas_call(
        functools.partial(_inproj_body, tm=tm, n_alias=len(aliased)),
        grid=(nt,),
        in_specs=[row(D_MODEL), pl.BlockSpec((D_MODEL, IN_WIDTH), lambda i: (0, 0))]
                 + [pl.BlockSpec(memory_space=pl.ANY) for _ in aliased],
        out_specs=out_specs,
        out_shape=out_shape,
        input_output_aliases={2 + a: 2 + a for a in range(len(aliased))},
        compiler_params=_cparams(("arbitrary",)),
        name="inproj",
    )(x, w_bf, *aliased)


def _pool_body(prev_ref, halo_ref, p_ref, pw_ref, sc_ref, y_ref, *, tq, pos0):
    i = pl.program_id(1)
    p = p_ref[0]
    halo = jnp.where(i == 0, prev_ref[0], halo_ref[0])
    z = jnp.concatenate([halo, p], axis=0)
    lane = lax.broadcasted_iota(jnp.int32, z.shape, 1)
    x = z
    for s in (8, 4, 2, 1):
        thr = POOL_WIDTH - POOL_GW * {8: 1, 4: 2, 2: 3, 1: 4}[s]
        sh = pltpu.roll(x, s, 0)
        x = x + (jnp.where(lane >= thr, sh, 0.0) if thr > 0 else sh)
    wsum = x[POOL_HALO:, :]
    row = lax.broadcasted_iota(jnp.int32, (tq, POOL_WIDTH), 0)
    lane2 = lax.broadcasted_iota(jnp.int32, (tq, POOL_WIDTH), 1)
    wl = jnp.where(lane2 < POOL_GW, POOL_WINDOWS[0],
                   jnp.where(lane2 < 2 * POOL_GW, POOL_WINDOWS[1],
                             jnp.where(lane2 < 3 * POOL_GW, POOL_WINDOWS[2], POOL_WINDOWS[3])))
    pos = pos0 + i * tq + row
    cnt = jnp.minimum(pos + 1, wl).astype(F32)
    d = (wsum / cnt - p).astype(BF16)
    ys = [jnp.dot(d[:, g * POOL_GW:(g + 1) * POOL_GW], pw_ref[g], preferred_element_type=F32)
          for g in range(len(POOL_WINDOWS))]
    y_ref[0] = jnp.concatenate(ys, axis=1) * sc_ref[...]


def _pool(p, prev16, pool_w_bf, scale, *, tq, pos0):
    b, t, c = p.shape
    nq = t // tq
    hb = tq // POOL_HALO
    return pl.pallas_call(
        functools.partial(_pool_body, tq=tq, pos0=pos0),
        grid=(b, nq),
        in_specs=[pl.BlockSpec((1, POOL_HALO, c), lambda bi, i: (bi, 0, 0)),
                  pl.BlockSpec((1, POOL_HALO, c), lambda bi, i: (bi, jnp.maximum(i * hb - 1, 0), 0)),
                  pl.BlockSpec((1, tq, c), lambda bi, i: (bi, i, 0)),
                  pl.BlockSpec((len(POOL_WINDOWS), POOL_GW, POOL_GW), lambda bi, i: (0, 0, 0)),
                  pl.BlockSpec((1, c), lambda bi, i: (0, 0))],
        out_specs=pl.BlockSpec((1, tq, c), lambda bi, i: (bi, i, 0)),
        out_shape=jax.ShapeDtypeStruct((b, t, c), F32),
        compiler_params=_cparams(("arbitrary", "arbitrary")),
        name="pool",
    )(prev16, p, p, pool_w_bf, scale)


def _bucket_np(dist):
    n = np.maximum(dist, 0)
    max_exact = N_BUCKETS // 2
    nf = np.maximum(n, 1).astype(np.float32)
    large = max_exact + (np.log(nf / np.float32(max_exact)) / np.float32(math.log(REL_MAX_DIST / max_exact))
                         * np.float32(N_BUCKETS - max_exact)).astype(np.int32)
    large = np.minimum(large, N_BUCKETS - 1)
    return np.where(n < max_exact, n, large)


def _far_bucket(min_dist, max_dist):
    b = _bucket_np(np.arange(min_dist, max_dist + 1))
    assert (b == b[0]).all()
    return int(b[0])


def _moba_prompt_body(rb_ref, q_ref, kb_ref, vt_ref, km_ref, io_ref, ip_ref, o_ref,
                      sel_ref, qz_ref, m_ref, l_ref, acc_ref, bown_ref, bprev_ref, *, nb, far):
    i = pl.program_id(1)
    tq = MOBA_BLOCK

    @pl.when((pl.program_id(0) == 0) & (i == 0))
    def _():
        for h in range(N_HEADS):
            bown_ref[h] = jnp.full((tq, tq), NEG_INF, F32)
            bprev_ref[h] = jnp.zeros((tq, tq), F32)
        io = io_ref[...]
        ip = ip_ref[...]

        def fill(bkt, carry):
            mo = io == bkt
            mp = ip == bkt
            for h in range(N_HEADS):
                val = rb_ref[bkt, h] - rb_ref[far, h]
                bown_ref[h] = jnp.where(mo, val, bown_ref[h])
                bprev_ref[h] = jnp.where(mp, val, bprev_ref[h])
            return carry

        lax.fori_loop(0, N_BUCKETS, fill, 0)

    blk = lax.broadcasted_iota(jnp.int32, (nb, tq), 0)
    valid = blk < i
    lane = lax.broadcasted_iota(jnp.int32, (tq, LANES), 1)
    for hp in range(N_HEADS // 2):
        qf = q_ref[0, :, LANES * hp:LANES * (hp + 1)]
        km = km_ref[0, :, LANES * hp:LANES * (hp + 1)].astype(BF16)
        for hh in range(2):
            h = 2 * hp + hh
            qz = jnp.where((lane >= HEAD_DIM * hh) & (lane < HEAD_DIM * (hh + 1)), qf, 0.0).astype(BF16)
            qz_ref[h] = qz
            sel = _top_mask(_nt_dot(km, qz), valid, MOBA_TOPK, nb)
            selb = jnp.where(sel, 0.0, NEG_INF)
            for jj in range(nb):
                sel_ref[h, jj] = selb[jj:jj + 1, :]
            m_ref[h] = jnp.full((1, tq), -1e30, F32)
            l_ref[h] = jnp.zeros((1, tq), F32)
            acc_ref[h] = jnp.zeros((HEAD_DIM, tq), F32)

    def key_blocks(items):
        sts = [[_nt_dot(kb_ref[0, j, :, LANES * (h // 2):LANES * (h // 2 + 1)], qz_ref[h])
                for h in range(N_HEADS)] for j, _, _ in items]
        ps, alphas = [], []
        for h in range(N_HEADS):
            m = m_ref[h]
            m_new = m
            shifted = []
            for idx, (j, bias_ref, masked) in enumerate(items):
                st = sts[idx][h] if bias_ref is None else sts[idx][h] + bias_ref[h]
                cm = jnp.max(st, axis=0, keepdims=True)
                mb = sel_ref[h, j] if masked else None
                m_new = jnp.maximum(m_new, cm + mb if masked else cm)
                shifted.append((st, mb))
            alpha = jnp.exp(m - m_new)
            lsum = alpha * l_ref[h]
            ph = []
            for st, mb in shifted:
                p = jnp.exp(st - m_new) if mb is None else jnp.exp(st + (mb - m_new))
                lsum = lsum + jnp.sum(p, axis=0, keepdims=True)
                ph.append(p.astype(BF16))
            l_ref[h] = lsum
            m_ref[h] = m_new
            ps.append(ph)
            alphas.append(alpha)
        for h in range(N_HEADS):
            acc = alphas[h] * acc_ref[h]
            for idx, (j, _, _) in enumerate(items):
                vt = vt_ref[0, j, HEAD_DIM * h:HEAD_DIM * (h + 1), :]
                acc = acc + jnp.dot(vt, ps[h][idx], preferred_element_type=F32)
            acc_ref[h] = acc

    n_far = jnp.maximum(i - 1, 0)

    def far_body(jj, carry):
        key_blocks([(2 * jj, None, True), (2 * jj + 1, None, True)])
        return carry

    lax.fori_loop(0, n_far // 2, far_body, 0)

    @pl.when(n_far % 2 == 1)
    def _():
        key_blocks([(n_far - 1, None, True)])

    jp = jnp.maximum(i - 1, 0)
    key_blocks([(jp, bprev_ref, True), (i, bown_ref, False)])
    for hp in range(N_HEADS // 2):
        pair = jnp.concatenate([acc_ref[2 * hp] / l_ref[2 * hp], acc_ref[2 * hp + 1] / l_ref[2 * hp + 1]],
                               axis=0)
        o_ref[0, :, LANES * hp:LANES * (hp + 1)] = pair.T


def _moba_prompt(q, kb, vt, km, rel_bias):
    b, t, _ = q.shape
    nb = t // MOBA_BLOCK
    far = _far_bucket(MOBA_BLOCK + 1, t)
    c = np.arange(MOBA_BLOCK)
    d_own = c[None, :] - c[:, None]
    idx_own = jnp.asarray(np.where(d_own >= 0, _bucket_np(d_own), -1).astype(np.int32))
    idx_prev = jnp.asarray(_bucket_np(d_own + MOBA_BLOCK).astype(np.int32))
    once = pl.Buffered(1)
    tile = pl.BlockSpec((MOBA_BLOCK, MOBA_BLOCK), lambda bi, i: (0, 0), pipeline_mode=once)
    return pl.pallas_call(
        functools.partial(_moba_prompt_body, nb=nb, far=far),
        grid=(b, nb),
        in_specs=[pl.BlockSpec(memory_space=pltpu.SMEM),
                  pl.BlockSpec((1, MOBA_BLOCK, ATT_WIDTH), lambda bi, i: (bi, i, 0)),
                  pl.BlockSpec((1, nb, MOBA_BLOCK, ATT_WIDTH), lambda bi, i: (bi, 0, 0, 0), pipeline_mode=once),
                  pl.BlockSpec((1, nb, ATT_WIDTH, MOBA_BLOCK), lambda bi, i: (bi, 0, 0, 0), pipeline_mode=once),
                  pl.BlockSpec((1, nb, ATT_WIDTH), lambda bi, i: (bi, 0, 0)),
                  tile, tile],
        out_specs=pl.BlockSpec((1, MOBA_BLOCK, ATT_WIDTH), lambda bi, i: (bi, i, 0)),
        out_shape=jax.ShapeDtypeStruct((b, t, ATT_WIDTH), F32),
        scratch_shapes=[pltpu.VMEM((N_HEADS, nb, 1, MOBA_BLOCK), F32),
                        pltpu.VMEM((N_HEADS, MOBA_BLOCK, LANES), BF16),
                        pltpu.VMEM((N_HEADS, 1, MOBA_BLOCK), F32),
                        pltpu.VMEM((N_HEADS, 1, MOBA_BLOCK), F32),
                        pltpu.VMEM((N_HEADS, HEAD_DIM, MOBA_BLOCK), F32),
                        pltpu.VMEM((N_HEADS, MOBA_BLOCK, MOBA_BLOCK), F32),
                        pltpu.VMEM((N_HEADS, MOBA_BLOCK, MOBA_BLOCK), F32)],
        compiler_params=_cparams(("arbitrary", "arbitrary")),
        name="moba_prompt",
    )(rel_bias, q, kb, vt, km, idx_own, idx_prev)


SAMPLE_PAGES_PER_STEP = 16
PAGES_PER_BLOCK = MOBA_BLOCK // PAGE_SIZE


def _moba_sample_body(pt_ref, qbd_ref, knew_ref, vnew_ref, cfar_ref, blast_ref, bown_ref, ck_hbm, cv_hbm,
                      o_ref, gs_sc, m_sc, l_sc, o_sc, kbuf, vbuf, sem_k, sem_v, *, nbs, ts, layer):
    npg = SAMPLE_PAGES_PER_STEP
    ppb = PAGES_PER_BLOCK
    b = pl.program_id(0)
    s = pl.program_id(1)
    nsteps = pl.num_programs(1)
    g = b * nsteps + s
    cur = g % 2
    bps = npg // ppb
    nr = ts * N_HEADS

    def page_copies(bb, ss, slot):
        out = []
        for u in range(npg):
            page = pt_ref[bb, ss * npg + u]
            out.append(pltpu.make_async_copy(ck_hbm.at[page, layer], kbuf.at[slot, u], sem_k.at[slot]))
            out.append(pltpu.make_async_copy(cv_hbm.at[page, layer], vbuf.at[slot, u], sem_v.at[slot]))
        return out

    @pl.when(g == 0)
    def _():
        for c in page_copies(b, s, 0):
            c.start()

    @pl.when(g + 1 < pl.num_programs(0) * nsteps)
    def _():
        wrap = s + 1 == nsteps
        for c in page_copies(jnp.where(wrap, b + 1, b), jnp.where(wrap, 0, s + 1), 1 - cur):
            c.start()

    for c in page_copies(b, s, cur):
        c.wait()
    k_refs = [kbuf.at[cur, u] for u in range(npg)]
    v_refs = [vbuf.at[cur, u] for u in range(npg)]
    qb = qbd_ref[0].astype(BF16)
    lane0 = lax.broadcasted_iota(jnp.int32, (ATT_WIDTH, LANES), 1) == 0
    sts = []
    for jj in range(bps):
        kts = [k_refs[ppb * jj + u][...].reshape(ATT_WIDTH, PAGE_SIZE) for u in range(ppb)]
        ksum = functools.reduce(lambda a, b: a + b, [jnp.sum(kt, axis=1, keepdims=True) for kt in kts])
        kmean = ksum * (1.0 / MOBA_BLOCK)
        kmcol = jnp.where(lane0, kmean, 0.0).astype(BF16)
        w = jnp.concatenate([kt.astype(BF16) for kt in kts] + [kmcol], axis=1)
        sts.append(jnp.dot(qb, w, preferred_element_type=F32))
    ps = []
    for jj in range(bps):
        j = s * bps + jj
        gs_sc[j] = sts[jj][:, MOBA_BLOCK:MOBA_BLOCK + 1]
        bias = jnp.where(j == nbs - 1, blast_ref[...], cfar_ref[...])
        st = sts[jj][:, :MOBA_BLOCK] + bias
        m = jnp.max(st, axis=1, keepdims=True)
        p = jnp.exp(st - m)
        m_sc[j] = m
        l_sc[j] = jnp.sum(p, axis=1, keepdims=True)
        ps.append(p.astype(BF16))
    for jj in range(bps):
        o = None
        for u in range(ppb):
            vt = v_refs[ppb * jj + u][...].reshape(ATT_WIDTH, PAGE_SIZE).astype(BF16)
            part = _nt_dot(ps[jj][:, PAGE_SIZE * u:PAGE_SIZE * (u + 1)], vt)
            o = part if o is None else o + part
        o_sc[s * bps + jj] = o

    @pl.when(s == nsteps - 1)
    def _():
        qf = qb.astype(F32)
        gs = gs_sc[...]
        sel = _top_mask(gs, jnp.ones(gs.shape, jnp.bool_), MOBA_TOPK, nbs)
        knew = knew_ref[0].astype(BF16).astype(F32)
        vnew = vnew_ref[0].astype(BF16).astype(F32)
        s_own = [jnp.sum(qf * knew[t:t + 1, :], axis=1, keepdims=True) + bown_ref[:, t:t + 1]
                 for t in range(ts)]
        m_all = m_sc[...]
        mtot = jnp.max(jnp.where(sel, m_all, NEG_INF), axis=0)
        for t in range(ts):
            mtot = jnp.maximum(mtot, s_own[t])
        w = jnp.where(sel, jnp.exp(m_all - mtot[None]), 0.0)
        ltot = jnp.sum(w * l_sc[...], axis=0)
        gs_sc[...] = w

        def merge(j, acc):
            return acc + gs_sc[j] * o_sc[j]

        otot = lax.fori_loop(0, nbs, merge, jnp.zeros((nr, ATT_WIDTH), F32))
        for t in range(ts):
            pt = jnp.exp(s_own[t] - mtot)
            ltot = ltot + pt
            otot = otot + pt.astype(BF16).astype(F32) * vnew[t:t + 1, :]
        out = otot / ltot
        row = lax.broadcasted_iota(jnp.int32, out.shape, 0)
        lane = lax.broadcasted_iota(jnp.int32, out.shape, 1)
        out = jnp.where((lane // HEAD_DIM) == (row % N_HEADS), out, 0.0)
        pieces = [jnp.sum(out[N_HEADS * t:N_HEADS * (t + 1), :], axis=0, keepdims=True) for t in range(ts)]
        pieces.append(jnp.zeros((8 - ts, ATT_WIDTH), F32))
        o_ref[0] = jnp.concatenate(pieces, axis=0)


def _moba_sample(page_table, qbd, knew8, vnew8, cfar_rows, b_last, b_own, cache_kt, cache_vt, layer):
    db, nr, _ = qbd.shape
    ts = nr // N_HEADS
    n_pages = page_table.shape[1]
    nbs = n_pages // PAGES_PER_BLOCK
    npg = SAMPLE_PAGES_PER_STEP
    nsteps = n_pages // npg

    full2 = lambda shp: pl.BlockSpec(shp, lambda b, s, pt: (0, 0))
    per_b = lambda r: pl.BlockSpec((1, r, ATT_WIDTH), lambda b, s, pt: (b, 0, 0))
    page_buf = pltpu.VMEM((2, npg, N_HEADS, HEAD_DIM, PAGE_SIZE), F32)
    grid_spec = pltpu.PrefetchScalarGridSpec(
        num_scalar_prefetch=1,
        grid=(db, nsteps),
        in_specs=[per_b(nr), per_b(8), per_b(8),
                  full2((nr, 1)), full2((nr, MOBA_BLOCK)), full2((nr, 8)),
                  pl.BlockSpec(memory_space=pl.ANY), pl.BlockSpec(memory_space=pl.ANY)],
        out_specs=per_b(8),
        scratch_shapes=[pltpu.VMEM((nbs, nr, 1), F32), pltpu.VMEM((nbs, nr, 1), F32),
                        pltpu.VMEM((nbs, nr, 1), F32), pltpu.VMEM((nbs, nr, ATT_WIDTH), F32),
                        page_buf, page_buf, pltpu.SemaphoreType.DMA((2,)), pltpu.SemaphoreType.DMA((2,))],
    )
    return pl.pallas_call(
        functools.partial(_moba_sample_body, nbs=nbs, ts=ts, layer=layer),
        grid_spec=grid_spec,
        out_shape=jax.ShapeDtypeStruct((db, 8, ATT_WIDTH), F32),
        compiler_params=_cparams(("arbitrary", "arbitrary")),
        name="moba_sample",
    )(page_table, qbd, knew8, vnew8, cfar_rows, b_last, b_own, cache_kt, cache_vt)


def _layer_norm(h, g, b):
    mu = jnp.mean(h, axis=-1, keepdims=True)
    c = h - mu
    var = jnp.mean(c * c, axis=-1, keepdims=True)
    return c * lax.rsqrt(var + LN_EPS) * g + b


def _argmax_first(vals):
    best, idx = vals[0], jnp.zeros(vals[0].shape, jnp.int32)
    for k in range(1, len(vals)):
        upd = vals[k] > best
        idx = jnp.where(upd, k, idx)
        best = jnp.where(upd, vals[k], best)
    return best, idx


def _route_rows(logit_rows):
    mx = functools.reduce(jnp.maximum, logit_rows)
    ex = [jnp.exp(r - mx) for r in logit_rows]
    tot = functools.reduce(lambda a, b: a + b, ex)
    probs = [e / tot for e in ex]
    scores = []
    for g in range(N_GROUPS):
        a, b, c, d = probs[EXPERTS_PER_GROUP * g:EXPERTS_PER_GROUP * (g + 1)]
        s1, t1 = jnp.maximum(a, b), jnp.minimum(a, b)
        s2, t2 = jnp.maximum(c, d), jnp.minimum(c, d)
        scores.append(jnp.maximum(s1, s2) + jnp.maximum(jnp.minimum(s1, s2), jnp.maximum(t1, t2)))
    _, gi = _argmax_first(scores)
    ing = []
    for j in range(EXPERTS_PER_GROUP):
        v = probs[j]
        for g in range(1, N_GROUPS):
            v = jnp.where(gi == g, probs[EXPERTS_PER_GROUP * g + j], v)
        ing.append(v)
    w1, i1 = _argmax_first(ing)
    w2, i2 = _argmax_first([jnp.where(i1 == j, -1.0, ing[j]) for j in range(EXPERTS_PER_GROUP)])
    den = w1 + w2
    e1 = (gi * EXPERTS_PER_GROUP + i1).astype(F32)
    e2 = (gi * EXPERTS_PER_GROUP + i2).astype(F32)
    return e1, e2, w1 / den, w2 / den


def _merge_body(x_ref, yp_ref, o_ref, gp_ref, ga_ref, wpo_ref, wao_ref, wo_ref, g_ref, b_ref,
                wr_ref, br_ref, x1_ref, rt_ref, *, tm):
    a = jnp.dot(yp_ref[...].astype(BF16), wpo_ref[...], preferred_element_type=F32)
    bb = jnp.dot(o_ref[...].astype(BF16), wao_ref[...], preferred_element_type=F32)
    merged = gp_ref[...] * a + ga_ref[...] * bb
    mix = jnp.dot(merged.astype(BF16), wo_ref[...], preferred_element_type=F32)
    x1 = _layer_norm(DEEPNORM_ALPHA * x_ref[...] + mix, g_ref[...], b_ref[...])
    x1_ref[...] = x1
    lg = jnp.dot(x1.astype(BF16), wr_ref[...], preferred_element_type=F32) + br_ref[...]
    lgt = lg.T
    e1, e2, g1, g2 = _route_rows([lgt[e:e + 1, :] for e in range(N_EXPERTS)])
    r = lax.broadcasted_iota(jnp.int32, (8, tm), 0)
    rt_ref[0] = jnp.where(r == 0, e1, jnp.where(r == 1, e2, jnp.where(r == 2, g1, jnp.where(r == 3, g2, 0.0))))


def _merge(x, yp, o, gp, ga, wpo, wao, wo, g, b, wr, br, *, tm):
    m = x.shape[0]
    row = lambda n: pl.BlockSpec((tm, n), lambda i: (i, 0))
    full = lambda r, c: pl.BlockSpec((r, c), lambda i: (0, 0))
    return pl.pallas_call(
        functools.partial(_merge_body, tm=tm),
        grid=(m // tm,),
        in_specs=[row(D_MODEL), row(POOL_WIDTH), row(ATT_WIDTH), row(D_MODEL), row(D_MODEL),
                  full(POOL_WIDTH, D_MODEL), full(ATT_WIDTH, D_MODEL), full(D_MODEL, D_MODEL),
                  full(1, D_MODEL), full(1, D_MODEL), full(D_MODEL, ROUTER_PAD), full(1, ROUTER_PAD)],
        out_specs=[row(D_MODEL), pl.BlockSpec((1, 8, tm), lambda i: (i, 0, 0))],
        out_shape=[jax.ShapeDtypeStruct((m, D_MODEL), F32), jax.ShapeDtypeStruct((m // tm, 8, tm), F32)],
        compiler_params=_cparams(("arbitrary",)),
        name="merge",
    )(x, yp, o, gp, ga, wpo, wao, wo, g, b, wr, br)


def _unpack_route(rt):
    n = rt.shape[0] * rt.shape[2]
    cols = rt[:, :4, :].transpose(0, 2, 1).reshape(n, 4)
    return cols[:, :2].astype(jnp.int32), cols[:, 2:]


def _experts_body(be_ref, xs_ref, wg_ref, wu_ref, wd_ref, ys_ref, wg_sc, wu_sc, wd_sc):
    i = pl.program_id(0)

    @pl.when((i == 0) | (be_ref[i] != be_ref[jnp.maximum(i - 1, 0)]))
    def _():
        wg_sc[...] = wg_ref[0].astype(BF16)
        wu_sc[...] = wu_ref[0].astype(BF16)
        wd_sc[...] = wd_ref[0].astype(BF16)

    xb = xs_ref[...].astype(BF16)
    h1 = jnp.dot(xb, wg_sc[...], preferred_element_type=F32)
    h2 = jnp.dot(xb, wu_sc[...], preferred_element_type=F32)
    act = (h1 * jax.nn.sigmoid(h1) * h2).astype(BF16)
    ys_ref[...] = jnp.dot(act, wd_sc[...], preferred_element_type=F32)


def _experts(blk_expert, xs, wg, wu, wd, layer):
    n_rows = xs.shape[0]
    n_blk = n_rows // EXPERT_ROWS
    grid_spec = pltpu.PrefetchScalarGridSpec(
        num_scalar_prefetch=1,
        grid=(n_blk,),
        in_specs=[pl.BlockSpec((EXPERT_ROWS, D_MODEL), lambda i, be: (i, 0)),
                  pl.BlockSpec((None, 1, D_MODEL, D_EXPERT), lambda i, be: (layer, be[i], 0, 0)),
                  pl.BlockSpec((None, 1, D_MODEL, D_EXPERT), lambda i, be: (layer, be[i], 0, 0)),
                  pl.BlockSpec((None, 1, D_EXPERT, D_MODEL), lambda i, be: (layer, be[i], 0, 0))],
        out_specs=pl.BlockSpec((EXPERT_ROWS, D_MODEL), lambda i, be: (i, 0)),
        scratch_shapes=[pltpu.VMEM((D_MODEL, D_EXPERT), BF16), pltpu.VMEM((D_MODEL, D_EXPERT), BF16),
                        pltpu.VMEM((D_EXPERT, D_MODEL), BF16)],
    )
    return pl.pallas_call(
        _experts_body,
        grid_spec=grid_spec,
        out_shape=jax.ShapeDtypeStruct((n_rows, D_MODEL), F32),
        compiler_params=_cparams(("arbitrary",)),
        name="experts",
    )(blk_expert, xs, wg, wu, wd)


def _ln2_body(xp_ref, xs_ref, y0_ref, y1_ref, gt_ref, g_ref, b_ref, op_ref, os_ref, *, n_ptiles):
    i = pl.program_id(0)
    gt = gt_ref[...]
    f = gt[:, 0:1] * y0_ref[...] + gt[:, 1:2] * y1_ref[...]

    @pl.when(i < n_ptiles)
    def _():
        op_ref[...] = _layer_norm(DEEPNORM_ALPHA * xp_ref[...] + f, g_ref[...], b_ref[...])

    @pl.when(i >= n_ptiles)
    def _():
        os_ref[...] = _layer_norm(DEEPNORM_ALPHA * xs_ref[...] + f, g_ref[...], b_ref[...])


def _ln2(x_p, x_s, y0, y1, gate, g, b, *, tm):
    n_p, n_s = x_p.shape[0], x_s.shape[0]
    n_ptiles, n_stiles = n_p // tm, n_s // tm
    assert n_p % tm == 0 and n_s % tm == 0
    row = lambda n: pl.BlockSpec((tm, n), lambda i: (i, 0))
    p_row = pl.BlockSpec((tm, D_MODEL), lambda i: (jnp.minimum(i, n_ptiles - 1), 0))
    s_row = pl.BlockSpec((tm, D_MODEL), lambda i: (jnp.maximum(i - n_ptiles, 0), 0))
    vec = pl.BlockSpec((1, D_MODEL), lambda i: (0, 0))
    return pl.pallas_call(
        functools.partial(_ln2_body, n_ptiles=n_ptiles),
        grid=(n_ptiles + n_stiles,),
        in_specs=[p_row, s_row, row(D_MODEL), row(D_MODEL), row(TOP_K), vec, vec],
        out_specs=[p_row, s_row],
        out_shape=[jax.ShapeDtypeStruct((n_p, D_MODEL), F32), jax.ShapeDtypeStruct((n_s, D_MODEL), F32)],
        compiler_params=_cparams(("arbitrary",)),
        name="ln2",
    )(x_p, x_s, y0, y1, gate, g, b)


def _moe(h, expert, wg, wu, wd, layer):
    n = h.shape[0]
    s = n * TOP_K
    e_flat = expert.reshape(-1)
    onehot = (e_flat[:, None] == jnp.arange(N_EXPERTS, dtype=jnp.int32)[None, :]).astype(jnp.int32)
    csum = jnp.cumsum(onehot, axis=0)
    counts = csum[-1]
    pos_in = jnp.take_along_axis(csum, e_flat[:, None], axis=1)[:, 0] - 1
    padded = (counts + EXPERT_ROWS - 1) // EXPERT_ROWS * EXPERT_ROWS
    pad_end = jnp.cumsum(padded)
    pad_start = pad_end - padded
    dest = pad_start[e_flat] + pos_in
    n_blk = (s + N_EXPERTS * (EXPERT_ROWS - 1) + EXPERT_ROWS - 1) // EXPERT_ROWS
    n_rows = n_blk * EXPERT_ROWS
    tok = jnp.arange(s, dtype=jnp.int32) // TOP_K
    row_tok = (jnp.arange(n_rows, dtype=jnp.int32) % n).at[dest].set(tok, unique_indices=True)
    xs = h[row_tok]
    blk_start = jnp.arange(n_blk, dtype=jnp.int32) * EXPERT_ROWS
    blk_expert = jnp.minimum(jnp.sum((pad_end[None, :] <= blk_start[:, None]).astype(jnp.int32), axis=1),
                             N_EXPERTS - 1)
    ys = _experts(blk_expert, xs, wg, wu, wd, layer)
    dest2 = dest.reshape(n, TOP_K)
    return ys[dest2[:, 0]], ys[dest2[:, 1]]


def kernel(x_prompt, x_sample, cache_k, cache_v, state_pool, page_table, rel_bias, w_router, b_router,
           w_in, pool_w, pool_scale, w_pool_out, w_attn_out, w_o, ln1_g, ln1_b, w_gate, w_up, w_down,
           ln2_g, ln2_b):
    bsz, seq, _ = x_prompt.shape
    db, ts, _ = x_sample.shape
    past_len = page_table.shape[1] * PAGE_SIZE
    nb = seq // MOBA_BLOCK
    nbs = past_len // MOBA_BLOCK
    assert seq % MOBA_BLOCK == 0 and past_len % MOBA_BLOCK == 0 and ts <= 8
    np_rows = bsz * seq
    ns_rows = db * ts

    far_s = _far_bucket(MOBA_BLOCK + 1, past_len + ts)
    c = np.arange(MOBA_BLOCK)
    t_idx = np.repeat(np.arange(ts), N_HEADS)
    h_idx = np.tile(np.arange(N_HEADS), ts)
    d_last = MOBA_BLOCK + t_idx[:, None] - c[None, :]
    b_last_s = rel_bias[_bucket_np(d_last), h_idx[:, None]]
    tn = np.arange(8)
    d_new = t_idx[:, None] - tn[None, :]
    b_own_s = jnp.where((d_new >= 0) & (tn[None, :] < ts),
                        rel_bias[_bucket_np(np.maximum(d_new, 0)), h_idx[:, None]], NEG_INF)
    cfar_s = rel_bias[far_s][h_idx][:, None]
    head_mask = jnp.asarray((np.arange(ATT_WIDTH)[None, :] // HEAD_DIM) == h_idx[:, None])
    cache_kt = jnp.transpose(cache_k, (0, 2, 3, 4, 1))
    cache_vt = jnp.transpose(cache_v, (0, 2, 3, 4, 1))

    wr_pad = jnp.zeros((D_MODEL, ROUTER_PAD), BF16).at[:, :N_EXPERTS].set(w_router.astype(BF16))
    br_pad = jnp.zeros((1, ROUTER_PAD), F32).at[0, :N_EXPERTS].set(b_router)

    xp = x_prompt.reshape(np_rows, D_MODEL)
    xs = x_sample.reshape(ns_rows, D_MODEL)
    pp, ksm, vsm, psm = [], [], [], []
    k_all = v_all = None
    for l in range(DEPTH):
        w_in_bf = w_in[l].astype(BF16)
        pool_w_bf = pool_w[l].astype(BF16)
        scale = pool_scale[l][None, :]
        wpo = w_pool_out[l].astype(BF16)
        wao = w_attn_out[l].astype(BF16)
        wo = w_o[l].astype(BF16)
        g1, b1 = ln1_g[l][None, :], ln1_b[l][None, :]
        g2, b2 = ln2_g[l][None, :], ln2_b[l][None, :]

        p_p, q_p, k_all, v_all, gp_p, ga_p, kb_p, vt_p, km_p = _inproj(
            xp, w_in_bf, tm=PROMPT_ROW_TILE, attn_layouts=True, seq=seq, layer=l,
            kv_all=None if l == 0 else (k_all, v_all))
        prev0 = jnp.zeros((bsz, POOL_HALO, POOL_WIDTH), F32)
        yp_p = _pool(p_p.reshape(bsz, seq, POOL_WIDTH), prev0, pool_w_bf, scale, tq=MOBA_BLOCK, pos0=0)
        o_p = _moba_prompt(q_p.reshape(bsz, seq, ATT_WIDTH),
                           kb_p.reshape(bsz, nb, MOBA_BLOCK, ATT_WIDTH),
                           vt_p.reshape(bsz, nb, ATT_WIDTH, MOBA_BLOCK),
                           km_p.reshape(bsz, nb, ATT_WIDTH), rel_bias)
        x1_p, rt_p = _merge(xp, yp_p.reshape(np_rows, POOL_WIDTH), o_p.reshape(np_rows, ATT_WIDTH),
                            gp_p, ga_p, wpo, wao, wo, g1, b1, wr_pad, br_pad, tm=PROMPT_ROW_TILE)

        p_s, q_s, k_s, v_s, gp_s, ga_s = _inproj(xs, w_in_bf, tm=ns_rows, attn_layouts=False)
        state = state_pool[:, l]
        prev_s = jnp.concatenate([jnp.zeros((db, 1, POOL_WIDTH), F32), state], axis=1)
        p_s3 = p_s.reshape(db, ts, POOL_WIDTH)
        p_s8 = jnp.pad(p_s3, ((0, 0), (0, 8 - ts), (0, 0)))
        yp_s = _pool(p_s8, prev_s, pool_w_bf, scale, tq=8, pos0=past_len)[:, :ts]
        qbd = jnp.where(head_mask[None], jnp.repeat(q_s.reshape(db, ts, ATT_WIDTH), N_HEADS, axis=1), 0.0)
        knew8 = jnp.pad(k_s.reshape(db, ts, ATT_WIDTH), ((0, 0), (0, 8 - ts), (0, 0)))
        vnew8 = jnp.pad(v_s.reshape(db, ts, ATT_WIDTH), ((0, 0), (0, 8 - ts), (0, 0)))
        o_s = _moba_sample(page_table, qbd, knew8, vnew8, cfar_s, b_last_s, b_own_s,
                           cache_kt, cache_vt, l)[:, :ts]
        x1_s, rt_s = _merge(xs, yp_s.reshape(ns_rows, POOL_WIDTH), o_s.reshape(ns_rows, ATT_WIDTH),
                            gp_s, ga_s, wpo, wao, wo, g1, b1, wr_pad, br_pad, tm=ns_rows)

        x1 = jnp.concatenate([x1_p, x1_s], axis=0)
        e_p, gt_p = _unpack_route(rt_p)
        e_s, gt_s = _unpack_route(rt_s)
        y0, y1 = _moe(x1, jnp.concatenate([e_p, e_s], axis=0), w_gate, w_up, w_down, l)
        xp, xs = _ln2(x1_p, x1_s, y0, y1, jnp.concatenate([gt_p, gt_s], axis=0), g2, b2, tm=ROW_TILE)

        pp.append(p_p.reshape(bsz, seq, POOL_WIDTH)[:, seq - POOL_BUF:])
        ksm.append(k_s.reshape(db, ts, N_HEADS, HEAD_DIM))
        vsm.append(v_s.reshape(db, ts, N_HEADS, HEAD_DIM))
        psm.append(jnp.concatenate([state, p_s3], axis=1)[:, -POOL_BUF:])

    def rows_out(a):
        return jnp.transpose(a.reshape(bsz, DEPTH, N_HEADS, HEAD_DIM, seq), (0, 4, 1, 2, 3))

    return (xp.reshape(bsz, seq, D_MODEL), xs.reshape(db, ts, D_MODEL),
            rows_out(k_all), rows_out(v_all), jnp.stack(pp, axis=1),
            jnp.stack(ksm, axis=2), jnp.stack(vsm, axis=2), jnp.stack(psm, axis=1))
```

```python
import functools
import math

import numpy as np
import jax
import jax.numpy as jnp
from jax import lax
from jax.experimental import pallas as pl
from jax.experimental.pallas import tpu as pltpu

F32 = jnp.float32
BF16 = jnp.bfloat16
NEG_INF = float("-inf")

D_MODEL = 1024
DEPTH = 2
N_HEADS = 8
HEAD_DIM = 64
ATT_WIDTH = N_HEADS * HEAD_DIM
MOBA_BLOCK = 256
MOBA_TOPK = 3
N_BUCKETS = 32
REL_MAX_DIST = 128
POOL_WINDOWS = (2, 4, 8, 16)
POOL_WIDTH = 512
POOL_GW = 128
POOL_BUF = 15
POOL_HALO = 16
IN_WIDTH = POOL_WIDTH + 3 * ATT_WIDTH + 2 * D_MODEL
N_EXPERTS = 16
N_GROUPS = 4
EXPERTS_PER_GROUP = 4
TOP_K = 2
D_EXPERT = 512
EXPERT_ROWS = 512
ROW_TILE = 128
PROMPT_ROW_TILE = 512
DEEPNORM_ALPHA = (2 * DEPTH) ** 0.25
LN_EPS = 1e-5
PAGE_SIZE = 128
ROUTER_PAD = 128
LANES = 128

VMEM_LIMIT = 52 * 1024 * 1024


def _cparams(sem):
    return pltpu.CompilerParams(dimension_semantics=sem, vmem_limit_bytes=VMEM_LIMIT)


def _nt_dot(a, b):
    return lax.dot_general(a, b, (((1,), (1,)), ((), ())), preferred_element_type=F32)


def _top_mask(scores, cand, k, n):
    idx = lax.broadcasted_iota(jnp.int32, scores.shape, 0)
    sel = jnp.zeros(scores.shape, jnp.bool_)
    for _ in range(k):
        c = cand & jnp.logical_not(sel)
        cur = jnp.where(c, scores, NEG_INF)
        mx = jnp.max(cur, axis=0, keepdims=True)
        first = jnp.min(jnp.where(c & (cur == mx), idx, n), axis=0, keepdims=True)
        sel = sel | (idx == first)
    return sel


def _inproj_body(x_ref, w_ref, *refs, tm, n_alias):
    p_ref, q_ref, k_ref, v_ref, gp_ref, ga_ref, *attn_refs = refs[n_alias:]
    xb = x_ref[...].astype(BF16)

    def seg(a, b):
        return jnp.dot(xb, w_ref[:, a:b], preferred_element_type=F32)

    c0 = POOL_WIDTH
    p_ref[...] = seg(0, c0)
    q_ref[...] = seg(c0, c0 + ATT_WIDTH) * (HEAD_DIM ** -0.5)
    k = seg(c0 + ATT_WIDTH, c0 + 2 * ATT_WIDTH)
    v = seg(c0 + 2 * ATT_WIDTH, c0 + 3 * ATT_WIDTH)
    g0 = c0 + 3 * ATT_WIDTH
    gp_ref[...] = jax.nn.sigmoid(seg(g0, g0 + D_MODEL))
    ga_ref[...] = jax.nn.sigmoid(seg(g0 + D_MODEL, g0 + 2 * D_MODEL))
    if attn_refs:
        kb_ref, vt_ref, km_ref = attn_refs
        vt = v.T
        if n_alias:
            k_ref[...] = k.T
            v_ref[...] = vt
        else:
            k_ref[0] = k.T
            v_ref[0] = vt
            for other in range(1, DEPTH):
                k_ref[other] = jnp.zeros((ATT_WIDTH, tm), F32)
                v_ref[other] = jnp.zeros((ATT_WIDTH, tm), F32)
        kb_ref[...] = k.astype(BF16)
        for blk in range(tm // MOBA_BLOCK):
            rows = slice(blk * MOBA_BLOCK, (blk + 1) * MOBA_BLOCK)
            vt_ref[blk] = vt[:, rows].astype(BF16)
            km_ref[blk] = jnp.sum(k[rows, :], axis=0, keepdims=True) * (1.0 / MOBA_BLOCK)
    else:
        k_ref[...] = k
        v_ref[...] = v


def _inproj(x, w_bf, *, tm, attn_layouts, seq=None, layer=0, kv_all=None):
    m = x.shape[0]
    nt = m // tm
    row = lambda n: pl.BlockSpec((tm, n), lambda i: (i, 0))
    out_shape = [jax.ShapeDtypeStruct((m, POOL_WIDTH), F32),
                 jax.ShapeDtypeStruct((m, ATT_WIDTH), F32),
                 jax.ShapeDtypeStruct((m, ATT_WIDTH), F32),
                 jax.ShapeDtypeStruct((m, ATT_WIDTH), F32),
                 jax.ShapeDtypeStruct((m, D_MODEL), F32),
                 jax.ShapeDtypeStruct((m, D_MODEL), F32)]
    out_specs = [row(POOL_WIDTH), row(ATT_WIDTH), row(ATT_WIDTH), row(ATT_WIDTH),
                 row(D_MODEL), row(D_MODEL)]
    if attn_layouts:
        assert tm % MOBA_BLOCK == 0 and seq % tm == 0
        tps = seq // tm
        bpt = tm // MOBA_BLOCK
        assert (layer == 0) == (kv_all is None)
        for idx in (2, 3):
            out_shape[idx] = jax.ShapeDtypeStruct((m // seq, DEPTH, ATT_WIDTH, seq), F32)
            if layer == 0:
                out_specs[idx] = pl.BlockSpec((None, DEPTH, ATT_WIDTH, tm), lambda i: (i // tps, 0, 0, i % tps))
            else:
                out_specs[idx] = pl.BlockSpec((None, None, ATT_WIDTH, tm),
                                              lambda i: (i // tps, layer, 0, i % tps))
        out_shape += [jax.ShapeDtypeStruct((m, ATT_WIDTH), BF16),
                      jax.ShapeDtypeStruct((nt * bpt, ATT_WIDTH, MOBA_BLOCK), BF16),
                      jax.ShapeDtypeStruct((nt * bpt, 1, ATT_WIDTH), F32)]
        out_specs += [row(ATT_WIDTH),
                      pl.BlockSpec((bpt, ATT_WIDTH, MOBA_BLOCK), lambda i: (i, 0, 0)),
                      pl.BlockSpec((bpt, 1, ATT_WIDTH), lambda i: (i, 0, 0))]
    aliased = list(kv_all) if kv_all is not None else []
    return pl.pallas_call(
        functools.partial(_inproj_body, tm=tm, n_alias=len(aliased)),
        grid=(nt,),
        in_specs=[row(D_MODEL), pl.BlockSpec((D_MODEL, IN_WIDTH), lambda i: (0, 0))]
                 + [pl.BlockSpec(memory_space=pl.ANY) for _ in aliased],
        out_specs=out_specs,
        out_shape=out_shape,
        input_output_aliases={2 + a: 2 + a for a in range(len(aliased))},
        compiler_params=_cparams(("arbitrary",)),
        name="inproj",
    )(x, w_bf, *aliased)


def _pool_body(prev_ref, halo_ref, p_ref, pw_ref, sc_ref, y_ref, *, tq, pos0):
    i = pl.program_id(1)
    p = p_ref[0]
    halo = jnp.where(i == 0, prev_ref[0], halo_ref[0])
    z = jnp.concatenate([halo, p], axis=0)
    lane = lax.broadcasted_iota(jnp.int32, z.shape, 1)
    x = z
    for s in (8, 4, 2, 1):
        thr = POOL_WIDTH - POOL_GW * {8: 1, 4: 2, 2: 3, 1: 4}[s]
        sh = pltpu.roll(x, s, 0)
        x = x + (jnp.where(lane >= thr, sh, 0.0) if thr > 0 else sh)
    wsum = x[POOL_HALO:, :]
    row = lax.broadcasted_iota(jnp.int32, (tq, POOL_WIDTH), 0)
    lane2 = lax.broadcasted_iota(jnp.int32, (tq, POOL_WIDTH), 1)
    wl = jnp.where(lane2 < POOL_GW, POOL_WINDOWS[0],
                   jnp.where(lane2 < 2 * POOL_GW, POOL_WINDOWS[1],
                             jnp.where(lane2 < 3 * POOL_GW, POOL_WINDOWS[2], POOL_WINDOWS[3])))
    pos = pos0 + i * tq + row
    cnt = jnp.minimum(pos + 1, wl).astype(F32)
    d = (wsum / cnt - p).astype(BF16)
    ys = [jnp.dot(d[:, g * POOL_GW:(g + 1) * POOL_GW], pw_ref[g], preferred_element_type=F32)
          for g in range(len(POOL_WINDOWS))]
    y_ref[0] = jnp.concatenate(ys, axis=1) * sc_ref[...]


def _pool(p, prev16, pool_w_bf, scale, *, tq, pos0):
    b, t, c = p.shape
    nq = t // tq
    hb = tq // POOL_HALO
    return pl.pallas_call(
        functools.partial(_pool_body, tq=tq, pos0=pos0),
        grid=(b, nq),
        in_specs=[pl.BlockSpec((1, POOL_HALO, c), lambda bi, i: (bi, 0, 0)),
                  pl.BlockSpec((1, POOL_HALO, c), lambda bi, i: (bi, jnp.maximum(i * hb - 1, 0), 0)),
                  pl.BlockSpec((1, tq, c), lambda bi, i: (bi, i, 0)),
                  pl.BlockSpec((len(POOL_WINDOWS), POOL_GW, POOL_GW), lambda bi, i: (0, 0, 0)),
                  pl.BlockSpec((1, c), lambda bi, i: (0, 0))],
        out_specs=pl.BlockSpec((1, tq, c), lambda bi, i: (bi, i, 0)),
        out_shape=jax.ShapeDtypeStruct((b, t, c), F32),
        compiler_params=_cparams(("arbitrary", "arbitrary")),
        name="pool",
    )(prev16, p, p, pool_w_bf, scale)


def _bucket_np(dist):
    n = np.maximum(dist, 0)
    max_exact = N_BUCKETS // 2
    nf = np.maximum(n, 1).astype(np.float32)
    large = max_exact + (np.log(nf / np.float32(max_exact)) / np.float32(math.log(REL_MAX_DIST / max_exact))
                         * np.float32(N_BUCKETS - max_exact)).astype(np.int32)
    large = np.minimum(large, N_BUCKETS - 1)
    return np.where(n < max_exact, n, large)


def _far_bucket(min_dist, max_dist):
    b = _bucket_np(np.arange(min_dist, max_dist + 1))
    assert (b == b[0]).all()
    return int(b[0])


def _moba_prompt_body(rb_ref, q_ref, kb_ref, vt_ref, km_ref, io_ref, ip_ref, o_ref,
                      sel_ref, qz_ref, m_ref, l_ref, acc_ref, bown_ref, bprev_ref, *, nb, far):
    i = pl.program_id(1)
    tq = MOBA_BLOCK

    @pl.when((pl.program_id(0) == 0) & (i == 0))
    def _():
        for h in range(N_HEADS):
            bown_ref[h] = jnp.full((tq, tq), NEG_INF, F32)
            bprev_ref[h] = jnp.zeros((tq, tq), F32)
        io = io_ref[...]
        ip = ip_ref[...]

        def fill(bkt, carry):
            mo = io == bkt
            mp = ip == bkt
            for h in range(N_HEADS):
                val = rb_ref[bkt, h] - rb_ref[far, h]
                bown_ref[h] = jnp.where(mo, val, bown_ref[h])
                bprev_ref[h] = jnp.where(mp, val, bprev_ref[h])
            return carry

        lax.fori_loop(0, N_BUCKETS, fill, 0)

    blk = lax.broadcasted_iota(jnp.int32, (nb, tq), 0)
    valid = blk < i
    lane = lax.broadcasted_iota(jnp.int32, (tq, LANES), 1)
    for hp in range(N_HEADS // 2):
        qf = q_ref[0, :, LANES * hp:LANES * (hp + 1)]
        km = km_ref[0, :, LANES * hp:LANES * (hp + 1)].astype(BF16)
        for hh in range(2):
            h = 2 * hp + hh
            qz = jnp.where((lane >= HEAD_DIM * hh) & (lane < HEAD_DIM * (hh + 1)), qf, 0.0).astype(BF16)
            qz_ref[h] = qz
            sel = _top_mask(_nt_dot(km, qz), valid, MOBA_TOPK, nb)
            selb = jnp.where(sel, 0.0, NEG_INF)
            for jj in range(nb):
                sel_ref[h, jj] = selb[jj:jj + 1, :]
            m_ref[h] = jnp.full((1, tq), -1e30, F32)
            l_ref[h] = jnp.zeros((1, tq), F32)
            acc_ref[h] = jnp.zeros((HEAD_DIM, tq), F32)

    def key_blocks(items):
        sts = [[_nt_dot(kb_ref[0, j, :, LANES * (h // 2):LANES * (h // 2 + 1)], qz_ref[h])
                for h in range(N_HEADS)] for j, _, _ in items]
        ps, alphas = [], []
        for h in range(N_HEADS):
            m = m_ref[h]
            m_new = m
            shifted = []
            for idx, (j, bias_ref, masked) in enumerate(items):
                st = sts[idx][h] if bias_ref is None else sts[idx][h] + bias_ref[h]
                cm = jnp.max(st, axis=0, keepdims=True)
                mb = sel_ref[h, j] if masked else None
                m_new = jnp.maximum(m_new, cm + mb if masked else cm)
                shifted.append((st, mb))
            alpha = jnp.exp(m - m_new)
            lsum = alpha * l_ref[h]
            ph = []
            for st, mb in shifted:
                p = jnp.exp(st - m_new) if mb is None else jnp.exp(st + (mb - m_new))
                lsum = lsum + jnp.sum(p, axis=0, keepdims=True)
                ph.append(p.astype(BF16))
            l_ref[h] = lsum
            m_ref[h] = m_new
            ps.append(ph)
            alphas.append(alpha)
        for h in range(N_HEADS):
            acc = alphas[h] * acc_ref[h]
            for idx, (j, _, _) in enumerate(items):
                vt = vt_ref[0, j, HEAD_DIM * h:HEAD_DIM * (h + 1), :]
                acc = acc + jnp.dot(vt, ps[h][idx], preferred_element_type=F32)
            acc_ref[h] = acc

    n_far = jnp.maximum(i - 1, 0)

    def far_body(jj, carry):
        key_blocks([(2 * jj, None, True), (2 * jj + 1, None, True)])
        return carry

    lax.fori_loop(0, n_far // 2, far_body, 0)

    @pl.when(n_far % 2 == 1)
    def _():
        key_blocks([(n_far - 1, None, True)])

    jp = jnp.maximum(i - 1, 0)
    key_blocks([(jp, bprev_ref, True), (i, bown_ref, False)])
    for hp in range(N_HEADS // 2):
        pair = jnp.concatenate([acc_ref[2 * hp] / l_ref[2 * hp], acc_ref[2 * hp + 1] / l_ref[2 * hp + 1]],
                               axis=0)
        o_ref[0, :, LANES * hp:LANES * (hp + 1)] = pair.T


def _moba_prompt(q, kb, vt, km, rel_bias):
    b, t, _ = q.shape
    nb = t // MOBA_BLOCK
    far = _far_bucket(MOBA_BLOCK + 1, t)
    c = np.arange(MOBA_BLOCK)
    d_own = c[None, :] - c[:, None]
    idx_own = jnp.asarray(np.where(d_own >= 0, _bucket_np(d_own), -1).astype(np.int32))
    idx_prev = jnp.asarray(_bucket_np(d_own + MOBA_BLOCK).astype(np.int32))
    once = pl.Buffered(1)
    tile = pl.BlockSpec((MOBA_BLOCK, MOBA_BLOCK), lambda bi, i: (0, 0), pipeline_mode=once)
    return pl.pallas_call(
        functools.partial(_moba_prompt_body, nb=nb, far=far),
        grid=(b, nb),
        in_specs=[pl.BlockSpec(memory_space=pltpu.SMEM),
                  pl.BlockSpec((1, MOBA_BLOCK, ATT_WIDTH), lambda bi, i: (bi, i, 0)),
                  pl.BlockSpec((1, nb, MOBA_BLOCK, ATT_WIDTH), lambda bi, i: (bi, 0, 0, 0), pipeline_mode=once),
                  pl.BlockSpec((1, nb, ATT_WIDTH, MOBA_BLOCK), lambda bi, i: (bi, 0, 0, 0), pipeline_mode=once),
                  pl.BlockSpec((1, nb, ATT_WIDTH), lambda bi, i: (bi, 0, 0)),
                  tile, tile],
        out_specs=pl.BlockSpec((1, MOBA_BLOCK, ATT_WIDTH), lambda bi, i: (bi, i, 0)),
        out_shape=jax.ShapeDtypeStruct((b, t, ATT_WIDTH), F32),
        scratch_shapes=[pltpu.VMEM((N_HEADS, nb, 1, MOBA_BLOCK), F32),
                        pltpu.VMEM((N_HEADS, MOBA_BLOCK, LANES), BF16),
                        pltpu.VMEM((N_HEADS, 1, MOBA_BLOCK), F32),
                        pltpu.VMEM((N_HEADS, 1, MOBA_BLOCK), F32),
                        pltpu.VMEM((N_HEADS, HEAD_DIM, MOBA_BLOCK), F32),
                        pltpu.VMEM((N_HEADS, MOBA_BLOCK, MOBA_BLOCK), F32),
                        pltpu.VMEM((N_HEADS, MOBA_BLOCK, MOBA_BLOCK), F32)],
        compiler_params=_cparams(("arbitrary", "arbitrary")),
        name="moba_prompt",
    )(rel_bias, q, kb, vt, km, idx_own, idx_prev)


SAMPLE_PAGES_PER_STEP = 16
PAGES_PER_BLOCK = MOBA_BLOCK // PAGE_SIZE


def _moba_sample_body(pt_ref, qbd_ref, knew_ref, vnew_ref, cfar_ref, blast_ref, bown_ref, ck_hbm, cv_hbm,
                      o_ref, gs_sc, m_sc, l_sc, o_sc, kbuf, vbuf, sem_k, sem_v, *, nbs, ts, layer):
    npg = SAMPLE_PAGES_PER_STEP
    ppb = PAGES_PER_BLOCK
    b = pl.program_id(0)
    s = pl.program_id(1)
    nsteps = pl.num_programs(1)
    g = b * nsteps + s
    cur = g % 2
    bps = npg // ppb
    nr = ts * N_HEADS

    def page_copies(bb, ss, slot):
        out = []
        for u in range(npg):
            page = pt_ref[bb, ss * npg + u]
            out.append(pltpu.make_async_copy(ck_hbm.at[page, layer], kbuf.at[slot, u], sem_k.at[slot]))
            out.append(pltpu.make_async_copy(cv_hbm.at[page, layer], vbuf.at[slot, u], sem_v.at[slot]))
        return out

    @pl.when(g == 0)
    def _():
        for n_c, c in enumerate(page_copies(b, s, 0)):
            c.start(priority=n_c % 2)

    @pl.when(g + 1 < pl.num_programs(0) * nsteps)
    def _():
        wrap = s + 1 == nsteps
        for n_c, c in enumerate(page_copies(jnp.where(wrap, b + 1, b), jnp.where(wrap, 0, s + 1), 1 - cur)):
            c.start(priority=n_c % 2)

    for c in page_copies(b, s, cur):
        c.wait()
    k_refs = [kbuf.at[cur, u] for u in range(npg)]
    v_refs = [vbuf.at[cur, u] for u in range(npg)]
    qb = qbd_ref[0].astype(BF16)
    lane0 = lax.broadcasted_iota(jnp.int32, (ATT_WIDTH, LANES), 1) == 0
    sts = []
    for jj in range(bps):
        kts = [k_refs[ppb * jj + u][...].reshape(ATT_WIDTH, PAGE_SIZE) for u in range(ppb)]
        ksum = functools.reduce(lambda a, b: a + b, [jnp.sum(kt, axis=1, keepdims=True) for kt in kts])
        kmean = ksum * (1.0 / MOBA_BLOCK)
        kmcol = jnp.where(lane0, kmean, 0.0).astype(BF16)
        w = jnp.concatenate([kt.astype(BF16) for kt in kts] + [kmcol], axis=1)
        sts.append(jnp.dot(qb, w, preferred_element_type=F32))
    ps = []
    for jj in range(bps):
        j = s * bps + jj
        gs_sc[j] = sts[jj][:, MOBA_BLOCK:MOBA_BLOCK + 1]
        bias = jnp.where(j == nbs - 1, blast_ref[...], cfar_ref[...])
        st = sts[jj][:, :MOBA_BLOCK] + bias
        m = jnp.max(st, axis=1, keepdims=True)
        p = jnp.exp(st - m)
        m_sc[j] = m
        l_sc[j] = jnp.sum(p, axis=1, keepdims=True)
        ps.append(p.astype(BF16))
    for jj in range(bps):
        o = None
        for u in range(ppb):
            vt = v_refs[ppb * jj + u][...].reshape(ATT_WIDTH, PAGE_SIZE).astype(BF16)
            part = _nt_dot(ps[jj][:, PAGE_SIZE * u:PAGE_SIZE * (u + 1)], vt)
            o = part if o is None else o + part
        o_sc[s * bps + jj] = o

    @pl.when(s == nsteps - 1)
    def _():
        qf = qb.astype(F32)
        gs = gs_sc[...]
        sel = _top_mask(gs, jnp.ones(gs.shape, jnp.bool_), MOBA_TOPK, nbs)
        knew = knew_ref[0].astype(BF16).astype(F32)
        vnew = vnew_ref[0].astype(BF16).astype(F32)
        s_own = [jnp.sum(qf * knew[t:t + 1, :], axis=1, keepdims=True) + bown_ref[:, t:t + 1]
                 for t in range(ts)]
        m_all = m_sc[...]
        mtot = jnp.max(jnp.where(sel, m_all, NEG_INF), axis=0)
        for t in range(ts):
            mtot = jnp.maximum(mtot, s_own[t])
        w = jnp.where(sel, jnp.exp(m_all - mtot[None]), 0.0)
        ltot = jnp.sum(w * l_sc[...], axis=0)
        gs_sc[...] = w

        def merge(j, acc):
            return acc + gs_sc[j] * o_sc[j]

        otot = lax.fori_loop(0, nbs, merge, jnp.zeros((nr, ATT_WIDTH), F32))
        for t in range(ts):
            pt = jnp.exp(s_own[t] - mtot)
            ltot = ltot + pt
            otot = otot + pt.astype(BF16).astype(F32) * vnew[t:t + 1, :]
        out = otot / ltot
        row = lax.broadcasted_iota(jnp.int32, out.shape, 0)
        lane = lax.broadcasted_iota(jnp.int32, out.shape, 1)
        out = jnp.where((lane // HEAD_DIM) == (row % N_HEADS), out, 0.0)
        pieces = [jnp.sum(out[N_HEADS * t:N_HEADS * (t + 1), :], axis=0, keepdims=True) for t in range(ts)]
        pieces.append(jnp.zeros((8 - ts, ATT_WIDTH), F32))
        o_ref[0] = jnp.concatenate(pieces, axis=0)


def _moba_sample(page_table, qbd, knew8, vnew8, cfar_rows, b_last, b_own, cache_kt, cache_vt, layer):
    db, nr, _ = qbd.shape
    ts = nr // N_HEADS
    n_pages = page_table.shape[1]
    nbs = n_pages // PAGES_PER_BLOCK
    npg = SAMPLE_PAGES_PER_STEP
    nsteps = n_pages // npg

    full2 = lambda shp: pl.BlockSpec(shp, lambda b, s, pt: (0, 0))
    per_b = lambda r: pl.BlockSpec((1, r, ATT_WIDTH), lambda b, s, pt: (b, 0, 0))
    page_buf = pltpu.VMEM((2, npg, N_HEADS, HEAD_DIM, PAGE_SIZE), F32)
    grid_spec = pltpu.PrefetchScalarGridSpec(
        num_scalar_prefetch=1,
        grid=(db, nsteps),
        in_specs=[per_b(nr), per_b(8), per_b(8),
                  full2((nr, 1)), full2((nr, MOBA_BLOCK)), full2((nr, 8)),
                  pl.BlockSpec(memory_space=pl.ANY), pl.BlockSpec(memory_space=pl.ANY)],
        out_specs=per_b(8),
        scratch_shapes=[pltpu.VMEM((nbs, nr, 1), F32), pltpu.VMEM((nbs, nr, 1), F32),
                        pltpu.VMEM((nbs, nr, 1), F32), pltpu.VMEM((nbs, nr, ATT_WIDTH), F32),
                        page_buf, page_buf, pltpu.SemaphoreType.DMA((2,)), pltpu.SemaphoreType.DMA((2,))],
    )
    return pl.pallas_call(
        functools.partial(_moba_sample_body, nbs=nbs, ts=ts, layer=layer),
        grid_spec=grid_spec,
        out_shape=jax.ShapeDtypeStruct((db, 8, ATT_WIDTH), F32),
        compiler_params=_cparams(("arbitrary", "arbitrary")),
        name="moba_sample",
    )(page_table, qbd, knew8, vnew8, cfar_rows, b_last, b_own, cache_kt, cache_vt)


def _layer_norm(h, g, b):
    mu = jnp.mean(h, axis=-1, keepdims=True)
    c = h - mu
    var = jnp.mean(c * c, axis=-1, keepdims=True)
    return c * lax.rsqrt(var + LN_EPS) * g + b


def _argmax_first(vals):
    best, idx = vals[0], jnp.zeros(vals[0].shape, jnp.int32)
    for k in range(1, len(vals)):
        upd = vals[k] > best
        idx = jnp.where(upd, k, idx)
        best = jnp.where(upd, vals[k], best)
    return best, idx


def _route_rows(logit_rows):
    mx = functools.reduce(jnp.maximum, logit_rows)
    ex = [jnp.exp(r - mx) for r in logit_rows]
    tot = functools.reduce(lambda a, b: a + b, ex)
    probs = [e / tot for e in ex]
    scores = []
    for g in range(N_GROUPS):
        a, b, c, d = probs[EXPERTS_PER_GROUP * g:EXPERTS_PER_GROUP * (g + 1)]
        s1, t1 = jnp.maximum(a, b), jnp.minimum(a, b)
        s2, t2 = jnp.maximum(c, d), jnp.minimum(c, d)
        scores.append(jnp.maximum(s1, s2) + jnp.maximum(jnp.minimum(s1, s2), jnp.maximum(t1, t2)))
    _, gi = _argmax_first(scores)
    ing = []
    for j in range(EXPERTS_PER_GROUP):
        v = probs[j]
        for g in range(1, N_GROUPS):
            v = jnp.where(gi == g, probs[EXPERTS_PER_GROUP * g + j], v)
        ing.append(v)
    w1, i1 = _argmax_first(ing)
    w2, i2 = _argmax_first([jnp.where(i1 == j, -1.0, ing[j]) for j in range(EXPERTS_PER_GROUP)])
    den = w1 + w2
    e1 = (gi * EXPERTS_PER_GROUP + i1).astype(F32)
    e2 = (gi * EXPERTS_PER_GROUP + i2).astype(F32)
    return e1, e2, w1 / den, w2 / den


def _merge_body(x_ref, yp_ref, o_ref, gp_ref, ga_ref, wpo_ref, wao_ref, wo_ref, g_ref, b_ref,
                wr_ref, br_ref, x1_ref, rt_ref, *, tm):
    a = jnp.dot(yp_ref[...].astype(BF16), wpo_ref[...], preferred_element_type=F32)
    bb = jnp.dot(o_ref[...].astype(BF16), wao_ref[...], preferred_element_type=F32)
    merged = gp_ref[...] * a + ga_ref[...] * bb
    mix = jnp.dot(merged.astype(BF16), wo_ref[...], preferred_element_type=F32)
    x1 = _layer_norm(DEEPNORM_ALPHA * x_ref[...] + mix, g_ref[...], b_ref[...])
    x1_ref[...] = x1
    lg = jnp.dot(x1.astype(BF16), wr_ref[...], preferred_element_type=F32) + br_ref[...]
    lgt = lg.T
    e1, e2, g1, g2 = _route_rows([lgt[e:e + 1, :] for e in range(N_EXPERTS)])
    r = lax.broadcasted_iota(jnp.int32, (8, tm), 0)
    rt_ref[0] = jnp.where(r == 0, e1, jnp.where(r == 1, e2, jnp.where(r == 2, g1, jnp.where(r == 3, g2, 0.0))))


def _merge(x, yp, o, gp, ga, wpo, wao, wo, g, b, wr, br, *, tm):
    m = x.shape[0]
    row = lambda n: pl.BlockSpec((tm, n), lambda i: (i, 0))
    full = lambda r, c: pl.BlockSpec((r, c), lambda i: (0, 0))
    return pl.pallas_call(
        functools.partial(_merge_body, tm=tm),
        grid=(m // tm,),
        in_specs=[row(D_MODEL), row(POOL_WIDTH), row(ATT_WIDTH), row(D_MODEL), row(D_MODEL),
                  full(POOL_WIDTH, D_MODEL), full(ATT_WIDTH, D_MODEL), full(D_MODEL, D_MODEL),
                  full(1, D_MODEL), full(1, D_MODEL), full(D_MODEL, ROUTER_PAD), full(1, ROUTER_PAD)],
        out_specs=[row(D_MODEL), pl.BlockSpec((1, 8, tm), lambda i: (i, 0, 0))],
        out_shape=[jax.ShapeDtypeStruct((m, D_MODEL), F32), jax.ShapeDtypeStruct((m // tm, 8, tm), F32)],
        compiler_params=_cparams(("arbitrary",)),
        name="merge",
    )(x, yp, o, gp, ga, wpo, wao, wo, g, b, wr, br)


def _unpack_route(rt):
    n = rt.shape[0] * rt.shape[2]
    cols = rt[:, :4, :].transpose(0, 2, 1).reshape(n, 4)
    return cols[:, :2].astype(jnp.int32), cols[:, 2:]


def _experts_body(be_ref, xs_ref, wg_ref, wu_ref, wd_ref, ys_ref, wg_sc, wu_sc, wd_sc):
    i = pl.program_id(0)

    @pl.when((i == 0) | (be_ref[i] != be_ref[jnp.maximum(i - 1, 0)]))
    def _():
        wg_sc[...] = wg_ref[0].astype(BF16)
        wu_sc[...] = wu_ref[0].astype(BF16)
        wd_sc[...] = wd_ref[0].astype(BF16)

    xb = xs_ref[...].astype(BF16)
    h1 = jnp.dot(xb, wg_sc[...], preferred_element_type=F32)
    h2 = jnp.dot(xb, wu_sc[...], preferred_element_type=F32)
    act = (h1 * jax.nn.sigmoid(h1) * h2).astype(BF16)
    ys_ref[...] = jnp.dot(act, wd_sc[...], preferred_element_type=F32)


def _experts(blk_expert, xs, wg, wu, wd, layer):
    n_rows = xs.shape[0]
    n_blk = n_rows // EXPERT_ROWS
    grid_spec = pltpu.PrefetchScalarGridSpec(
        num_scalar_prefetch=1,
        grid=(n_blk,),
        in_specs=[pl.BlockSpec((EXPERT_ROWS, D_MODEL), lambda i, be: (i, 0)),
                  pl.BlockSpec((None, 1, D_MODEL, D_EXPERT), lambda i, be: (layer, be[i], 0, 0)),
                  pl.BlockSpec((None, 1, D_MODEL, D_EXPERT), lambda i, be: (layer, be[i], 0, 0)),
                  pl.BlockSpec((None, 1, D_EXPERT, D_MODEL), lambda i, be: (layer, be[i], 0, 0))],
        out_specs=pl.BlockSpec((EXPERT_ROWS, D_MODEL), lambda i, be: (i, 0)),
        scratch_shapes=[pltpu.VMEM((D_MODEL, D_EXPERT), BF16), pltpu.VMEM((D_MODEL, D_EXPERT), BF16),
                        pltpu.VMEM((D_EXPERT, D_MODEL), BF16)],
    )
    return pl.pallas_call(
        _experts_body,
        grid_spec=grid_spec,
        out_shape=jax.ShapeDtypeStruct((n_rows, D_MODEL), F32),
        compiler_params=_cparams(("arbitrary",)),
        name="experts",
    )(blk_expert, xs, wg, wu, wd)


def _ln2_body(xp_ref, xs_ref, y0_ref, y1_ref, gt_ref, g_ref, b_ref, op_ref, os_ref, *, n_ptiles):
    i = pl.program_id(0)
    gt = gt_ref[...]
    f = gt[:, 0:1] * y0_ref[...] + gt[:, 1:2] * y1_ref[...]

    @pl.when(i < n_ptiles)
    def _():
        op_ref[...] = _layer_norm(DEEPNORM_ALPHA * xp_ref[...] + f, g_ref[...], b_ref[...])

    @pl.when(i >= n_ptiles)
    def _():
        os_ref[...] = _layer_norm(DEEPNORM_ALPHA * xs_ref[...] + f, g_ref[...], b_ref[...])


def _ln2(x_p, x_s, y0, y1, gate, g, b, *, tm):
    n_p, n_s = x_p.shape[0], x_s.shape[0]
    n_ptiles, n_stiles = n_p // tm, n_s // tm
    assert n_p % tm == 0 and n_s % tm == 0
    row = lambda n: pl.BlockSpec((tm, n), lambda i: (i, 0))
    p_row = pl.BlockSpec((tm, D_MODEL), lambda i: (jnp.minimum(i, n_ptiles - 1), 0))
    s_row = pl.BlockSpec((tm, D_MODEL), lambda i: (jnp.maximum(i - n_ptiles, 0), 0))
    vec = pl.BlockSpec((1, D_MODEL), lambda i: (0, 0))
    return pl.pallas_call(
        functools.partial(_ln2_body, n_ptiles=n_ptiles),
        grid=(n_ptiles + n_stiles,),
        in_specs=[p_row, s_row, row(D_MODEL), row(D_MODEL), row(TOP_K), vec, vec],
        out_specs=[p_row, s_row],
        out_shape=[jax.ShapeDtypeStruct((n_p, D_MODEL), F32), jax.ShapeDtypeStruct((n_s, D_MODEL), F32)],
        compiler_params=_cparams(("arbitrary",)),
        name="ln2",
    )(x_p, x_s, y0, y1, gate, g, b)


def _moe(h, expert, wg, wu, wd, layer):
    n = h.shape[0]
    s = n * TOP_K
    e_flat = expert.reshape(-1)
    onehot = (e_flat[:, None] == jnp.arange(N_EXPERTS, dtype=jnp.int32)[None, :]).astype(jnp.int32)
    csum = jnp.cumsum(onehot, axis=0)
    counts = csum[-1]
    pos_in = jnp.take_along_axis(csum, e_flat[:, None], axis=1)[:, 0] - 1
    padded = (counts + EXPERT_ROWS - 1) // EXPERT_ROWS * EXPERT_ROWS
    pad_end = jnp.cumsum(padded)
    pad_start = pad_end - padded
    dest = pad_start[e_flat] + pos_in
    n_blk = (s + N_EXPERTS * (EXPERT_ROWS - 1) + EXPERT_ROWS - 1) // EXPERT_ROWS
    n_rows = n_blk * EXPERT_ROWS
    tok = jnp.arange(s, dtype=jnp.int32) // TOP_K
    row_tok = (jnp.arange(n_rows, dtype=jnp.int32) % n).at[dest].set(tok, unique_indices=True)
    xs = h[row_tok]
    blk_start = jnp.arange(n_blk, dtype=jnp.int32) * EXPERT_ROWS
    blk_expert = jnp.minimum(jnp.sum((pad_end[None, :] <= blk_start[:, None]).astype(jnp.int32), axis=1),
                             N_EXPERTS - 1)
    ys = _experts(blk_expert, xs, wg, wu, wd, layer)
    dest2 = dest.reshape(n, TOP_K)
    return ys[dest2[:, 0]], ys[dest2[:, 1]]


def kernel(x_prompt, x_sample, cache_k, cache_v, state_pool, page_table, rel_bias, w_router, b_router,
           w_in, pool_w, pool_scale, w_pool_out, w_attn_out, w_o, ln1_g, ln1_b, w_gate, w_up, w_down,
           ln2_g, ln2_b):
    bsz, seq, _ = x_prompt.shape
    db, ts, _ = x_sample.shape
    past_len = page_table.shape[1] * PAGE_SIZE
    nb = seq // MOBA_BLOCK
    nbs = past_len // MOBA_BLOCK
    assert seq % MOBA_BLOCK == 0 and past_len % MOBA_BLOCK == 0 and ts <= 8
    np_rows = bsz * seq
    ns_rows = db * ts

    far_s = _far_bucket(MOBA_BLOCK + 1, past_len + ts)
    c = np.arange(MOBA_BLOCK)
    t_idx = np.repeat(np.arange(ts), N_HEADS)
    h_idx = np.tile(np.arange(N_HEADS), ts)
    d_last = MOBA_BLOCK + t_idx[:, None] - c[None, :]
    b_last_s = rel_bias[_bucket_np(d_last), h_idx[:, None]]
    tn = np.arange(8)
    d_new = t_idx[:, None] - tn[None, :]
    b_own_s = jnp.where((d_new >= 0) & (tn[None, :] < ts),
                        rel_bias[_bucket_np(np.maximum(d_new, 0)), h_idx[:, None]], NEG_INF)
    cfar_s = rel_bias[far_s][h_idx][:, None]
    head_mask = jnp.asarray((np.arange(ATT_WIDTH)[None, :] // HEAD_DIM) == h_idx[:, None])
    cache_kt = jnp.transpose(cache_k, (0, 2, 3, 4, 1))
    cache_vt = jnp.transpose(cache_v, (0, 2, 3, 4, 1))

    wr_pad = jnp.zeros((D_MODEL, ROUTER_PAD), BF16).at[:, :N_EXPERTS].set(w_router.astype(BF16))
    br_pad = jnp.zeros((1, ROUTER_PAD), F32).at[0, :N_EXPERTS].set(b_router)

    xp = x_prompt.reshape(np_rows, D_MODEL)
    xs = x_sample.reshape(ns_rows, D_MODEL)
    pp, ksm, vsm, psm = [], [], [], []
    k_all = v_all = None
    for l in range(DEPTH):
        w_in_bf = w_in[l].astype(BF16)
        pool_w_bf = pool_w[l].astype(BF16)
        scale = pool_scale[l][None, :]
        wpo = w_pool_out[l].astype(BF16)
        wao = w_attn_out[l].astype(BF16)
        wo = w_o[l].astype(BF16)
        g1, b1 = ln1_g[l][None, :], ln1_b[l][None, :]
        g2, b2 = ln2_g[l][None, :], ln2_b[l][None, :]

        p_p, q_p, k_all, v_all, gp_p, ga_p, kb_p, vt_p, km_p = _inproj(
            xp, w_in_bf, tm=PROMPT_ROW_TILE, attn_layouts=True, seq=seq, layer=l,
            kv_all=None if l == 0 else (k_all, v_all))
        prev0 = jnp.zeros((bsz, POOL_HALO, POOL_WIDTH), F32)
        yp_p = _pool(p_p.reshape(bsz, seq, POOL_WIDTH), prev0, pool_w_bf, scale, tq=MOBA_BLOCK, pos0=0)
        o_p = _moba_prompt(q_p.reshape(bsz, seq, ATT_WIDTH),
                           kb_p.reshape(bsz, nb, MOBA_BLOCK, ATT_WIDTH),
                           vt_p.reshape(bsz, nb, ATT_WIDTH, MOBA_BLOCK),
                           km_p.reshape(bsz, nb, ATT_WIDTH), rel_bias)
        x1_p, rt_p = _merge(xp, yp_p.reshape(np_rows, POOL_WIDTH), o_p.reshape(np_rows, ATT_WIDTH),
                            gp_p, ga_p, wpo, wao, wo, g1, b1, wr_pad, br_pad, tm=PROMPT_ROW_TILE)

        p_s, q_s, k_s, v_s, gp_s, ga_s = _inproj(xs, w_in_bf, tm=ns_rows, attn_layouts=False)
        state = state_pool[:, l]
        prev_s = jnp.concatenate([jnp.zeros((db, 1, POOL_WIDTH), F32), state], axis=1)
        p_s3 = p_s.reshape(db, ts, POOL_WIDTH)
        p_s8 = jnp.pad(p_s3, ((0, 0), (0, 8 - ts), (0, 0)))
        yp_s = _pool(p_s8, prev_s, pool_w_bf, scale, tq=8, pos0=past_len)[:, :ts]
        qbd = jnp.where(head_mask[None], jnp.repeat(q_s.reshape(db, ts, ATT_WIDTH), N_HEADS, axis=1), 0.0)
        knew8 = jnp.pad(k_s.reshape(db, ts, ATT_WIDTH), ((0, 0), (0, 8 - ts), (0, 0)))
        vnew8 = jnp.pad(v_s.reshape(db, ts, ATT_WIDTH), ((0, 0), (0, 8 - ts), (0, 0)))
        o_s = _moba_sample(page_table, qbd, knew8, vnew8, cfar_s, b_last_s, b_own_s,
                           cache_kt, cache_vt, l)[:, :ts]
        x1_s, rt_s = _merge(xs, yp_s.reshape(ns_rows, POOL_WIDTH), o_s.reshape(ns_rows, ATT_WIDTH),
                            gp_s, ga_s, wpo, wao, wo, g1, b1, wr_pad, br_pad, tm=ns_rows)

        x1 = jnp.concatenate([x1_p, x1_s], axis=0)
        e_p, gt_p = _unpack_route(rt_p)
        e_s, gt_s = _unpack_route(rt_s)
        y0, y1 = _moe(x1, jnp.concatenate([e_p, e_s], axis=0), w_gate, w_up, w_down, l)
        xp, xs = _ln2(x1_p, x1_s, y0, y1, jnp.concatenate([gt_p, gt_s], axis=0), g2, b2, tm=ROW_TILE)

        pp.append(p_p.reshape(bsz, seq, POOL_WIDTH)[:, seq - POOL_BUF:])
        ksm.append(k_s.reshape(db, ts, N_HEADS, HEAD_DIM))
        vsm.append(v_s.reshape(db, ts, N_HEADS, HEAD_DIM))
        psm.append(jnp.concatenate([state, p_s3], axis=1)[:, -POOL_BUF:])

    def rows_out(a):
        return jnp.transpose(a.reshape(bsz, DEPTH, N_HEADS, HEAD_DIM, seq), (0, 4, 1, 2, 3))

    return (xp.reshape(bsz, seq, D_MODEL), xs.reshape(db, ts, D_MODEL),
            rows_out(k_all), rows_out(v_all), jnp.stack(pp, axis=1),
            jnp.stack(ksm, axis=2), jnp.stack(vsm, axis=2), jnp.stack(psm, axis=1))
```
